```python
import math
import jax, jax.numpy as jnp
from jax import lax
import numpy as np

D_MODEL = 1024
BATCH = 4
SEQ = 8192
DEPTH = 2

N_A = DEPTH // 2
N_B = DEPTH - N_A

A_HEADS = 8
A_QK_DIM = D_MODEL // 2 // A_HEADS
A_V_DIM = D_MODEL // A_HEADS
A_QK_W = A_HEADS * A_QK_DIM
A_V_W = A_HEADS * A_V_DIM
A_IN_WIDTH = 2 * A_QK_W + A_V_W + 2 * A_HEADS + A_V_W
A_CHUNK = 128

B_HEADS = 8
B_Q_LORA = 384
B_KV_LORA = 256
B_NOPE = 128
B_ROPE = 64
B_V = 128
B_QBLOCK = 128
ROPE_THETA = 10000.0

D_FF = 4 * D_MODEL
EPS = 1e-6

kernel_name = "yoco_mlstm_mla_sandwich_adaln"


def rmsnorm(x, g):
    xf = x.astype(jnp.float32)
    y = xf * lax.rsqrt(jnp.mean(xf * xf, axis=-1, keepdims=True) + EPS)
    return (y * g.astype(jnp.float32)).astype(x.dtype)


def modulate(h, shift, scale):
    return h * (1 + scale[:, None, :]) + shift[:, None, :]


def rope_tables(positions):
    half = B_ROPE // 2
    inv = ROPE_THETA ** (-jnp.arange(half, dtype=jnp.float32) / half)
    ang = positions.astype(jnp.float32)[..., None] * inv
    return jnp.cos(ang), jnp.sin(ang)


def apply_rope(x, cos, sin):
    xf = x.astype(jnp.float32)
    x1, x2 = jnp.split(xf, 2, axis=-1)
    out = jnp.concatenate([x1 * cos - x2 * sin, x2 * cos + x1 * sin], axis=-1)
    return out.astype(x.dtype)


def mlstm_chunkwise(q, k, v, i_pre, f_pre):
    B_, H, S, dk = q.shape
    dv = v.shape[-1]
    L = A_CHUNK
    nc = S // L
    f32 = jnp.float32
    qf = q.astype(f32)
    kf = k.astype(f32) * (dk ** -0.5)
    vf = v.astype(f32)
    lf = jax.nn.log_sigmoid(f_pre.astype(f32))
    ig = i_pre.astype(f32)

    def chunks(a):
        return jnp.moveaxis(a.reshape(B_, H, nc, L, *a.shape[3:]), 2, 0)

    qc, kc, vc, ic = chunks(qf), chunks(kf), chunks(vf), chunks(ig)
    bc = jnp.cumsum(chunks(lf), axis=-1)
    tri = jnp.tril(jnp.ones((L, L), dtype=bool))

    def step(carry, xs):
        C, n, m = carry
        qb, kb, vb, ib, bb = xs
        log_d = bb[..., :, None] - bb[..., None, :] + ib[..., None, :]
        log_d = jnp.where(tri, log_d, -jnp.inf)
        log_inter = bb + m[..., None]
        m_t = jnp.maximum(log_inter, jnp.max(log_d, axis=-1))
        w_intra = jnp.exp(log_d - m_t[..., None])
        w_inter = jnp.exp(log_inter - m_t)
        s = jnp.einsum('bhtd,bhsd->bhts', qb, kb) * w_intra
        num = (w_inter[..., None] * jnp.einsum('bhtd,bhde->bhte', qb, C)
               + jnp.einsum('bhts,bhse->bhte', s, vb))
        den = w_inter * jnp.einsum('bhtd,bhd->bht', qb, n) + jnp.sum(s, axis=-1)
        h = num / jnp.maximum(jnp.abs(den), jnp.exp(-m_t))[..., None]
        b_last = bb[..., -1]
        log_w = b_last[..., None] - bb + ib
        m_new = jnp.maximum(b_last + m, jnp.max(log_w, axis=-1))
        w = jnp.exp(log_w - m_new[..., None])
        decay = jnp.exp(b_last + m - m_new)
        C_new = decay[..., None, None] * C + jnp.einsum('bhs,bhsd,bhse->bhde', w, kb, vb)
        n_new = decay[..., None] * n + jnp.einsum('bhs,bhsd->bhd', w, kb)
        return (C_new, n_new, m_new), h

    init = (jnp.zeros((B_, H, dk, dv), f32), jnp.zeros((B_, H, dk), f32), jnp.zeros((B_, H), f32))
    _, hc = lax.scan(step, init, (qc, kc, vc, ic, bc))
    return jnp.moveaxis(hc, 0, 2).reshape(B_, H, S, dv).astype(q.dtype)


def mlstm_mixer(h, w_in, gate_b, head_g, w_out):
    B_, S, _ = h.shape
    proj = h @ w_in
    splits = [A_QK_W, 2 * A_QK_W, 2 * A_QK_W + A_V_W, 2 * A_QK_W + A_V_W + A_HEADS,
              2 * A_QK_W + A_V_W + 2 * A_HEADS]
    q, k, v, ig, fg, og = jnp.split(proj, splits, axis=-1)
    heads = lambda a, d: a.reshape(B_, S, A_HEADS, d).transpose(0, 2, 1, 3)
    i_pre = (ig + gate_b[0]).transpose(0, 2, 1)
    f_pre = (fg + gate_b[1]).transpose(0, 2, 1)
    hh = mlstm_chunkwise(heads(q, A_QK_DIM), heads(k, A_QK_DIM), heads(v, A_V_DIM), i_pre, f_pre)
    hh = rmsnorm(hh.transpose(0, 2, 1, 3), head_g).reshape(B_, S, A_V_W)
    return (hh * jax.nn.sigmoid(og)) @ w_out


def mla_shared_kv(x, shift, scale, g_in, w_a, g_latent, w_b, cos, sin):
    B_, S, _ = x.shape
    h = modulate(rmsnorm(x, g_in), shift, scale)
    kv = h @ w_a
    c_kv, k_rope = jnp.split(kv, [B_KV_LORA], axis=-1)
    c_kv = rmsnorm(c_kv, g_latent)
    kvb = (c_kv @ w_b).reshape(B_, S, B_HEADS, B_NOPE + B_V)
    k_nope, v = jnp.split(kvb, [B_NOPE], axis=-1)
    k_rope = apply_rope(k_rope, cos, sin)
    return k_nope, k_rope, v


def causal_mla_attention(q_nope, q_rope, k_nope, k_rope, v):
    B_, S, H, _ = q_nope.shape
    nb = S // B_QBLOCK
    scale = (B_NOPE + B_ROPE) ** -0.5
    qn = jnp.moveaxis(q_nope.reshape(B_, nb, B_QBLOCK, H, B_NOPE), 1, 0)
    qr = jnp.moveaxis(q_rope.reshape(B_, nb, B_QBLOCK, H, B_ROPE), 1, 0)
    kpos = jnp.arange(S, dtype=jnp.int32)
    starts = jnp.arange(nb, dtype=jnp.int32) * B_QBLOCK

    def block(args):
        qn_b, qr_b, start = args
        s = (jnp.einsum('bqhd,bkhd->bhqk', qn_b, k_nope)
             + jnp.einsum('bqhd,bkd->bhqk', qr_b, k_rope)).astype(jnp.float32) * scale
        qpos = start + jnp.arange(B_QBLOCK, dtype=jnp.int32)
        s = jnp.where(kpos[None, :] <= qpos[:, None], s, -jnp.inf)
        p = jax.nn.softmax(s, axis=-1)
        return jnp.einsum('bhqk,bkhd->bqhd', p.astype(v.dtype), v)

    o = lax.map(block, (qn, qr, starts))
    return jnp.moveaxis(o, 0, 1).reshape(B_, S, H, B_V)


def mla_mixer(h, w_q_a, g_q_latent, w_q_b, w_out, k_nope, k_rope, v, cos, sin):
    B_, S, _ = h.shape
    cq = rmsnorm(h @ w_q_a, g_q_latent)
    q = (cq @ w_q_b).reshape(B_, S, B_HEADS, B_NOPE + B_ROPE)
    q_nope, q_rope = jnp.split(q, [B_NOPE], axis=-1)
    q_rope = apply_rope(q_rope, cos[:, :, None, :], sin[:, :, None, :])
    o = causal_mla_attention(q_nope, q_rope, k_nope, k_rope, v)
    return o.reshape(B_, S, B_HEADS * B_V) @ w_out


def sqrelu_mlp(h, w1, w2):
    return jnp.square(jax.nn.relu(h @ w1)) @ w2


def setup_inputs(seed: int = 0) -> dict:
    key = jax.random.key(seed)
    ks = jax.random.split(key, 24)
    f32 = jnp.float32

    def nrm(k, shape, fan_in, s=1.0):
        return s * (fan_in ** -0.5) * jax.random.normal(k, shape, f32)

    def gain(k, shape):
        return 1.0 + 0.02 * jax.random.normal(k, shape, f32)

    x = jax.random.normal(ks[0], (BATCH, SEQ, D_MODEL), f32)
    c = jax.random.normal(ks[1], (BATCH, D_MODEL), f32)
    positions = (jax.random.randint(ks[2], (BATCH, 1), 0, 4096, dtype=jnp.int32)
                 + jnp.arange(SEQ, dtype=jnp.int32)[None, :])
    ada_w = nrm(ks[3], (DEPTH, D_MODEL, 6 * D_MODEL), D_MODEL, 0.5)
    ada_b = 0.02 * jax.random.normal(ks[4], (DEPTH, 6 * D_MODEL), f32)
    norm_g = gain(ks[5], (DEPTH, 4, D_MODEL))
    a_w_in = nrm(ks[6], (N_A, D_MODEL, A_IN_WIDTH), D_MODEL)
    kg1, kg2 = jax.random.split(ks[7])
    i_bias = 0.1 * jax.random.normal(kg1, (N_A, A_HEADS), f32)
    f_bias = jnp.linspace(3.0, 6.0, A_HEADS, dtype=f32)[None, :] + 0.1 * jax.random.normal(kg2, (N_A, A_HEADS), f32)
    a_gate_b = jnp.stack([i_bias, f_bias], axis=1)
    a_head_g = gain(ks[8], (N_A, A_HEADS, A_V_DIM))
    a_w_out = nrm(ks[9], (N_A, A_V_W, D_MODEL), A_V_W)
    kv_ada_w = nrm(ks[10], (D_MODEL, 2 * D_MODEL), D_MODEL, 0.5)
    kv_ada_b = 0.02 * jax.random.normal(ks[11], (2 * D_MODEL,), f32)
    kv_norm_g = gain(ks[12], (D_MODEL,))
    kv_w_a = nrm(ks[13], (D_MODEL, B_KV_LORA + B_ROPE), D_MODEL)
    kv_latent_g = gain(ks[14], (B_KV_LORA,))
    kv_w_b = nrm(ks[15], (B_KV_LORA, B_HEADS * (B_NOPE + B_V)), B_KV_LORA)
    b_w_q_a = nrm(ks[16], (N_B, D_MODEL, B_Q_LORA), D_MODEL)
    b_q_latent_g = gain(ks[17], (N_B, B_Q_LORA))
    b_w_q_b = nrm(ks[18], (N_B, B_Q_LORA, B_HEADS * (B_NOPE + B_ROPE)), B_Q_LORA)
    b_w_out = nrm(ks[19], (N_B, B_HEADS * B_V, D_MODEL), B_HEADS * B_V)
    mlp_w1 = nrm(ks[20], (DEPTH, D_MODEL, D_FF), D_MODEL)
    mlp_w2 = nrm(ks[21], (DEPTH, D_FF, D_MODEL), D_FF)
    return {"x": x, "c": c, "positions": positions, "ada_w": ada_w, "ada_b": ada_b,
            "norm_g": norm_g, "a_w_in": a_w_in, "a_gate_b": a_gate_b, "a_head_g": a_head_g,
            "a_w_out": a_w_out, "kv_ada_w": kv_ada_w, "kv_ada_b": kv_ada_b, "kv_norm_g": kv_norm_g,
            "kv_w_a": kv_w_a, "kv_latent_g": kv_latent_g, "kv_w_b": kv_w_b, "b_w_q_a": b_w_q_a,
            "b_q_latent_g": b_q_latent_g, "b_w_q_b": b_w_q_b, "b_w_out": b_w_out,
            "mlp_w1": mlp_w1, "mlp_w2": mlp_w2}


def reference(x, c, positions, ada_w, ada_b, norm_g, a_w_in, a_gate_b, a_head_g, a_w_out,
              kv_ada_w, kv_ada_b, kv_norm_g, kv_w_a, kv_latent_g, kv_w_b, b_w_q_a, b_q_latent_g,
              b_w_q_b, b_w_out, mlp_w1, mlp_w2):
    cond = jax.nn.silu(c)
    cos, sin = rope_tables(positions)
    shared_kv = None
    for l in range(DEPTH):
        ada = cond @ ada_w[l] + ada_b[l]
        sh1, sc1, g1, sh2, sc2, g2 = jnp.split(ada, 6, axis=-1)
        h = modulate(rmsnorm(x, norm_g[l, 0]), sh1, sc1)
        if l < N_A:
            y = mlstm_mixer(h, a_w_in[l], a_gate_b[l], a_head_g[l], a_w_out[l])
        else:
            if shared_kv is None:
                kv_shift, kv_scale = jnp.split(cond @ kv_ada_w + kv_ada_b, 2, axis=-1)
                shared_kv = mla_shared_kv(x, kv_shift, kv_scale, kv_norm_g, kv_w_a, kv_latent_g,
                                          kv_w_b, cos, sin)
            k_nope, k_rope, v = shared_kv
            j = l - N_A
            y = mla_mixer(h, b_w_q_a[j], b_q_latent_g[j], b_w_q_b[j], b_w_out[j],
                          k_nope, k_rope, v, cos, sin)
        x = x + g1[:, None, :] * rmsnorm(y, norm_g[l, 1])
        h = modulate(rmsnorm(x, norm_g[l, 2]), sh2, sc2)
        y = sqrelu_mlp(h, mlp_w1[l], mlp_w2[l])
        x = x + g2[:, None, :] * rmsnorm(y, norm_g[l, 3])
    return x
```

```python
import functools
import math

import jax
import jax.numpy as jnp
from jax import lax
from jax.experimental import pallas as pl
from jax.experimental.pallas import tpu as pltpu

F32 = jnp.float32
BF16 = jnp.bfloat16

D_MODEL = 1024
D_FF = 4 * D_MODEL
EPS = 1e-6

A_HEADS = 8
A_QK = 64
A_V = 128
A_QK_W = A_HEADS * A_QK
A_V_W = A_HEADS * A_V
A_CHUNK = 128

B_HEADS = 8
B_Q_LORA = 384
B_KV_LORA = 256
B_NOPE = 128
B_ROPE = 64
B_V = 128
ROPE_THETA = 10000.0
HEAD_CAT = 256

LANES = 128
VMEM_LIMIT = 56 * 1024 * 1024


def _dot(a, b):
    return jnp.dot(a, b, preferred_element_type=F32)


def _dot_nt(a, b):
    return lax.dot_general(a, b, (((1,), (1,)), ((), ())), preferred_element_type=F32)


def _dot_tn(a, b):
    return lax.dot_general(a, b, (((0,), (0,)), ((), ())), preferred_element_type=F32)


def _rms(x):
    return x * lax.rsqrt(jnp.mean(x * x, axis=-1, keepdims=True) + EPS)


def _const_spec(shape):
    nd = len(shape)
    return pl.BlockSpec(shape, lambda *_: (0,) * nd, pipeline_mode=pl.Buffered(1))


def _adaln_kernel(c_ref, w_ref, b_ref, o_ref):
    c = c_ref[...]
    cond = c * jax.nn.sigmoid(c)
    o_ref[...] = _dot(cond.astype(BF16), w_ref[...].astype(BF16)) + b_ref[...]


def _adaln(c8, w, b, tn):
    nl, d, n = w.shape
    return pl.pallas_call(
        _adaln_kernel,
        grid=(nl, n // tn),
        in_specs=[
            pl.BlockSpec((8, d), lambda l, j: (0, 0)),
            pl.BlockSpec((None, d, tn), lambda l, j: (l, 0, j)),
            pl.BlockSpec((None, 1, tn), lambda l, j: (l, 0, j)),
        ],
        out_specs=pl.BlockSpec((None, 8, tn), lambda l, j: (l, 0, j)),
        out_shape=jax.ShapeDtypeStruct((nl, 8, n), F32),
        compiler_params=pltpu.CompilerParams(
            dimension_semantics=("arbitrary", "arbitrary"), vmem_limit_bytes=VMEM_LIMIT),
        name="adaln",
    )(c8, w, b)


def _inproj_kernel(x_ref, g_ref, mod_ref, w_ref, q_ref, k_ref, v_ref, o_ref, gate_ref):
    h = _rms(x_ref[...]) * g_ref[...]
    h = (h * (1.0 + mod_ref[1:2, :]) + mod_ref[0:1, :]).astype(BF16)
    q_ref[...] = _dot(h, w_ref[:, 0:A_QK_W]).astype(BF16)
    k_ref[...] = (_dot(h, w_ref[:, A_QK_W:2 * A_QK_W]) * (A_QK ** -0.5)).astype(BF16)
    c0 = 2 * A_QK_W
    v_ref[...] = _dot(h, w_ref[:, c0:c0 + A_V_W]).astype(BF16)
    o_ref[...] = _dot(h, w_ref[:, c0 + A_V_W:c0 + 2 * A_V_W]).astype(BF16)
    gate_ref[...] = _dot(h, w_ref[:, c0 + 2 * A_V_W:])


def _inproj(x, g, mod, w, seq, tm):
    t, d = x.shape
    per_b = seq // tm
    n_gate = 2 * LANES
    row = lambda i: (i, 0)
    return pl.pallas_call(
        _inproj_kernel,
        grid=(t // tm,),
        in_specs=[
            pl.BlockSpec((tm, d), row),
            _const_spec((1, d)),
            pl.BlockSpec((None, 8, d), lambda i: (i // per_b, 0, 0)),
            _const_spec(w.shape),
        ],
        out_specs=[
            pl.BlockSpec((tm, A_QK_W), row),
            pl.BlockSpec((tm, A_QK_W), row),
            pl.BlockSpec((tm, A_V_W), row),
            pl.BlockSpec((tm, A_V_W), row),
            pl.BlockSpec((tm, n_gate), row),
        ],
        out_shape=[
            jax.ShapeDtypeStruct((t, A_QK_W), BF16),
            jax.ShapeDtypeStruct((t, A_QK_W), BF16),
            jax.ShapeDtypeStruct((t, A_V_W), BF16),
            jax.ShapeDtypeStruct((t, A_V_W), BF16),
            jax.ShapeDtypeStruct((t, n_gate), F32),
        ],
        compiler_params=pltpu.CompilerParams(
            dimension_semantics=("arbitrary",), vmem_limit_bytes=VMEM_LIMIT),
        name="inproj",
    )(x, g, mod, w)


def _scan_rows(x, op, fill):
    n = x.shape[0]
    rows = lax.broadcasted_iota(jnp.int32, x.shape, 0)
    sh = 1
    while sh < n:
        shifted = jnp.where(rows >= sh, pltpu.roll(x, sh, axis=0), fill)
        x = op(x, shifted)
        sh *= 2
    return x


def _mlstm_kernel(q_ref, k_ref, v_ref, og_ref, gate_ref, gb_ref, hg_ref, out_ref, c_ref, m_ref,
                  *, n_chunks):
    L = A_CHUNK

    @pl.when(pl.program_id(1) == 0)
    def _():
        c_ref[...] = jnp.zeros_like(c_ref)
        m_ref[...] = jnp.zeros_like(m_ref)

    rows = lax.broadcasted_iota(jnp.int32, (L, L), 0)
    cols = lax.broadcasted_iota(jnp.int32, (L, L), 1)
    tri = cols <= rows
    ones_blk = jnp.ones((L, LANES), BF16)

    def chunk(ci, carry):
        r0 = pl.multiple_of(ci * L, L)
        rs = pl.ds(r0, L)
        ig = gate_ref[rs, 0:LANES] + gb_ref[0:1, :]
        fg = gate_ref[rs, LANES:2 * LANES] + gb_ref[1:2, :]
        bb = _scan_rows(jax.nn.log_sigmoid(fg), jnp.add, 0.0)
        r = ig - bb
        cm = _scan_rows(r, jnp.maximum, -jnp.inf)
        m_prev = m_ref[...]
        mm = jnp.maximum(m_prev, cm)
        e_inv = jnp.exp(-(bb + mm))
        w_inter = jnp.exp(m_prev - mm)
        mm_last = mm[L - 1:L, :]
        w_upd = jnp.exp(r - mm_last)
        decay = jnp.exp(m_prev - mm_last)
        m_ref[...] = bb[L - 1:L, :] + mm_last
        r_t = r.T

        for h in range(A_HEADS):
            qh = q_ref[rs, h * A_QK:(h + 1) * A_QK]
            kh = k_ref[rs, h * A_QK:(h + 1) * A_QK]
            vh = v_ref[rs, h * A_V:(h + 1) * A_V]
            vext = jnp.concatenate([vh, ones_blk], axis=1)
            arg = jnp.where(tri, r_t[h:h + 1, :] - mm[:, h:h + 1], -jnp.inf)
            s = _dot_nt(qh, kh) * jnp.exp(arg)
            c_ext = c_ref[h]
            tot = w_inter[:, h:h + 1] * _dot(qh, c_ext.astype(BF16)) + _dot(s.astype(BF16), vext)
            num = tot[:, 0:A_V]
            den = tot[:, A_V:2 * A_V]
            hh = num / jnp.maximum(jnp.abs(den), e_inv[:, h:h + 1])
            hn = _rms(hh) * hg_ref[0:1, h * A_V:(h + 1) * A_V]
            og = og_ref[rs, h * A_V:(h + 1) * A_V].astype(F32)
            out_ref[rs, h * A_V:(h + 1) * A_V] = (hn * jax.nn.sigmoid(og)).astype(BF16)
            wv = (w_upd[:, h:h + 1] * vext.astype(F32)).astype(BF16)
            c_ref[h] = decay[:, h:h + 1] * c_ext + _dot_tn(kh, wv)
        return carry

    lax.fori_loop(0, n_chunks, chunk, 0)


def _mlstm(q, k, v, og, gates, gate_b, head_g, seq, tb):
    t = q.shape[0]
    per_b = seq // tb
    row = lambda b, i: (b * per_b + i, 0)
    return pl.pallas_call(
        functools.partial(_mlstm_kernel, n_chunks=tb // A_CHUNK),
        grid=(t // seq, per_b),
        in_specs=[
            pl.BlockSpec((tb, A_QK_W), row),
            pl.BlockSpec((tb, A_QK_W), row),
            pl.BlockSpec((tb, A_V_W), row),
            pl.BlockSpec((tb, A_V_W), row),
            pl.BlockSpec((tb, 2 * LANES), row),
            pl.BlockSpec((8, LANES), lambda b, i: (0, 0)),
            pl.BlockSpec((1, A_V_W), lambda b, i: (0, 0)),
        ],
        out_specs=pl.BlockSpec((tb, A_V_W), row),
        out_shape=jax.ShapeDtypeStruct((t, A_V_W), BF16),
        scratch_shapes=[
            pltpu.VMEM((A_HEADS, A_QK, 2 * A_V), F32),
            pltpu.VMEM((1, LANES), F32),
        ],
        compiler_params=pltpu.CompilerParams(
            dimension_semantics=("arbitrary", "arbitrary"), vmem_limit_bytes=VMEM_LIMIT),
        name="mlstm",
    )(q, k, v, og, gates, gate_b, head_g)


def _post_kernel(x_ref, mix_ref, mod_ref, ng_ref, wo_ref, w1_ref, w2_ref, out_ref, *, ff_chunk):
    y = _dot(mix_ref[...], wo_ref[...])
    x1 = x_ref[...] + mod_ref[2:3, :] * (_rms(y) * ng_ref[1:2, :])
    h = _rms(x1) * ng_ref[2:3, :]
    h = (h * (1.0 + mod_ref[4:5, :]) + mod_ref[3:4, :]).astype(BF16)
    acc = None
    for j in range(D_FF // ff_chunk):
        a = _dot(h, w1_ref[:, j * ff_chunk:(j + 1) * ff_chunk])
        a = jnp.square(jnp.maximum(a, 0.0)).astype(BF16)
        part = _dot(a, w2_ref[j * ff_chunk:(j + 1) * ff_chunk, :])
        acc = part if acc is None else acc + part
    out_ref[...] = x1 + mod_ref[5:6, :] * (_rms(acc) * ng_ref[3:4, :])


def _post(x, mix, mod, ng, wo, w1, w2, seq, tm):
    t, d = x.shape
    per_b = seq // tm
    row = lambda i: (i, 0)
    return pl.pallas_call(
        functools.partial(_post_kernel, ff_chunk=1024),
        grid=(t // tm,),
        in_specs=[
            pl.BlockSpec((tm, d), row),
            pl.BlockSpec((tm, d), row),
            pl.BlockSpec((None, 8, d), lambda i: (i // per_b, 0, 0)),
            _const_spec((4, d)),
            _const_spec(wo.shape),
            _const_spec(w1.shape),
            _const_spec(w2.shape),
        ],
        out_specs=pl.BlockSpec((tm, d), row),
        out_shape=jax.ShapeDtypeStruct((t, d), F32),
        compiler_params=pltpu.CompilerParams(
            dimension_semantics=("arbitrary",), vmem_limit_bytes=VMEM_LIMIT),
        name="post",
    )(x, mix, mod, ng, wo, w1, w2)


def _l1proj_kernel(x_ref, pos_ref, freq_ref, ng_ref, mod_ref, wa_ref, gl_ref, wb_ref,
                   wqa_ref, gq_ref, wqb_ref, q_ref, k_ref, v_ref, *, q_scale):
    xn = _rms(x_ref[...])
    ang = pos_ref[...].astype(F32) * freq_ref[0:1, :]
    cos_t = jnp.cos(ang)
    sin_t = jnp.sin(ang) * freq_ref[1:2, :]

    hk = xn * ng_ref[0:1, :]
    hk = (hk * (1.0 + mod_ref[1:2, :]) + mod_ref[0:1, :]).astype(BF16)
    kva = _dot(hk, wa_ref[...])
    ckv = (_rms(kva[:, 0:B_KV_LORA]) * gl_ref[...]).astype(BF16)
    k_rope = (kva[:, B_KV_LORA:B_KV_LORA + LANES] * cos_t
              + kva[:, B_KV_LORA + LANES:B_KV_LORA + 2 * LANES] * sin_t).astype(BF16)
    kvb = _dot(ckv, wb_ref[...])
    for h in range(B_HEADS):
        k_ref[:, h * HEAD_CAT:h * HEAD_CAT + B_NOPE] = kvb[:, h * 256:h * 256 + B_NOPE].astype(BF16)
        k_ref[:, h * HEAD_CAT + B_NOPE:(h + 1) * HEAD_CAT] = k_rope
        v_ref[:, h * B_V:(h + 1) * B_V] = kvb[:, h * 256 + B_NOPE:(h + 1) * 256].astype(BF16)

    hq = xn * ng_ref[1:2, :]
    hq = (hq * (1.0 + mod_ref[3:4, :]) + mod_ref[2:3, :]).astype(BF16)
    cq = (_rms(_dot(hq, wqa_ref[...])) * gq_ref[...]).astype(BF16)
    nw = B_HEADS * B_NOPE
    qn = _dot(cq, wqb_ref[:, 0:nw])
    qr = _dot(cq, wqb_ref[:, nw:2 * nw])
    qs = _dot(cq, wqb_ref[:, 2 * nw:3 * nw])
    for h in range(B_HEADS):
        sl = slice(h * LANES, (h + 1) * LANES)
        q_ref[:, h * HEAD_CAT:h * HEAD_CAT + B_NOPE] = (qn[:, sl] * q_scale).astype(BF16)
        q_ref[:, h * HEAD_CAT + B_NOPE:(h + 1) * HEAD_CAT] = (
            (qr[:, sl] * cos_t + qs[:, sl] * sin_t) * q_scale).astype(BF16)


def _l1proj(x, pos, freq, ng, mod, wa, gl, wb, wqa, gq, wqb, seq, tm):
    t, d = x.shape
    per_b = seq // tm
    row = lambda i: (i, 0)
    q_scale = float((B_NOPE + B_ROPE) ** -0.5)
    return pl.pallas_call(
        functools.partial(_l1proj_kernel, q_scale=q_scale),
        grid=(t // tm,),
        in_specs=[
            pl.BlockSpec((tm, d), row),
            pl.BlockSpec((tm, 1), row),
            _const_spec(freq.shape),
            _const_spec(ng.shape),
            pl.BlockSpec((None, 8, d), lambda i: (i // per_b, 0, 0)),
            _const_spec(wa.shape),
            _const_spec(gl.shape),
            _const_spec(wb.shape),
            _const_spec(wqa.shape),
            _const_spec(gq.shape),
            _const_spec(wqb.shape),
        ],
        out_specs=[
            pl.BlockSpec((tm, B_HEADS * HEAD_CAT), row),
            pl.BlockSpec((tm, B_HEADS * HEAD_CAT), row),
            pl.BlockSpec((tm, B_HEADS * B_V), row),
        ],
        out_shape=[
            jax.ShapeDtypeStruct((t, B_HEADS * HEAD_CAT), BF16),
            jax.ShapeDtypeStruct((t, B_HEADS * HEAD_CAT), BF16),
            jax.ShapeDtypeStruct((t, B_HEADS * B_V), BF16),
        ],
        compiler_params=pltpu.CompilerParams(
            dimension_semantics=("arbitrary",), vmem_limit_bytes=VMEM_LIMIT),
        name="l1proj",
    )(x, pos, freq, ng, mod, wa, gl, wb, wqa, gq, wqb)


def _attn_kernel(q_ref, k_ref, v_ref, o_ref, vext_ref, *, tq):
    qi = pl.program_id(2)

    @pl.when(qi == 0)
    def _():
        vext_ref[:, 0:B_V] = v_ref[...]
        vext_ref[:, B_V:2 * B_V] = jnp.ones((v_ref.shape[0], B_V), BF16)

    q = q_ref[...]

    def block(j, carry, masked):
        m, acc = carry
        ks = pl.ds(pl.multiple_of(j * tq, tq), tq)
        s = _dot_nt(q, k_ref[ks, :])
        if masked:
            rows = lax.broadcasted_iota(jnp.int32, (tq, tq), 0)
            cols = lax.broadcasted_iota(jnp.int32, (tq, tq), 1)
            s = jnp.where(cols <= rows, s, -jnp.inf)
        m_new = jnp.maximum(m, jnp.max(s, axis=-1, keepdims=True))
        p = jnp.exp(s - m_new)
        acc = jnp.exp(m - m_new) * acc + _dot(p.astype(BF16), vext_ref[ks, :])
        return m_new, acc

    init = (jnp.full((tq, 1), -jnp.inf, F32), jnp.zeros((tq, 2 * B_V), F32))
    carry = lax.fori_loop(0, qi, functools.partial(block, masked=False), init)
    _, acc = block(qi, carry, masked=True)
    o_ref[...] = (acc[:, 0:B_V] / acc[:, B_V:2 * B_V]).astype(BF16)


def _attn(q, k, v, tq):
    b, s, _ = q.shape
    return pl.pallas_call(
        functools.partial(_attn_kernel, tq=tq),
        grid=(b, B_HEADS, s // tq),
        in_specs=[
            pl.BlockSpec((None, tq, HEAD_CAT), lambda bi, h, i: (bi, i, h)),
            pl.BlockSpec((None, s, HEAD_CAT), lambda bi, h, i: (bi, 0, h)),
            pl.BlockSpec((None, s, B_V), lambda bi, h, i: (bi, 0, h)),
        ],
        out_specs=pl.BlockSpec((None, tq, B_V), lambda bi, h, i: (bi, i, h)),
        out_shape=jax.ShapeDtypeStruct((b, s, B_HEADS * B_V), BF16),
        scratch_shapes=[pltpu.VMEM((s, 2 * B_V), BF16)],
        compiler_params=pltpu.CompilerParams(
            dimension_semantics=("arbitrary", "arbitrary", "arbitrary"),
            vmem_limit_bytes=VMEM_LIMIT),
        name="attn",
    )(q, k, v)


def _swap_halves(w):
    half = w.shape[-1] // 2
    return jnp.concatenate([w[..., half:], w[..., :half]], axis=-1)


def _pad_cols(w, n):
    return jnp.pad(w, ((0, 0), (0, n - w.shape[1])))


def kernel(x, c, positions, ada_w, ada_b, norm_g, a_w_in, a_gate_b, a_head_g, a_w_out,
           kv_ada_w, kv_ada_b, kv_norm_g, kv_w_a, kv_latent_g, kv_w_b, b_w_q_a, b_q_latent_g,
           b_w_q_b, b_w_out, mlp_w1, mlp_w2):
    bsz, seq, d = x.shape
    t = bsz * seq
    x2d = x.reshape(t, d)

    c8 = jnp.pad(c, ((0, 8 - bsz), (0, 0)))
    ada = _adaln(c8, ada_w, ada_b[:, None, :], tn=1536)[:, :bsz]
    kv_ada = _adaln(c8, kv_ada_w[None], kv_ada_b[None, None, :], tn=1024)[0, :bsz]

    def mod_rows(vecs):
        rows = [v.reshape(bsz, 1, d) for v in vecs]
        rows += [jnp.zeros((bsz, 1, d), F32)] * (8 - len(rows))
        return jnp.concatenate(rows, axis=1)

    ada0 = [ada[0][:, i * d:(i + 1) * d] for i in range(6)]
    ada1 = [ada[1][:, i * d:(i + 1) * d] for i in range(6)]
    kv_shift, kv_scale = kv_ada[:, :d], kv_ada[:, d:]

    w_in = a_w_in[0]
    c0 = 2 * A_QK_W + A_V_W
    w_in_cat = jnp.concatenate([
        w_in[:, :c0],
        w_in[:, c0 + 2 * A_HEADS:],
        _pad_cols(w_in[:, c0:c0 + A_HEADS], LANES),
        _pad_cols(w_in[:, c0 + A_HEADS:c0 + 2 * A_HEADS], LANES),
    ], axis=1).astype(BF16)
    q, k, v, og, gates = _inproj(x2d, norm_g[0, 0][None], mod_rows(ada0[:2]), w_in_cat, seq, tm=512)
    gate_b = jnp.pad(a_gate_b[0], ((0, 6), (0, LANES - A_HEADS)))
    mix0 = _mlstm(q, k, v, og, gates, gate_b, a_head_g[0].reshape(1, A_V_W), seq, tb=512)
    x2d = _post(x2d, mix0, mod_rows(ada0), norm_g[0], a_w_out[0].astype(BF16),
                mlp_w1[0].astype(BF16), mlp_w2[0].astype(BF16), seq, tm=512)

    rope = kv_w_a[:, B_KV_LORA:]
    wa_cat = jnp.concatenate([
        kv_w_a[:, :B_KV_LORA], _pad_cols(rope, LANES), _pad_cols(_swap_halves(rope), LANES),
    ], axis=1).astype(BF16)
    wqb = b_w_q_b[0].reshape(B_Q_LORA, B_HEADS, B_NOPE + B_ROPE)
    wq_rope = wqb[:, :, B_NOPE:]
    pad_r = lambda w: jnp.pad(w, ((0, 0), (0, 0), (0, LANES - B_ROPE))).reshape(B_Q_LORA, -1)
    wqb_cat = jnp.concatenate([
        wqb[:, :, :B_NOPE].reshape(B_Q_LORA, -1), pad_r(wq_rope), pad_r(_swap_halves(wq_rope)),
    ], axis=1).astype(BF16)
    half = B_ROPE // 2
    inv = ROPE_THETA ** (-jnp.arange(half, dtype=F32) / half)
    freq = jnp.zeros((8, LANES), F32)
    freq = freq.at[0].set(jnp.tile(inv, LANES // half))
    freq = freq.at[1].set(jnp.tile(jnp.concatenate([-jnp.ones(half, F32), jnp.ones(half, F32)]),
                                   LANES // B_ROPE))
    ng1 = jnp.concatenate([kv_norm_g[None], norm_g[1, 0][None]], axis=0)
    mod1 = mod_rows([kv_shift, kv_scale, ada1[0], ada1[1]])
    q_cat, k_cat, v1 = _l1proj(
        x2d, positions.reshape(t, 1), freq, ng1, mod1, wa_cat, kv_latent_g[None],
        kv_w_b.astype(BF16), b_w_q_a[0].astype(BF16), b_q_latent_g[0][None], wqb_cat, seq, tm=512)
    o = _attn(q_cat.reshape(bsz, seq, -1), k_cat.reshape(bsz, seq, -1),
              v1.reshape(bsz, seq, -1), tq=512)
    x2d = _post(x2d, o.reshape(t, -1), mod_rows(ada1), norm_g[1], b_w_out[0].astype(BF16),
                mlp_w1[1].astype(BF16), mlp_w2[1].astype(BF16), seq, tm=512)
    return x2d.reshape(bsz, seq, d)
```

```python
import functools
import math

import jax
import jax.numpy as jnp
from jax import lax
from jax.experimental import pallas as pl
from jax.experimental.pallas import tpu as pltpu

F32 = jnp.float32
BF16 = jnp.bfloat16

D_MODEL = 1024
D_FF = 4 * D_MODEL
EPS = 1e-6

A_HEADS = 8
A_QK = 64
A_V = 128
A_QK_W = A_HEADS * A_QK
A_V_W = A_HEADS * A_V
A_CHUNK = 128

B_HEADS = 8
B_Q_LORA = 384
B_KV_LORA = 256
B_NOPE = 128
B_ROPE = 64
B_V = 128
ROPE_THETA = 10000.0
HEAD_CAT = 256

LANES = 128
VMEM_LIMIT = 56 * 1024 * 1024


def _dot(a, b):
    return jnp.dot(a, b, preferred_element_type=F32)


def _dot_nt(a, b):
    return lax.dot_general(a, b, (((1,), (1,)), ((), ())), preferred_element_type=F32)


def _dot_tn(a, b):
    return lax.dot_general(a, b, (((0,), (0,)), ((), ())), preferred_element_type=F32)


def _rms(x):
    return x * lax.rsqrt(jnp.mean(x * x, axis=-1, keepdims=True) + EPS)


def _const_spec(shape):
    nd = len(shape)
    return pl.BlockSpec(shape, lambda *_: (0,) * nd, pipeline_mode=pl.Buffered(1))


def _adaln_kernel(c_ref, w_ref, b_ref, o_ref):
    c = c_ref[...]
    cond = c * jax.nn.sigmoid(c)
    o_ref[...] = _dot(cond.astype(BF16), w_ref[...].astype(BF16)) + b_ref[...]


def _adaln(c8, w, b, tn):
    nl, d, n = w.shape
    return pl.pallas_call(
        _adaln_kernel,
        grid=(nl, n // tn),
        in_specs=[
            pl.BlockSpec((8, d), lambda l, j: (0, 0)),
            pl.BlockSpec((None, d, tn), lambda l, j: (l, 0, j)),
            pl.BlockSpec((None, 1, tn), lambda l, j: (l, 0, j)),
        ],
        out_specs=pl.BlockSpec((None, 8, tn), lambda l, j: (l, 0, j)),
        out_shape=jax.ShapeDtypeStruct((nl, 8, n), F32),
        compiler_params=pltpu.CompilerParams(
            dimension_semantics=("arbitrary", "arbitrary"), vmem_limit_bytes=VMEM_LIMIT),
        name="adaln",
    )(c8, w, b)


def _inproj_kernel(x_ref, g_ref, mod_ref, w_ref, q_ref, k_ref, v_ref, o_ref, gate_ref):
    h = _rms(x_ref[...]) * g_ref[...]
    h = (h * (1.0 + mod_ref[1:2, :]) + mod_ref[0:1, :]).astype(BF16)
    q_ref[...] = _dot(h, w_ref[:, 0:A_QK_W]).astype(BF16)
    k_ref[...] = (_dot(h, w_ref[:, A_QK_W:2 * A_QK_W]) * (A_QK ** -0.5)).astype(BF16)
    c0 = 2 * A_QK_W
    v_ref[...] = _dot(h, w_ref[:, c0:c0 + A_V_W]).astype(BF16)
    o_ref[...] = _dot(h, w_ref[:, c0 + A_V_W:c0 + 2 * A_V_W]).astype(BF16)
    gate_ref[...] = _dot(h, w_ref[:, c0 + 2 * A_V_W:])


def _inproj(x, g, mod, w, seq, tm):
    t, d = x.shape
    per_b = seq // tm
    n_gate = 2 * LANES
    row = lambda i: (i, 0)
    return pl.pallas_call(
        _inproj_kernel,
        grid=(t // tm,),
        in_specs=[
            pl.BlockSpec((tm, d), row),
            _const_spec((1, d)),
            pl.BlockSpec((None, 8, d), lambda i: (i // per_b, 0, 0)),
            _const_spec(w.shape),
        ],
        out_specs=[
            pl.BlockSpec((tm, A_QK_W), row),
            pl.BlockSpec((tm, A_QK_W), row),
            pl.BlockSpec((tm, A_V_W), row),
            pl.BlockSpec((tm, A_V_W), row),
            pl.BlockSpec((tm, n_gate), row),
        ],
        out_shape=[
            jax.ShapeDtypeStruct((t, A_QK_W), BF16),
            jax.ShapeDtypeStruct((t, A_QK_W), BF16),
            jax.ShapeDtypeStruct((t, A_V_W), BF16),
            jax.ShapeDtypeStruct((t, A_V_W), BF16),
            jax.ShapeDtypeStruct((t, n_gate), F32),
        ],
        compiler_params=pltpu.CompilerParams(
            dimension_semantics=("arbitrary",), vmem_limit_bytes=VMEM_LIMIT),
        name="inproj",
    )(x, g, mod, w)


def _scan_rows(x, op, fill):
    n = x.shape[0]
    rows = lax.broadcasted_iota(jnp.int32, x.shape, 0)
    sh = 1
    while sh < n:
        shifted = jnp.where(rows >= sh, pltpu.roll(x, sh, axis=0), fill)
        x = op(x, shifted)
        sh *= 2
    return x


def _mlstm_kernel(q_ref, k_ref, v_ref, og_ref, gate_ref, gb_ref, hg_ref, out_ref, c_ref, m_ref,
                  *, n_chunks):
    L = A_CHUNK

    @pl.when(pl.program_id(1) == 0)
    def _():
        c_ref[...] = jnp.zeros_like(c_ref)
        m_ref[...] = jnp.zeros_like(m_ref)

    rows = lax.broadcasted_iota(jnp.int32, (L, L), 0)
    cols = lax.broadcasted_iota(jnp.int32, (L, L), 1)
    tri = cols <= rows
    ones_blk = jnp.ones((L, LANES), BF16)

    def chunk(ci, carry):
        r0 = pl.multiple_of(ci * L, L)
        rs = pl.ds(r0, L)
        ig = gate_ref[rs, 0:LANES] + gb_ref[0:1, :]
        fg = gate_ref[rs, LANES:2 * LANES] + gb_ref[1:2, :]
        bb = _scan_rows(jax.nn.log_sigmoid(fg), jnp.add, 0.0)
        r = ig - bb
        cm = _scan_rows(r, jnp.maximum, -jnp.inf)
        m_prev = m_ref[...]
        mm = jnp.maximum(m_prev, cm)
        e_inv = jnp.exp(-(bb + mm))
        w_inter = jnp.exp(m_prev - mm)
        mm_last = mm[L - 1:L, :]
        w_upd = jnp.exp(r - mm_last)
        decay = jnp.exp(m_prev - mm_last)
        m_ref[...] = bb[L - 1:L, :] + mm_last
        r_t = r.T

        for h in range(A_HEADS):
            qh = q_ref[rs, h * A_QK:(h + 1) * A_QK]
            kh = k_ref[rs, h * A_QK:(h + 1) * A_QK]
            vh = v_ref[rs, h * A_V:(h + 1) * A_V]
            vext = jnp.concatenate([vh, ones_blk], axis=1)
            arg = jnp.where(tri, r_t[h:h + 1, :] - mm[:, h:h + 1], -jnp.inf)
            s = _dot_nt(qh, kh) * jnp.exp(arg)
            c_ext = c_ref[h]
            tot = w_inter[:, h:h + 1] * _dot(qh, c_ext.astype(BF16)) + _dot(s.astype(BF16), vext)
            num = tot[:, 0:A_V]
            den = tot[:, A_V:2 * A_V]
            hh = num / jnp.maximum(jnp.abs(den), e_inv[:, h:h + 1])
            hn = _rms(hh) * hg_ref[0:1, h * A_V:(h + 1) * A_V]
            og = og_ref[rs, h * A_V:(h + 1) * A_V].astype(F32)
            out_ref[rs, h * A_V:(h + 1) * A_V] = (hn * jax.nn.sigmoid(og)).astype(BF16)
            wv = (w_upd[:, h:h + 1] * vext.astype(F32)).astype(BF16)
            c_ref[h] = decay[:, h:h + 1] * c_ext + _dot_tn(kh, wv)
        return carry

    lax.fori_loop(0, n_chunks, chunk, 0)


def _mlstm(q, k, v, og, gates, gate_b, head_g, seq, tb):
    t = q.shape[0]
    per_b = seq // tb
    row = lambda b, i: (b * per_b + i, 0)
    return pl.pallas_call(
        functools.partial(_mlstm_kernel, n_chunks=tb // A_CHUNK),
        grid=(t // seq, per_b),
        in_specs=[
            pl.BlockSpec((tb, A_QK_W), row),
            pl.BlockSpec((tb, A_QK_W), row),
            pl.BlockSpec((tb, A_V_W), row),
            pl.BlockSpec((tb, A_V_W), row),
            pl.BlockSpec((tb, 2 * LANES), row),
            pl.BlockSpec((8, LANES), lambda b, i: (0, 0)),
            pl.BlockSpec((1, A_V_W), lambda b, i: (0, 0)),
        ],
        out_specs=pl.BlockSpec((tb, A_V_W), row),
        out_shape=jax.ShapeDtypeStruct((t, A_V_W), BF16),
        scratch_shapes=[
            pltpu.VMEM((A_HEADS, A_QK, 2 * A_V), F32),
            pltpu.VMEM((1, LANES), F32),
        ],
        compiler_params=pltpu.CompilerParams(
            dimension_semantics=("arbitrary", "arbitrary"), vmem_limit_bytes=VMEM_LIMIT),
        name="mlstm",
    )(q, k, v, og, gates, gate_b, head_g)


def _post_kernel(x_ref, mix_ref, mod_ref, ng_ref, wo_ref, w1_ref, w2_ref, out_ref, *, ff_chunk):
    y = _dot(mix_ref[...], wo_ref[...])
    x1 = x_ref[...] + mod_ref[2:3, :] * (_rms(y) * ng_ref[1:2, :])
    h = _rms(x1) * ng_ref[2:3, :]
    h = (h * (1.0 + mod_ref[4:5, :]) + mod_ref[3:4, :]).astype(BF16)
    acc = None
    for j in range(D_FF // ff_chunk):
        a = _dot(h, w1_ref[:, j * ff_chunk:(j + 1) * ff_chunk])
        a = jnp.square(jnp.maximum(a, 0.0)).astype(BF16)
        part = _dot(a, w2_ref[j * ff_chunk:(j + 1) * ff_chunk, :])
        acc = part if acc is None else acc + part
    out_ref[...] = x1 + mod_ref[5:6, :] * (_rms(acc) * ng_ref[3:4, :])


def _post(x, mix, mod, ng, wo, w1, w2, seq, tm):
    t, d = x.shape
    per_b = seq // tm
    row = lambda i: (i, 0)
    return pl.pallas_call(
        functools.partial(_post_kernel, ff_chunk=1024),
        grid=(t // tm,),
        in_specs=[
            pl.BlockSpec((tm, d), row),
            pl.BlockSpec((tm, d), row),
            pl.BlockSpec((None, 8, d), lambda i: (i // per_b, 0, 0)),
            _const_spec((4, d)),
            _const_spec(wo.shape),
            _const_spec(w1.shape),
            _const_spec(w2.shape),
        ],
        out_specs=pl.BlockSpec((tm, d), row),
        out_shape=jax.ShapeDtypeStruct((t, d), F32),
        compiler_params=pltpu.CompilerParams(
            dimension_semantics=("arbitrary",), vmem_limit_bytes=VMEM_LIMIT),
        name="post",
    )(x, mix, mod, ng, wo, w1, w2)


def _l1proj_kernel(x_ref, pos_ref, freq_ref, ng_ref, mod_ref, wa_ref, gl_ref, wb_ref,
                   wqa_ref, gq_ref, wqb_ref, q_ref, k_ref, v_ref, *, q_scale):
    xn = _rms(x_ref[...])
    ang = pos_ref[...].astype(F32) * freq_ref[0:1, :]
    cos_t = jnp.cos(ang)
    sin_t = jnp.sin(ang) * freq_ref[1:2, :]

    hk = xn * ng_ref[0:1, :]
    hk = (hk * (1.0 + mod_ref[1:2, :]) + mod_ref[0:1, :]).astype(BF16)
    kva = _dot(hk, wa_ref[...])
    ckv = (_rms(kva[:, 0:B_KV_LORA]) * gl_ref[...]).astype(BF16)
    k_rope = (kva[:, B_KV_LORA:B_KV_LORA + LANES] * cos_t
              + kva[:, B_KV_LORA + LANES:B_KV_LORA + 2 * LANES] * sin_t).astype(BF16)
    kvb = _dot(ckv, wb_ref[...])
    for h in range(B_HEADS):
        k_ref[:, h * HEAD_CAT:h * HEAD_CAT + B_NOPE] = kvb[:, h * 256:h * 256 + B_NOPE].astype(BF16)
        k_ref[:, h * HEAD_CAT + B_NOPE:(h + 1) * HEAD_CAT] = k_rope
        v_ref[:, h * B_V:(h + 1) * B_V] = kvb[:, h * 256 + B_NOPE:(h + 1) * 256].astype(BF16)

    hq = xn * ng_ref[1:2, :]
    hq = (hq * (1.0 + mod_ref[3:4, :]) + mod_ref[2:3, :]).astype(BF16)
    cq = (_rms(_dot(hq, wqa_ref[...])) * gq_ref[...]).astype(BF16)
    nw = B_HEADS * B_NOPE
    qn = _dot(cq, wqb_ref[:, 0:nw])
    qr = _dot(cq, wqb_ref[:, nw:2 * nw])
    qs = _dot(cq, wqb_ref[:, 2 * nw:3 * nw])
    for h in range(B_HEADS):
        sl = slice(h * LANES, (h + 1) * LANES)
        q_ref[:, h * HEAD_CAT:h * HEAD_CAT + B_NOPE] = (qn[:, sl] * q_scale).astype(BF16)
        q_ref[:, h * HEAD_CAT + B_NOPE:(h + 1) * HEAD_CAT] = (
            (qr[:, sl] * cos_t + qs[:, sl] * sin_t) * q_scale).astype(BF16)


def _l1proj(x, pos, freq, ng, mod, wa, gl, wb, wqa, gq, wqb, seq, tm):
    t, d = x.shape
    per_b = seq // tm
    row = lambda i: (i, 0)
    q_scale = float((B_NOPE + B_ROPE) ** -0.5 * math.log2(math.e))
    return pl.pallas_call(
        functools.partial(_l1proj_kernel, q_scale=q_scale),
        grid=(t // tm,),
        in_specs=[
            pl.BlockSpec((tm, d), row),
            pl.BlockSpec((tm, 1), row),
            _const_spec(freq.shape),
            _const_spec(ng.shape),
            pl.BlockSpec((None, 8, d), lambda i: (i // per_b, 0, 0)),
            _const_spec(wa.shape),
            _const_spec(gl.shape),
            _const_spec(wb.shape),
            _const_spec(wqa.shape),
            _const_spec(gq.shape),
            _const_spec(wqb.shape),
        ],
        out_specs=[
            pl.BlockSpec((tm, B_HEADS * HEAD_CAT), row),
            pl.BlockSpec((tm, B_HEADS * HEAD_CAT), row),
            pl.BlockSpec((tm, B_HEADS * B_V), row),
        ],
        out_shape=[
            jax.ShapeDtypeStruct((t, B_HEADS * HEAD_CAT), BF16),
            jax.ShapeDtypeStruct((t, B_HEADS * HEAD_CAT), BF16),
            jax.ShapeDtypeStruct((t, B_HEADS * B_V), BF16),
        ],
        compiler_params=pltpu.CompilerParams(
            dimension_semantics=("arbitrary",), vmem_limit_bytes=VMEM_LIMIT),
        name="l1proj",
    )(x, pos, freq, ng, mod, wa, gl, wb, wqa, gq, wqb)


def _attn_kernel(q_ref, k_ref, v_ref, o_ref, vext_ref, s_ref, *, tq):
    qi = pl.program_id(2)

    @pl.when(qi == 0)
    def _():
        vext_ref[:, 0:B_V] = v_ref[...]
        vext_ref[:, B_V:2 * B_V] = jnp.ones((v_ref.shape[0], B_V), BF16)

    q = q_ref[...]

    def kv_rows(j):
        return pl.ds(pl.multiple_of(j * tq, tq), tq)

    def scores(j):
        return _dot_nt(q, k_ref[kv_rows(j), :])

    def soft_pv(j, s, m, acc):
        m_new = jnp.maximum(m, jnp.max(s, axis=-1, keepdims=True))
        p = jnp.exp2(s - m_new)
        acc = jnp.exp2(m - m_new) * acc + _dot(p.astype(BF16), vext_ref[kv_rows(j), :])
        return m_new, acc

    s_ref[0] = scores(0)

    def body(j, carry):
        m, acc = carry
        slot = j % 2
        m, acc = soft_pv(j, s_ref[slot], m, acc)
        s_ref[1 - slot] = scores(j + 1)
        return m, acc

    init = (jnp.full((tq, 1), -jnp.inf, F32), jnp.zeros((tq, 2 * B_V), F32))
    m, acc = lax.fori_loop(0, qi, body, init)
    rows = lax.broadcasted_iota(jnp.int32, (tq, tq), 0)
    cols = lax.broadcasted_iota(jnp.int32, (tq, tq), 1)
    s = jnp.where(cols <= rows, s_ref[qi % 2], -jnp.inf)
    _, acc = soft_pv(qi, s, m, acc)
    o_ref[...] = (acc[:, 0:B_V] / acc[:, B_V:2 * B_V]).astype(BF16)


def _attn(q, k, v, tq):
    b, s, _ = q.shape
    return pl.pallas_call(
        functools.partial(_attn_kernel, tq=tq),
        grid=(b, B_HEADS, s // tq),
        in_specs=[
            pl.BlockSpec((None, tq, HEAD_CAT), lambda bi, h, i: (bi, i, h)),
            pl.BlockSpec((None, s, HEAD_CAT), lambda bi, h, i: (bi, 0, h)),
            pl.BlockSpec((None, s, B_V), lambda bi, h, i: (bi, 0, h)),
        ],
        out_specs=pl.BlockSpec((None, tq, B_V), lambda bi, h, i: (bi, i, h)),
        out_shape=jax.ShapeDtypeStruct((b, s, B_HEADS * B_V), BF16),
        scratch_shapes=[pltpu.VMEM((s, 2 * B_V), BF16), pltpu.VMEM((2, tq, tq), F32)],
        compiler_params=pltpu.CompilerParams(
            dimension_semantics=("arbitrary", "arbitrary", "arbitrary"),
            vmem_limit_bytes=VMEM_LIMIT),
        name="attn",
    )(q, k, v)


def _swap_halves(w):
    half = w.shape[-1] // 2
    return jnp.concatenate([w[..., half:], w[..., :half]], axis=-1)


def _pad_cols(w, n):
    return jnp.pad(w, ((0, 0), (0, n - w.shape[1])))


def kernel(x, c, positions, ada_w, ada_b, norm_g, a_w_in, a_gate_b, a_head_g, a_w_out,
           kv_ada_w, kv_ada_b, kv_norm_g, kv_w_a, kv_latent_g, kv_w_b, b_w_q_a, b_q_latent_g,
           b_w_q_b, b_w_out, mlp_w1, mlp_w2):
    bsz, seq, d = x.shape
    t = bsz * seq
    x2d = x.reshape(t, d)

    c8 = jnp.pad(c, ((0, 8 - bsz), (0, 0)))
    ada = _adaln(c8, ada_w, ada_b[:, None, :], tn=1536)[:, :bsz]
    kv_ada = _adaln(c8, kv_ada_w[None], kv_ada_b[None, None, :], tn=1024)[0, :bsz]

    def mod_rows(vecs):
        rows = [v.reshape(bsz, 1, d) for v in vecs]
        rows += [jnp.zeros((bsz, 1, d), F32)] * (8 - len(rows))
        return jnp.concatenate(rows, axis=1)

    ada0 = [ada[0][:, i * d:(i + 1) * d] for i in range(6)]
    ada1 = [ada[1][:, i * d:(i + 1) * d] for i in range(6)]
    kv_shift, kv_scale = kv_ada[:, :d], kv_ada[:, d:]

    w_in = a_w_in[0]
    c0 = 2 * A_QK_W + A_V_W
    w_in_cat = jnp.concatenate([
        w_in[:, :c0],
        w_in[:, c0 + 2 * A_HEADS:],
        _pad_cols(w_in[:, c0:c0 + A_HEADS], LANES),
        _pad_cols(w_in[:, c0 + A_HEADS:c0 + 2 * A_HEADS], LANES),
    ], axis=1).astype(BF16)
    q, k, v, og, gates = _inproj(x2d, norm_g[0, 0][None], mod_rows(ada0[:2]), w_in_cat, seq, tm=512)
    gate_b = jnp.pad(a_gate_b[0], ((0, 6), (0, LANES - A_HEADS)))
    mix0 = _mlstm(q, k, v, og, gates, gate_b, a_head_g[0].reshape(1, A_V_W), seq, tb=512)
    x2d = _post(x2d, mix0, mod_rows(ada0), norm_g[0], a_w_out[0].astype(BF16),
                mlp_w1[0].astype(BF16), mlp_w2[0].astype(BF16), seq, tm=512)

    rope = kv_w_a[:, B_KV_LORA:]
    wa_cat = jnp.concatenate([
        kv_w_a[:, :B_KV_LORA], _pad_cols(rope, LANES), _pad_cols(_swap_halves(rope), LANES),
    ], axis=1).astype(BF16)
    wqb = b_w_q_b[0].reshape(B_Q_LORA, B_HEADS, B_NOPE + B_ROPE)
    wq_rope = wqb[:, :, B_NOPE:]
    pad_r = lambda w: jnp.pad(w, ((0, 0), (0, 0), (0, LANES - B_ROPE))).reshape(B_Q_LORA, -1)
    wqb_cat = jnp.concatenate([
        wqb[:, :, :B_NOPE].reshape(B_Q_LORA, -1), pad_r(wq_rope), pad_r(_swap_halves(wq_rope)),
    ], axis=1).astype(BF16)
    half = B_ROPE // 2
    inv = ROPE_THETA ** (-jnp.arange(half, dtype=F32) / half)
    freq = jnp.zeros((8, LANES), F32)
    freq = freq.at[0].set(jnp.tile(inv, LANES // half))
    freq = freq.at[1].set(jnp.tile(jnp.concatenate([-jnp.ones(half, F32), jnp.ones(half, F32)]),
                                   LANES // B_ROPE))
    ng1 = jnp.concatenate([kv_norm_g[None], norm_g[1, 0][None]], axis=0)
    mod1 = mod_rows([kv_shift, kv_scale, ada1[0], ada1[1]])
    q_cat, k_cat, v1 = _l1proj(
        x2d, positions.reshape(t, 1), freq, ng1, mod1, wa_cat, kv_latent_g[None],
        kv_w_b.astype(BF16), b_w_q_a[0].astype(BF16), b_q_latent_g[0][None], wqb_cat, seq, tm=512)
    o = _attn(q_cat.reshape(bsz, seq, -1), k_cat.reshape(bsz, seq, -1),
              v1.reshape(bsz, seq, -1), tq=1024)
    x2d = _post(x2d, o.reshape(t, -1), mod_rows(ada1), norm_g[1], b_w_out[0].astype(BF16),
                mlp_w1[1].astype(BF16), mlp_w2[1].astype(BF16), seq, tm=512)
    return x2d.reshape(bsz, seq, d)
```

```python
import functools
import math

import jax
import jax.numpy as jnp
from jax import lax
from jax.experimental import pallas as pl
from jax.experimental.pallas import tpu as pltpu

F32 = jnp.float32
BF16 = jnp.bfloat16

D_MODEL = 1024
D_FF = 4 * D_MODEL
EPS = 1e-6

A_HEADS = 8
A_QK = 64
A_V = 128
A_QK_W = A_HEADS * A_QK
A_V_W = A_HEADS * A_V
A_CHUNK = 128

B_HEADS = 8
B_Q_LORA = 384
B_KV_LORA = 256
B_NOPE = 128
B_ROPE = 64
B_V = 128
ROPE_THETA = 10000.0
HEAD_CAT = 256

LANES = 128
VMEM_LIMIT = 56 * 1024 * 1024


def _dot(a, b):
    return jnp.dot(a, b, preferred_element_type=F32)


def _dot_nt(a, b):
    return lax.dot_general(a, b, (((1,), (1,)), ((), ())), preferred_element_type=F32)


def _dot_tn(a, b):
    return lax.dot_general(a, b, (((0,), (0,)), ((), ())), preferred_element_type=F32)


def _rms(x):
    return x * lax.rsqrt(jnp.mean(x * x, axis=-1, keepdims=True) + EPS)


def _const_spec(shape):
    nd = len(shape)
    return pl.BlockSpec(shape, lambda *_: (0,) * nd, pipeline_mode=pl.Buffered(1))


def _adaln_kernel(c_ref, w_ref, b_ref, o_ref):
    c = c_ref[...]
    cond = c * jax.nn.sigmoid(c)
    o_ref[...] = _dot(cond.astype(BF16), w_ref[...].astype(BF16)) + b_ref[...]


def _adaln(c8, w, b, tn):
    nl, d, n = w.shape
    return pl.pallas_call(
        _adaln_kernel,
        grid=(nl, n // tn),
        in_specs=[
            pl.BlockSpec((8, d), lambda l, j: (0, 0)),
            pl.BlockSpec((None, d, tn), lambda l, j: (l, 0, j)),
            pl.BlockSpec((None, 1, tn), lambda l, j: (l, 0, j)),
        ],
        out_specs=pl.BlockSpec((None, 8, tn), lambda l, j: (l, 0, j)),
        out_shape=jax.ShapeDtypeStruct((nl, 8, n), F32),
        compiler_params=pltpu.CompilerParams(
            dimension_semantics=("arbitrary", "arbitrary"), vmem_limit_bytes=VMEM_LIMIT),
        name="adaln",
    )(c8, w, b)


def _inproj_kernel(x_ref, g_ref, mod_ref, w_ref, q_ref, k_ref, v_ref, o_ref, gate_ref):
    h = _rms(x_ref[...]) * g_ref[...]
    h = (h * (1.0 + mod_ref[1:2, :]) + mod_ref[0:1, :]).astype(BF16)
    q_ref[...] = _dot(h, w_ref[:, 0:A_QK_W]).astype(BF16)
    k_ref[...] = (_dot(h, w_ref[:, A_QK_W:2 * A_QK_W]) * (A_QK ** -0.5)).astype(BF16)
    c0 = 2 * A_QK_W
    v_ref[...] = _dot(h, w_ref[:, c0:c0 + A_V_W]).astype(BF16)
    o_ref[...] = _dot(h, w_ref[:, c0 + A_V_W:c0 + 2 * A_V_W]).astype(BF16)
    gate_ref[...] = _dot(h, w_ref[:, c0 + 2 * A_V_W:])


def _inproj(x, g, mod, w, seq, tm):
    t, d = x.shape
    per_b = seq // tm
    n_gate = 2 * LANES
    row = lambda i: (i, 0)
    return pl.pallas_call(
        _inproj_kernel,
        grid=(t // tm,),
        in_specs=[
            pl.BlockSpec((tm, d), row),
            _const_spec((1, d)),
            pl.BlockSpec((None, 8, d), lambda i: (i // per_b, 0, 0)),
            _const_spec(w.shape),
        ],
        out_specs=[
            pl.BlockSpec((tm, A_QK_W), row),
            pl.BlockSpec((tm, A_QK_W), row),
            pl.BlockSpec((tm, A_V_W), row),
            pl.BlockSpec((tm, A_V_W), row),
            pl.BlockSpec((tm, n_gate), row),
        ],
        out_shape=[
            jax.ShapeDtypeStruct((t, A_QK_W), BF16),
            jax.ShapeDtypeStruct((t, A_QK_W), BF16),
            jax.ShapeDtypeStruct((t, A_V_W), BF16),
            jax.ShapeDtypeStruct((t, A_V_W), BF16),
            jax.ShapeDtypeStruct((t, n_gate), F32),
        ],
        compiler_params=pltpu.CompilerParams(
            dimension_semantics=("arbitrary",), vmem_limit_bytes=VMEM_LIMIT),
        name="inproj",
    )(x, g, mod, w)


def _scan_rows(x, op, fill):
    n = x.shape[0]
    rows = lax.broadcasted_iota(jnp.int32, x.shape, 0)
    sh = 1
    while sh < n:
        shifted = jnp.where(rows >= sh, pltpu.roll(x, sh, axis=0), fill)
        x = op(x, shifted)
        sh *= 2
    return x


def _mlstm_kernel(q_ref, k_ref, v_ref, og_ref, gate_ref, gb_ref, hg_ref, out_ref, c_ref, m_ref,
                  *, n_chunks):
    L = A_CHUNK

    @pl.when(pl.program_id(1) == 0)
    def _():
        c_ref[...] = jnp.zeros_like(c_ref)
        m_ref[...] = jnp.zeros_like(m_ref)

    rows = lax.broadcasted_iota(jnp.int32, (L, L), 0)
    cols = lax.broadcasted_iota(jnp.int32, (L, L), 1)
    tri = cols <= rows
    ones_blk = jnp.ones((L, LANES), BF16)

    def chunk(ci, carry):
        r0 = pl.multiple_of(ci * L, L)
        rs = pl.ds(r0, L)
        ig = gate_ref[rs, 0:LANES] + gb_ref[0:1, :]
        fg = gate_ref[rs, LANES:2 * LANES] + gb_ref[1:2, :]
        bb = _scan_rows(jax.nn.log_sigmoid(fg), jnp.add, 0.0)
        r = ig - bb
        cm = _scan_rows(r, jnp.maximum, -jnp.inf)
        m_prev = m_ref[...]
        mm = jnp.maximum(m_prev, cm)
        e_inv = jnp.exp(-(bb + mm))
        w_inter = jnp.exp(m_prev - mm)
        mm_last = mm[L - 1:L, :]
        w_upd = jnp.exp(r - mm_last)
        decay = jnp.exp(m_prev - mm_last)
        m_ref[...] = bb[L - 1:L, :] + mm_last
        r_t = r.T

        for h in range(A_HEADS):
            qh = q_ref[rs, h * A_QK:(h + 1) * A_QK]
            kh = k_ref[rs, h * A_QK:(h + 1) * A_QK]
            vh = v_ref[rs, h * A_V:(h + 1) * A_V]
            vext = jnp.concatenate([vh, ones_blk], axis=1)
            arg = jnp.where(tri, r_t[h:h + 1, :] - mm[:, h:h + 1], -jnp.inf)
            s = _dot_nt(qh, kh) * jnp.exp(arg)
            c_ext = c_ref[h]
            tot = w_inter[:, h:h + 1] * _dot(qh, c_ext.astype(BF16)) + _dot(s.astype(BF16), vext)
            num = tot[:, 0:A_V]
            den = tot[:, A_V:2 * A_V]
            hh = num / jnp.maximum(jnp.abs(den), e_inv[:, h:h + 1])
            hn = _rms(hh) * hg_ref[0:1, h * A_V:(h + 1) * A_V]
            og = og_ref[rs, h * A_V:(h + 1) * A_V].astype(F32)
            out_ref[rs, h * A_V:(h + 1) * A_V] = (hn * jax.nn.sigmoid(og)).astype(BF16)
            wv = (w_upd[:, h:h + 1] * vext.astype(F32)).astype(BF16)
            c_ref[h] = decay[:, h:h + 1] * c_ext + _dot_tn(kh, wv)
        return carry

    lax.fori_loop(0, n_chunks, chunk, 0)


def _mlstm(q, k, v, og, gates, gate_b, head_g, seq, tb):
    t = q.shape[0]
    per_b = seq // tb
    row = lambda b, i: (b * per_b + i, 0)
    return pl.pallas_call(
        functools.partial(_mlstm_kernel, n_chunks=tb // A_CHUNK),
        grid=(t // seq, per_b),
        in_specs=[
            pl.BlockSpec((tb, A_QK_W), row),
            pl.BlockSpec((tb, A_QK_W), row),
            pl.BlockSpec((tb, A_V_W), row),
            pl.BlockSpec((tb, A_V_W), row),
            pl.BlockSpec((tb, 2 * LANES), row),
            pl.BlockSpec((8, LANES), lambda b, i: (0, 0)),
            pl.BlockSpec((1, A_V_W), lambda b, i: (0, 0)),
        ],
        out_specs=pl.BlockSpec((tb, A_V_W), row),
        out_shape=jax.ShapeDtypeStruct((t, A_V_W), BF16),
        scratch_shapes=[
            pltpu.VMEM((A_HEADS, A_QK, 2 * A_V), F32),
            pltpu.VMEM((1, LANES), F32),
        ],
        compiler_params=pltpu.CompilerParams(
            dimension_semantics=("arbitrary", "arbitrary"), vmem_limit_bytes=VMEM_LIMIT),
        name="mlstm",
    )(q, k, v, og, gates, gate_b, head_g)


def _post_kernel(x_ref, mix_ref, mod_ref, ng_ref, wo_ref, w1_ref, w2_ref, out_ref, *, ff_chunk):
    y = _dot(mix_ref[...], wo_ref[...])
    x1 = x_ref[...] + mod_ref[2:3, :] * (_rms(y) * ng_ref[1:2, :])
    h = _rms(x1) * ng_ref[2:3, :]
    h = (h * (1.0 + mod_ref[4:5, :]) + mod_ref[3:4, :]).astype(BF16)
    acc = None
    for j in range(D_FF // ff_chunk):
        a = _dot(h, w1_ref[:, j * ff_chunk:(j + 1) * ff_chunk])
        a = jnp.square(jnp.maximum(a, 0.0)).astype(BF16)
        part = _dot(a, w2_ref[j * ff_chunk:(j + 1) * ff_chunk, :])
        acc = part if acc is None else acc + part
    out_ref[...] = x1 + mod_ref[5:6, :] * (_rms(acc) * ng_ref[3:4, :])


def _post(x, mix, mod, ng, wo, w1, w2, seq, tm):
    t, d = x.shape
    per_b = seq // tm
    row = lambda i: (i, 0)
    return pl.pallas_call(
        functools.partial(_post_kernel, ff_chunk=1024),
        grid=(t // tm,),
        in_specs=[
            pl.BlockSpec((tm, d), row),
            pl.BlockSpec((tm, d), row),
            pl.BlockSpec((None, 8, d), lambda i: (i // per_b, 0, 0)),
            _const_spec((4, d)),
            _const_spec(wo.shape),
            _const_spec(w1.shape),
            _const_spec(w2.shape),
        ],
        out_specs=pl.BlockSpec((tm, d), row),
        out_shape=jax.ShapeDtypeStruct((t, d), F32),
        compiler_params=pltpu.CompilerParams(
            dimension_semantics=("arbitrary",), vmem_limit_bytes=VMEM_LIMIT),
        name="post",
    )(x, mix, mod, ng, wo, w1, w2)


def _l1proj_kernel(x_ref, pos_ref, freq_ref, ng_ref, mod_ref, wa_ref, gl_ref, wb_ref,
                   wqa_ref, gq_ref, wqb_ref, q_ref, k_ref, v_ref, *, q_scale):
    xn = _rms(x_ref[...])
    ang = pos_ref[...].astype(F32) * freq_ref[0:1, :]
    cos_t = jnp.cos(ang)
    sin_t = jnp.sin(ang) * freq_ref[1:2, :]

    hk = xn * ng_ref[0:1, :]
    hk = (hk * (1.0 + mod_ref[1:2, :]) + mod_ref[0:1, :]).astype(BF16)
    kva = _dot(hk, wa_ref[...])
    ckv = (_rms(kva[:, 0:B_KV_LORA]) * gl_ref[...]).astype(BF16)
    k_rope = (kva[:, B_KV_LORA:B_KV_LORA + LANES] * cos_t
              + kva[:, B_KV_LORA + LANES:B_KV_LORA + 2 * LANES] * sin_t).astype(BF16)
    kvb = _dot(ckv, wb_ref[...])
    for h in range(B_HEADS):
        k_ref[:, h * HEAD_CAT:h * HEAD_CAT + B_NOPE] = kvb[:, h * 256:h * 256 + B_NOPE].astype(BF16)
        k_ref[:, h * HEAD_CAT + B_NOPE:(h + 1) * HEAD_CAT] = k_rope
        v_ref[:, h * B_V:(h + 1) * B_V] = kvb[:, h * 256 + B_NOPE:(h + 1) * 256].astype(BF16)

    hq = xn * ng_ref[1:2, :]
    hq = (hq * (1.0 + mod_ref[3:4, :]) + mod_ref[2:3, :]).astype(BF16)
    cq = (_rms(_dot(hq, wqa_ref[...])) * gq_ref[...]).astype(BF16)
    nw = B_HEADS * B_NOPE
    qn = _dot(cq, wqb_ref[:, 0:nw])
    qr = _dot(cq, wqb_ref[:, nw:2 * nw])
    qs = _dot(cq, wqb_ref[:, 2 * nw:3 * nw])
    for h in range(B_HEADS):
        sl = slice(h * LANES, (h + 1) * LANES)
        q_ref[:, h * HEAD_CAT:h * HEAD_CAT + B_NOPE] = (qn[:, sl] * q_scale).astype(BF16)
        q_ref[:, h * HEAD_CAT + B_NOPE:(h + 1) * HEAD_CAT] = (
            (qr[:, sl] * cos_t + qs[:, sl] * sin_t) * q_scale).astype(BF16)


def _l1proj(x, pos, freq, ng, mod, wa, gl, wb, wqa, gq, wqb, seq, tm):
    t, d = x.shape
    per_b = seq // tm
    row = lambda i: (i, 0)
    q_scale = float((B_NOPE + B_ROPE) ** -0.5 * math.log2(math.e))
    return pl.pallas_call(
        functools.partial(_l1proj_kernel, q_scale=q_scale),
        grid=(t // tm,),
        in_specs=[
            pl.BlockSpec((tm, d), row),
            pl.BlockSpec((tm, 1), row),
            _const_spec(freq.shape),
            _const_spec(ng.shape),
            pl.BlockSpec((None, 8, d), lambda i: (i // per_b, 0, 0)),
            _const_spec(wa.shape),
            _const_spec(gl.shape),
            _const_spec(wb.shape),
            _const_spec(wqa.shape),
            _const_spec(gq.shape),
            _const_spec(wqb.shape),
        ],
        out_specs=[
            pl.BlockSpec((tm, B_HEADS * HEAD_CAT), row),
            pl.BlockSpec((tm, B_HEADS * HEAD_CAT), row),
            pl.BlockSpec((tm, B_HEADS * B_V), row),
        ],
        out_shape=[
            jax.ShapeDtypeStruct((t, B_HEADS * HEAD_CAT), BF16),
            jax.ShapeDtypeStruct((t, B_HEADS * HEAD_CAT), BF16),
            jax.ShapeDtypeStruct((t, B_HEADS * B_V), BF16),
        ],
        compiler_params=pltpu.CompilerParams(
            dimension_semantics=("arbitrary",), vmem_limit_bytes=VMEM_LIMIT),
        name="l1proj",
    )(x, pos, freq, ng, mod, wa, gl, wb, wqa, gq, wqb)


V_ROWS_T = B_V + 16


def _attn_kernel(q_ref, k_ref, v_ref, o_ref, vt_ref, sa_ref, sb_ref, qt_ref, *, tq):
    qi = pl.program_id(2)

    @pl.when(qi == 0)
    def _():
        for i in range(v_ref.shape[0] // tq):
            vt_ref[i, 0:B_V, :] = v_ref[i * tq:(i + 1) * tq, :].astype(F32).T.astype(BF16)
            vt_ref[i, B_V:V_ROWS_T, :] = jnp.ones((V_ROWS_T - B_V, tq), BF16)

    qt_ref[...] = q_ref[...].astype(F32).T.astype(BF16)

    def scores(j):
        rows = pl.ds(pl.multiple_of(j * tq, tq), tq)
        return _dot(k_ref[rows, :], qt_ref[...])

    def soft_pv(j, s, m, acc, masked=False):
        if masked:
            keys = lax.broadcasted_iota(jnp.int32, (tq, tq), 0)
            qrys = lax.broadcasted_iota(jnp.int32, (tq, tq), 1)
            s = jnp.where(keys <= qrys, s, -jnp.inf)
        m_new = jnp.maximum(m, jnp.max(s, axis=0, keepdims=True))
        p = jnp.exp2(s - m_new).astype(BF16)
        acc = jnp.exp2(m - m_new) * acc + _dot(vt_ref[j], p)
        return m_new, acc

    sa_ref[...] = scores(0)

    def pair(jj, carry):
        m, acc = carry
        j0 = 2 * jj
        sb_ref[...] = scores(j0 + 1)
        m, acc = soft_pv(j0, sa_ref[...], m, acc)
        sa_ref[...] = scores(j0 + 2)
        return soft_pv(j0 + 1, sb_ref[...], m, acc)

    init = (jnp.full((1, tq), -jnp.inf, F32), jnp.zeros((V_ROWS_T, tq), F32))
    m, acc = lax.fori_loop(0, qi // 2, pair, init)

    def odd_tail(m, acc):
        sb_ref[...] = scores(qi)
        m, acc = soft_pv(qi - 1, sa_ref[...], m, acc)
        return soft_pv(qi, sb_ref[...], m, acc, masked=True)

    def even_tail(m, acc):
        return soft_pv(qi, sa_ref[...], m, acc, masked=True)

    _, acc = lax.cond(qi % 2 == 1, odd_tail, even_tail, m, acc)
    o = acc[0:B_V, :] / acc[B_V:B_V + 1, :]
    o_ref[...] = o.T.astype(BF16)


def _attn(q, k, v, tq):
    b, s, _ = q.shape
    return pl.pallas_call(
        functools.partial(_attn_kernel, tq=tq),
        grid=(b, B_HEADS, s // tq),
        in_specs=[
            pl.BlockSpec((None, tq, HEAD_CAT), lambda bi, h, i: (bi, i, h)),
            pl.BlockSpec((None, s, HEAD_CAT), lambda bi, h, i: (bi, 0, h)),
            pl.BlockSpec((None, s, B_V), lambda bi, h, i: (bi, 0, h)),
        ],
        out_specs=pl.BlockSpec((None, tq, B_V), lambda bi, h, i: (bi, i, h)),
        out_shape=jax.ShapeDtypeStruct((b, s, B_HEADS * B_V), BF16),
        scratch_shapes=[
            pltpu.VMEM((s // tq, V_ROWS_T, tq), BF16),
            pltpu.VMEM((tq, tq), F32),
            pltpu.VMEM((tq, tq), F32),
            pltpu.VMEM((HEAD_CAT, tq), BF16),
        ],
        compiler_params=pltpu.CompilerParams(
            dimension_semantics=("arbitrary", "arbitrary", "arbitrary"),
            vmem_limit_bytes=VMEM_LIMIT),
        name="attn",
    )(q, k, v)


def _swap_halves(w):
    half = w.shape[-1] // 2
    return jnp.concatenate([w[..., half:], w[..., :half]], axis=-1)


def _pad_cols(w, n):
    return jnp.pad(w, ((0, 0), (0, n - w.shape[1])))


def kernel(x, c, positions, ada_w, ada_b, norm_g, a_w_in, a_gate_b, a_head_g, a_w_out,
           kv_ada_w, kv_ada_b, kv_norm_g, kv_w_a, kv_latent_g, kv_w_b, b_w_q_a, b_q_latent_g,
           b_w_q_b, b_w_out, mlp_w1, mlp_w2):
    bsz, seq, d = x.shape
    t = bsz * seq
    x2d = x.reshape(t, d)

    c8 = jnp.pad(c, ((0, 8 - bsz), (0, 0)))
    ada = _adaln(c8, ada_w, ada_b[:, None, :], tn=1536)[:, :bsz]
    kv_ada = _adaln(c8, kv_ada_w[None], kv_ada_b[None, None, :], tn=1024)[0, :bsz]

    def mod_rows(vecs):
        rows = [v.reshape(bsz, 1, d) for v in vecs]
        rows += [jnp.zeros((bsz, 1, d), F32)] * (8 - len(rows))
        return jnp.concatenate(rows, axis=1)

    ada0 = [ada[0][:, i * d:(i + 1) * d] for i in range(6)]
    ada1 = [ada[1][:, i * d:(i + 1) * d] for i in range(6)]
    kv_shift, kv_scale = kv_ada[:, :d], kv_ada[:, d:]

    w_in = a_w_in[0]
    c0 = 2 * A_QK_W + A_V_W
    w_in_cat = jnp.concatenate([
        w_in[:, :c0],
        w_in[:, c0 + 2 * A_HEADS:],
        _pad_cols(w_in[:, c0:c0 + A_HEADS], LANES),
        _pad_cols(w_in[:, c0 + A_HEADS:c0 + 2 * A_HEADS], LANES),
    ], axis=1).astype(BF16)
    q, k, v, og, gates = _inproj(x2d, norm_g[0, 0][None], mod_rows(ada0[:2]), w_in_cat, seq, tm=512)
    gate_b = jnp.pad(a_gate_b[0], ((0, 6), (0, LANES - A_HEADS)))
    mix0 = _mlstm(q, k, v, og, gates, gate_b, a_head_g[0].reshape(1, A_V_W), seq, tb=512)
    x2d = _post(x2d, mix0, mod_rows(ada0), norm_g[0], a_w_out[0].astype(BF16),
                mlp_w1[0].astype(BF16), mlp_w2[0].astype(BF16), seq, tm=512)

    rope = kv_w_a[:, B_KV_LORA:]
    wa_cat = jnp.concatenate([
        kv_w_a[:, :B_KV_LORA], _pad_cols(rope, LANES), _pad_cols(_swap_halves(rope), LANES),
    ], axis=1).astype(BF16)
    wqb = b_w_q_b[0].reshape(B_Q_LORA, B_HEADS, B_NOPE + B_ROPE)
    wq_rope = wqb[:, :, B_NOPE:]
    pad_r = lambda w: jnp.pad(w, ((0, 0), (0, 0), (0, LANES - B_ROPE))).reshape(B_Q_LORA, -1)
    wqb_cat = jnp.concatenate([
        wqb[:, :, :B_NOPE].reshape(B_Q_LORA, -1), pad_r(wq_rope), pad_r(_swap_halves(wq_rope)),
    ], axis=1).astype(BF16)
    half = B_ROPE // 2
    inv = ROPE_THETA ** (-jnp.arange(half, dtype=F32) / half)
    freq = jnp.zeros((8, LANES), F32)
    freq = freq.at[0].set(jnp.tile(inv, LANES // half))
    freq = freq.at[1].set(jnp.tile(jnp.concatenate([-jnp.ones(half, F32), jnp.ones(half, F32)]),
                                   LANES // B_ROPE))
    ng1 = jnp.concatenate([kv_norm_g[None], norm_g[1, 0][None]], axis=0)
    mod1 = mod_rows([kv_shift, kv_scale, ada1[0], ada1[1]])
    q_cat, k_cat, v1 = _l1proj(
        x2d, positions.reshape(t, 1), freq, ng1, mod1, wa_cat, kv_latent_g[None],
        kv_w_b.astype(BF16), b_w_q_a[0].astype(BF16), b_q_latent_g[0][None], wqb_cat, seq, tm=512)
    o = _attn(q_cat.reshape(bsz, seq, -1), k_cat.reshape(bsz, seq, -1),
              v1.reshape(bsz, seq, -1), tq=1024)
    x2d = _post(x2d, o.reshape(t, -1), mod_rows(ada1), norm_g[1], b_w_out[0].astype(BF16),
                mlp_w1[1].astype(BF16), mlp_w2[1].astype(BF16), seq, tm=512)
    return x2d.reshape(bsz, seq, d)
```

```python
import functools
import math

import jax
import jax.numpy as jnp
from jax import lax
from jax.experimental import pallas as pl
from jax.experimental.pallas import tpu as pltpu

F32 = jnp.float32
BF16 = jnp.bfloat16

D_MODEL = 1024
D_FF = 4 * D_MODEL
EPS = 1e-6

A_HEADS = 8
A_QK = 64
A_V = 128
A_QK_W = A_HEADS * A_QK
A_V_W = A_HEADS * A_V
MLSTM_CHUNK = 256
MLSTM_TB = 1024

B_HEADS = 8
B_Q_LORA = 384
B_KV_LORA = 256
B_NOPE = 128
B_ROPE = 64
B_V = 128
ROPE_THETA = 10000.0
HEAD_CAT = 256

LANES = 128
VMEM_LIMIT = 56 * 1024 * 1024


def _dot(a, b):
    return jnp.dot(a, b, preferred_element_type=F32)


def _dot_nt(a, b):
    return lax.dot_general(a, b, (((1,), (1,)), ((), ())), preferred_element_type=F32)


def _dot_tn(a, b):
    return lax.dot_general(a, b, (((0,), (0,)), ((), ())), preferred_element_type=F32)


def _rms(x):
    return x * lax.rsqrt(jnp.mean(x * x, axis=-1, keepdims=True) + EPS)


def _const_spec(shape):
    nd = len(shape)
    return pl.BlockSpec(shape, lambda *_: (0,) * nd, pipeline_mode=pl.Buffered(1))


def _adaln_kernel(c_ref, w_ref, b_ref, o_ref):
    c = c_ref[...]
    cond = c * jax.nn.sigmoid(c)
    o_ref[...] = _dot(cond.astype(BF16), w_ref[...].astype(BF16)) + b_ref[...]


def _adaln(c8, w, b, tn):
    nl, d, n = w.shape
    return pl.pallas_call(
        _adaln_kernel,
        grid=(nl, n // tn),
        in_specs=[
            pl.BlockSpec((8, d), lambda l, j: (0, 0)),
            pl.BlockSpec((None, d, tn), lambda l, j: (l, 0, j)),
            pl.BlockSpec((None, 1, tn), lambda l, j: (l, 0, j)),
        ],
        out_specs=pl.BlockSpec((None, 8, tn), lambda l, j: (l, 0, j)),
        out_shape=jax.ShapeDtypeStruct((nl, 8, n), F32),
        compiler_params=pltpu.CompilerParams(
            dimension_semantics=("arbitrary", "arbitrary"), vmem_limit_bytes=VMEM_LIMIT),
        name="adaln",
    )(c8, w, b)


A_GATE_ROWS = 2 * A_HEADS
WT_Q0, WT_V0, WT_O0, WT_G0 = 0, A_QK_W, A_QK_W + A_V_W, A_QK_W + 2 * A_V_W
WT_ROWS = WT_G0 + A_GATE_ROWS


def _inproj_kernel(x_ref, g_ref, mod_ref, wk_ref, wt_ref, k_ref, qt_ref, vt_ref, ot_ref, gt_ref):
    h = _rms(x_ref[...]) * g_ref[...]
    h = (h * (1.0 + mod_ref[1:2, :]) + mod_ref[0:1, :]).astype(BF16)
    k_ref[...] = (_dot(h, wk_ref[...]) * (A_QK ** -0.5)).astype(BF16)
    qt_ref[...] = _dot_nt(wt_ref[WT_Q0:WT_V0, :], h).astype(BF16)
    vt_ref[...] = _dot_nt(wt_ref[WT_V0:WT_O0, :], h).astype(BF16)
    ot_ref[...] = _dot_nt(wt_ref[WT_O0:WT_G0, :], h).astype(BF16)
    gt_ref[...] = _dot_nt(wt_ref[WT_G0:WT_ROWS, :], h)


def _inproj(x, g, mod, wk, wt, seq, tm):
    t, d = x.shape
    bsz = t // seq
    per_b = seq // tm
    row = lambda i: (i, 0)
    col = lambda i: (i // per_b, 0, i % per_b)
    return pl.pallas_call(
        _inproj_kernel,
        grid=(t // tm,),
        in_specs=[
            pl.BlockSpec((tm, d), row),
            _const_spec((1, d)),
            pl.BlockSpec((None, 8, d), lambda i: (i // per_b, 0, 0)),
            _const_spec(wk.shape),
            _const_spec(wt.shape),
        ],
        out_specs=[
            pl.BlockSpec((tm, A_QK_W), row),
            pl.BlockSpec((None, A_QK_W, tm), col),
            pl.BlockSpec((None, A_V_W, tm), col),
            pl.BlockSpec((None, A_V_W, tm), col),
            pl.BlockSpec((None, A_GATE_ROWS, tm), col),
        ],
        out_shape=[
            jax.ShapeDtypeStruct((t, A_QK_W), BF16),
            jax.ShapeDtypeStruct((bsz, A_QK_W, seq), BF16),
            jax.ShapeDtypeStruct((bsz, A_V_W, seq), BF16),
            jax.ShapeDtypeStruct((bsz, A_V_W, seq), BF16),
            jax.ShapeDtypeStruct((bsz, A_GATE_ROWS, seq), F32),
        ],
        compiler_params=pltpu.CompilerParams(
            dimension_semantics=("arbitrary",), vmem_limit_bytes=VMEM_LIMIT),
        name="inproj",
    )(x, g, mod, wk, wt)


ONES_ROWS = 16
CT_ROWS = A_V + ONES_ROWS


def _scan_lanes(x, op, fill, seg):
    pos = lax.broadcasted_iota(jnp.int32, x.shape, 1) % seg
    sh = 1
    while sh < seg:
        x = op(x, jnp.where(pos >= sh, pltpu.roll(x, sh, axis=1), fill))
        sh *= 2
    return x


def _gates_kernel(gt_ref, gb_ref, bb_ref, r_ref, cm_ref, *, chunk):
    H = A_HEADS
    ig = gt_ref[0:H, :] + gb_ref[0:H, 0:1]
    fg = gt_ref[H:2 * H, :] + gb_ref[H:2 * H, 0:1]
    bb = _scan_lanes(jax.nn.log_sigmoid(fg), jnp.add, 0.0, chunk)
    r = ig - bb
    bb_ref[...] = bb
    r_ref[...] = r
    cm_ref[...] = _scan_lanes(r, jnp.maximum, -jnp.inf, chunk)


def _gates(gt, gate_b, chunk):
    bsz, _, seq = gt.shape
    spec = pl.BlockSpec((None, A_HEADS, seq), lambda b: (b, 0, 0))
    shape = jax.ShapeDtypeStruct((bsz, A_HEADS, seq), F32)
    return pl.pallas_call(
        functools.partial(_gates_kernel, chunk=chunk),
        grid=(bsz,),
        in_specs=[pl.BlockSpec((None, A_GATE_ROWS, seq), lambda b: (b, 0, 0)),
                  pl.BlockSpec(gate_b.shape, lambda b: (0, 0))],
        out_specs=[spec, spec, spec],
        out_shape=[shape, shape, shape],
        compiler_params=pltpu.CompilerParams(
            dimension_semantics=("arbitrary",), vmem_limit_bytes=VMEM_LIMIT),
        name="gates",
    )(gt, gate_b)


def _mlstm_kernel(k_ref, qt_ref, vt_ref, ot_ref, bb_ref, r_ref, cm_ref, hg_ref, out_ref,
                  ct_ref, m_ref, *, chunk, n_chunks):
    L = chunk
    H = A_HEADS

    @pl.when(pl.program_id(1) == 0)
    def _():
        ct_ref[...] = jnp.zeros_like(ct_ref)
        m_ref[...] = jnp.zeros_like(m_ref)

    src = lax.broadcasted_iota(jnp.int32, (L, L), 0)
    tgt = lax.broadcasted_iota(jnp.int32, (L, L), 1)
    causal = src <= tgt
    ones_rows = jnp.ones((ONES_ROWS, L), BF16)

    for c in range(n_chunks):
        cs = slice(c * L, (c + 1) * L)
        bb = bb_ref[:, cs]
        r = r_ref[:, cs]
        m_prev = m_ref[...]
        mm = jnp.maximum(m_prev, cm_ref[:, cs])
        e_inv = jnp.exp(-(bb + mm))
        w_inter = jnp.exp(m_prev - mm)
        mm_last = mm[:, L - 1:L]
        w_upd = jnp.exp(r - mm_last)
        decay = jnp.exp(m_prev - mm_last)
        m_ref[...] = bb[:, L - 1:L] + mm_last
        r_cols = jnp.concatenate([r, jnp.zeros((L - H, L), F32)], axis=0).T

        for h in range(H):
            kh = k_ref[cs, h * A_QK:(h + 1) * A_QK]
            qth = qt_ref[h * A_QK:(h + 1) * A_QK, cs]
            ct = ct_ref[h]
            st = _dot(kh, qth)
            inter = _dot(ct.astype(BF16), qth)
            arg = jnp.where(causal, r_cols[:, h:h + 1] - mm[h:h + 1, :], -jnp.inf)
            pt = (st * jnp.exp(arg)).astype(BF16)
            vext = jnp.concatenate([vt_ref[h * A_V:(h + 1) * A_V, cs], ones_rows], axis=0)
            tot = w_inter[h:h + 1, :] * inter + _dot(vext, pt)
            den = tot[A_V:A_V + 1, :]
            hh = tot[0:A_V, :] * (1.0 / jnp.maximum(jnp.abs(den), e_inv[h:h + 1, :]))
            ms = jnp.mean(hh * hh, axis=0, keepdims=True)
            hn = hh * lax.rsqrt(ms + EPS) * hg_ref[h * A_V:(h + 1) * A_V, :]
            og = ot_ref[h * A_V:(h + 1) * A_V, cs].astype(F32)
            out_ref[h * A_V:(h + 1) * A_V, cs] = (hn * jax.nn.sigmoid(og)).astype(BF16)
            wv = (vext.astype(F32) * w_upd[h:h + 1, :]).astype(BF16)
            ct_ref[h] = decay[h:h + 1, :] * ct + _dot(wv, kh)


def _mlstm(k, qt, vt, ot, bb, r, cm, head_g, chunk, tb):
    bsz, _, seq = qt.shape
    per_b = seq // tb
    col = lambda b, i: (b, 0, i)
    gate_spec = pl.BlockSpec((None, A_HEADS, tb), col)
    return pl.pallas_call(
        functools.partial(_mlstm_kernel, chunk=chunk, n_chunks=tb // chunk),
        grid=(bsz, per_b),
        in_specs=[
            pl.BlockSpec((tb, A_QK_W), lambda b, i: (b * per_b + i, 0)),
            pl.BlockSpec((None, A_QK_W, tb), col),
            pl.BlockSpec((None, A_V_W, tb), col),
            pl.BlockSpec((None, A_V_W, tb), col),
            gate_spec, gate_spec, gate_spec,
            pl.BlockSpec(head_g.shape, lambda b, i: (0, 0)),
        ],
        out_specs=pl.BlockSpec((None, A_V_W, tb), col),
        out_shape=jax.ShapeDtypeStruct((bsz, A_V_W, seq), BF16),
        scratch_shapes=[
            pltpu.VMEM((A_HEADS, CT_ROWS, A_QK), F32),
            pltpu.VMEM((A_HEADS, 1), F32),
        ],
        compiler_params=pltpu.CompilerParams(
            dimension_semantics=("arbitrary", "arbitrary"), vmem_limit_bytes=VMEM_LIMIT),
        name="mlstm",
    )(k, qt, vt, ot, bb, r, cm, head_g)


def _post_kernel(x_ref, mix_ref, mod_ref, ng_ref, wo_ref, w1_ref, w2_ref, out_ref, *, ff_chunk):
    y = _dot_tn(mix_ref[...], wo_ref[...])
    x1 = x_ref[...] + mod_ref[2:3, :] * (_rms(y) * ng_ref[1:2, :])
    h = _rms(x1) * ng_ref[2:3, :]
    h = (h * (1.0 + mod_ref[4:5, :]) + mod_ref[3:4, :]).astype(BF16)
    acc = None
    for j in range(D_FF // ff_chunk):
        a = _dot(h, w1_ref[:, j * ff_chunk:(j + 1) * ff_chunk])
        a = jnp.square(jnp.maximum(a, 0.0)).astype(BF16)
        part = _dot(a, w2_ref[j * ff_chunk:(j + 1) * ff_chunk, :])
        acc = part if acc is None else acc + part
    out_ref[...] = x1 + mod_ref[5:6, :] * (_rms(acc) * ng_ref[3:4, :])


def _post(x, mix_t, mod, ng, wo, w1, w2, seq, tm):
    t, d = x.shape
    per_b = seq // tm
    row = lambda i: (i, 0)
    return pl.pallas_call(
        functools.partial(_post_kernel, ff_chunk=1024),
        grid=(t // tm,),
        in_specs=[
            pl.BlockSpec((tm, d), row),
            pl.BlockSpec((None, d, tm), lambda i: (i // per_b, 0, i % per_b)),
            pl.BlockSpec((None, 8, d), lambda i: (i // per_b, 0, 0)),
            _const_spec((4, d)),
            _const_spec(wo.shape),
            _const_spec(w1.shape),
            _const_spec(w2.shape),
        ],
        out_specs=pl.BlockSpec((tm, d), row),
        out_shape=jax.ShapeDtypeStruct((t, d), F32),
        compiler_params=pltpu.CompilerParams(
            dimension_semantics=("arbitrary",), vmem_limit_bytes=VMEM_LIMIT),
        name="post",
    )(x, mix_t, mod, ng, wo, w1, w2)


def _l1proj_kernel(x_ref, pos_ref, freq_ref, ng_ref, mod_ref, wa_ref, gl_ref, wb_ref,
                   wqa_ref, gq_ref, wqb_ref, q_ref, k_ref, v_ref, *, q_scale):
    xn = _rms(x_ref[...])
    ang = pos_ref[...].astype(F32) * freq_ref[0:1, :]
    cos_t = jnp.cos(ang)
    sin_t = jnp.sin(ang) * freq_ref[1:2, :]

    hk = xn * ng_ref[0:1, :]
    hk = (hk * (1.0 + mod_ref[1:2, :]) + mod_ref[0:1, :]).astype(BF16)
    kva = _dot(hk, wa_ref[...])
    ckv = (_rms(kva[:, 0:B_KV_LORA]) * gl_ref[...]).astype(BF16)
    k_rope = (kva[:, B_KV_LORA:B_KV_LORA + LANES] * cos_t
              + kva[:, B_KV_LORA + LANES:B_KV_LORA + 2 * LANES] * sin_t).astype(BF16)
    kvb = _dot(ckv, wb_ref[...])
    for h in range(B_HEADS):
        k_ref[:, h * HEAD_CAT:h * HEAD_CAT + B_NOPE] = kvb[:, h * 256:h * 256 + B_NOPE].astype(BF16)
        k_ref[:, h * HEAD_CAT + B_NOPE:(h + 1) * HEAD_CAT] = k_rope
        v_ref[:, h * B_V:(h + 1) * B_V] = kvb[:, h * 256 + B_NOPE:(h + 1) * 256].astype(BF16)

    hq = xn * ng_ref[1:2, :]
    hq = (hq * (1.0 + mod_ref[3:4, :]) + mod_ref[2:3, :]).astype(BF16)
    cq = (_rms(_dot(hq, wqa_ref[...])) * gq_ref[...]).astype(BF16)
    nw = B_HEADS * B_NOPE
    qn = _dot(cq, wqb_ref[:, 0:nw])
    qr = _dot(cq, wqb_ref[:, nw:2 * nw])
    qs = _dot(cq, wqb_ref[:, 2 * nw:3 * nw])
    for h in range(B_HEADS):
        sl = slice(h * LANES, (h + 1) * LANES)
        q_ref[:, h * HEAD_CAT:h * HEAD_CAT + B_NOPE] = (qn[:, sl] * q_scale).astype(BF16)
        q_ref[:, h * HEAD_CAT + B_NOPE:(h + 1) * HEAD_CAT] = (
            (qr[:, sl] * cos_t + qs[:, sl] * sin_t) * q_scale).astype(BF16)


def _l1proj(x, pos, freq, ng, mod, wa, gl, wb, wqa, gq, wqb, seq, tm):
    t, d = x.shape
    per_b = seq // tm
    row = lambda i: (i, 0)
    q_scale = float((B_NOPE + B_ROPE) ** -0.5 * math.log2(math.e))
    return pl.pallas_call(
        functools.partial(_l1proj_kernel, q_scale=q_scale),
        grid=(t // tm,),
        in_specs=[
            pl.BlockSpec((tm, d), row),
            pl.BlockSpec((tm, 1), row),
            _const_spec(freq.shape),
            _const_spec(ng.shape),
            pl.BlockSpec((None, 8, d), lambda i: (i // per_b, 0, 0)),
            _const_spec(wa.shape),
            _const_spec(gl.shape),
            _const_spec(wb.shape),
            _const_spec(wqa.shape),
            _const_spec(gq.shape),
            _const_spec(wqb.shape),
        ],
        out_specs=[
            pl.BlockSpec((tm, B_HEADS * HEAD_CAT), row),
            pl.BlockSpec((tm, B_HEADS * HEAD_CAT), row),
            pl.BlockSpec((tm, B_HEADS * B_V), row),
        ],
        out_shape=[
            jax.ShapeDtypeStruct((t, B_HEADS * HEAD_CAT), BF16),
            jax.ShapeDtypeStruct((t, B_HEADS * HEAD_CAT), BF16),
            jax.ShapeDtypeStruct((t, B_HEADS * B_V), BF16),
        ],
        compiler_params=pltpu.CompilerParams(
            dimension_semantics=("arbitrary",), vmem_limit_bytes=VMEM_LIMIT),
        name="l1proj",
    )(x, pos, freq, ng, mod, wa, gl, wb, wqa, gq, wqb)


V_ROWS_T = B_V + 16


def _attn_kernel(q_ref, k_ref, v_ref, o_ref, vt_ref, sa_ref, sb_ref, qt_ref, *, tq):
    qi = pl.program_id(2)

    @pl.when(qi == 0)
    def _():
        for i in range(v_ref.shape[0] // tq):
            vt_ref[i, 0:B_V, :] = v_ref[i * tq:(i + 1) * tq, :].astype(F32).T.astype(BF16)
            vt_ref[i, B_V:V_ROWS_T, :] = jnp.ones((V_ROWS_T - B_V, tq), BF16)

    qt_ref[...] = q_ref[...].astype(F32).T.astype(BF16)

    def scores(j):
        rows = pl.ds(pl.multiple_of(j * tq, tq), tq)
        return _dot(k_ref[rows, :], qt_ref[...])

    def soft_pv(j, s, m, acc, masked=False):
        if masked:
            keys = lax.broadcasted_iota(jnp.int32, (tq, tq), 0)
            qrys = lax.broadcasted_iota(jnp.int32, (tq, tq), 1)
            s = jnp.where(keys <= qrys, s, -jnp.inf)
        m_new = jnp.maximum(m, jnp.max(s, axis=0, keepdims=True))
        p = jnp.exp2(s - m_new).astype(BF16)
        acc = jnp.exp2(m - m_new) * acc + _dot(vt_ref[j], p)
        return m_new, acc

    sa_ref[...] = scores(0)

    def pair(jj, carry):
        m, acc = carry
        j0 = 2 * jj
        sb_ref[...] = scores(j0 + 1)
        m, acc = soft_pv(j0, sa_ref[...], m, acc)
        sa_ref[...] = scores(j0 + 2)
        return soft_pv(j0 + 1, sb_ref[...], m, acc)

    init = (jnp.full((1, tq), -jnp.inf, F32), jnp.zeros((V_ROWS_T, tq), F32))
    m, acc = lax.fori_loop(0, qi // 2, pair, init)

    def odd_tail(m, acc):
        sb_ref[...] = scores(qi)
        m, acc = soft_pv(qi - 1, sa_ref[...], m, acc)
        return soft_pv(qi, sb_ref[...], m, acc, masked=True)

    def even_tail(m, acc):
        return soft_pv(qi, sa_ref[...], m, acc, masked=True)

    _, acc = lax.cond(qi % 2 == 1, odd_tail, even_tail, m, acc)
    o_ref[...] = (acc[0:B_V, :] * (1.0 / acc[B_V:B_V + 1, :])).astype(BF16)


def _attn(q, k, v, tq):
    b, s, _ = q.shape
    return pl.pallas_call(
        functools.partial(_attn_kernel, tq=tq),
        grid=(b, B_HEADS, s // tq),
        in_specs=[
            pl.BlockSpec((None, tq, HEAD_CAT), lambda bi, h, i: (bi, i, h)),
            pl.BlockSpec((None, s, HEAD_CAT), lambda bi, h, i: (bi, 0, h)),
            pl.BlockSpec((None, s, B_V), lambda bi, h, i: (bi, 0, h)),
        ],
        out_specs=pl.BlockSpec((None, B_V, tq), lambda bi, h, i: (bi, h, i)),
        out_shape=jax.ShapeDtypeStruct((b, B_HEADS * B_V, s), BF16),
        scratch_shapes=[
            pltpu.VMEM((s // tq, V_ROWS_T, tq), BF16),
            pltpu.VMEM((tq, tq), F32),
            pltpu.VMEM((tq, tq), F32),
            pltpu.VMEM((HEAD_CAT, tq), BF16),
        ],
        compiler_params=pltpu.CompilerParams(
            dimension_semantics=("arbitrary", "arbitrary", "arbitrary"),
            vmem_limit_bytes=VMEM_LIMIT),
        name="attn",
    )(q, k, v)


def _swap_halves(w):
    half = w.shape[-1] // 2
    return jnp.concatenate([w[..., half:], w[..., :half]], axis=-1)


def _pad_cols(w, n):
    return jnp.pad(w, ((0, 0), (0, n - w.shape[1])))


def kernel(x, c, positions, ada_w, ada_b, norm_g, a_w_in, a_gate_b, a_head_g, a_w_out,
           kv_ada_w, kv_ada_b, kv_norm_g, kv_w_a, kv_latent_g, kv_w_b, b_w_q_a, b_q_latent_g,
           b_w_q_b, b_w_out, mlp_w1, mlp_w2):
    bsz, seq, d = x.shape
    t = bsz * seq
    x2d = x.reshape(t, d)

    c8 = jnp.pad(c, ((0, 8 - bsz), (0, 0)))
    ada = _adaln(c8, ada_w, ada_b[:, None, :], tn=1536)[:, :bsz]
    kv_ada = _adaln(c8, kv_ada_w[None], kv_ada_b[None, None, :], tn=1024)[0, :bsz]

    def mod_rows(vecs):
        rows = [v.reshape(bsz, 1, d) for v in vecs]
        rows += [jnp.zeros((bsz, 1, d), F32)] * (8 - len(rows))
        return jnp.concatenate(rows, axis=1)

    ada0 = [ada[0][:, i * d:(i + 1) * d] for i in range(6)]
    ada1 = [ada[1][:, i * d:(i + 1) * d] for i in range(6)]
    kv_shift, kv_scale = kv_ada[:, :d], kv_ada[:, d:]

    w_in = a_w_in[0]
    c0 = 2 * A_QK_W + A_V_W
    wk = w_in[:, A_QK_W:2 * A_QK_W].astype(BF16)
    wt = jnp.concatenate([
        w_in[:, :A_QK_W],
        w_in[:, 2 * A_QK_W:c0],
        w_in[:, c0 + 2 * A_HEADS:],
        w_in[:, c0:c0 + 2 * A_HEADS],
    ], axis=1).T.astype(BF16)
    k, qt, vt, ot, gt = _inproj(x2d, norm_g[0, 0][None], mod_rows(ada0[:2]), wk, wt, seq, tm=512)
    gate_b = jnp.broadcast_to(a_gate_b[0].reshape(A_GATE_ROWS, 1), (A_GATE_ROWS, LANES))
    head_g = jnp.broadcast_to(a_head_g[0].reshape(A_V_W, 1), (A_V_W, MLSTM_CHUNK))
    bb, r, cm = _gates(gt, gate_b, MLSTM_CHUNK)
    mix0 = _mlstm(k, qt, vt, ot, bb, r, cm, head_g, MLSTM_CHUNK, tb=MLSTM_TB)
    x2d = _post(x2d, mix0, mod_rows(ada0), norm_g[0], a_w_out[0].astype(BF16),
                mlp_w1[0].astype(BF16), mlp_w2[0].astype(BF16), seq, tm=512)

    rope = kv_w_a[:, B_KV_LORA:]
    wa_cat = jnp.concatenate([
        kv_w_a[:, :B_KV_LORA], _pad_cols(rope, LANES), _pad_cols(_swap_halves(rope), LANES),
    ], axis=1).astype(BF16)
    wqb = b_w_q_b[0].reshape(B_Q_LORA, B_HEADS, B_NOPE + B_ROPE)
    wq_rope = wqb[:, :, B_NOPE:]
    pad_r = lambda w: jnp.pad(w, ((0, 0), (0, 0), (0, LANES - B_ROPE))).reshape(B_Q_LORA, -1)
    wqb_cat = jnp.concatenate([
        wqb[:, :, :B_NOPE].reshape(B_Q_LORA, -1), pad_r(wq_rope), pad_r(_swap_halves(wq_rope)),
    ], axis=1).astype(BF16)
    half = B_ROPE // 2
    inv = ROPE_THETA ** (-jnp.arange(half, dtype=F32) / half)
    freq = jnp.zeros((8, LANES), F32)
    freq = freq.at[0].set(jnp.tile(inv, LANES // half))
    freq = freq.at[1].set(jnp.tile(jnp.concatenate([-jnp.ones(half, F32), jnp.ones(half, F32)]),
                                   LANES // B_ROPE))
    ng1 = jnp.concatenate([kv_norm_g[None], norm_g[1, 0][None]], axis=0)
    mod1 = mod_rows([kv_shift, kv_scale, ada1[0], ada1[1]])
    q_cat, k_cat, v1 = _l1proj(
        x2d, positions.reshape(t, 1), freq, ng1, mod1, wa_cat, kv_latent_g[None],
        kv_w_b.astype(BF16), b_w_q_a[0].astype(BF16), b_q_latent_g[0][None], wqb_cat, seq, tm=512)
    o = _attn(q_cat.reshape(bsz, seq, -1), k_cat.reshape(bsz, seq, -1),
              v1.reshape(bsz, seq, -1), tq=1024)
    x2d = _post(x2d, o, mod_rows(ada1), norm_g[1], b_w_out[0].astype(BF16),
                mlp_w1[1].astype(BF16), mlp_w2[1].astype(BF16), seq, tm=512)
    return x2d.reshape(bsz, seq, d)
```

```python
import functools
import math

import jax
import jax.numpy as jnp
from jax import lax
from jax.experimental import pallas as pl
from jax.experimental.pallas import tpu as pltpu

F32 = jnp.float32
BF16 = jnp.bfloat16

D_MODEL = 1024
D_FF = 4 * D_MODEL
EPS = 1e-6

A_HEADS = 8
A_QK = 64
A_V = 128
A_QK_W = A_HEADS * A_QK
A_V_W = A_HEADS * A_V
MLSTM_CHUNK = 256
MLSTM_TB = 1024

B_HEADS = 8
B_Q_LORA = 384
B_KV_LORA = 256
B_NOPE = 128
B_ROPE = 64
B_V = 128
ROPE_THETA = 10000.0
HEAD_CAT = 256
ATTN_TQ = 1024
L1_TM = 512

LANES = 128
VMEM_LIMIT = 56 * 1024 * 1024


def _dot(a, b):
    return jnp.dot(a, b, preferred_element_type=F32)


def _dot_nt(a, b):
    return lax.dot_general(a, b, (((1,), (1,)), ((), ())), preferred_element_type=F32)


def _dot_tn(a, b):
    return lax.dot_general(a, b, (((0,), (0,)), ((), ())), preferred_element_type=F32)


def _rms(x):
    return x * lax.rsqrt(jnp.mean(x * x, axis=-1, keepdims=True) + EPS)


def _const_spec(shape):
    nd = len(shape)
    return pl.BlockSpec(shape, lambda *_: (0,) * nd, pipeline_mode=pl.Buffered(1))


def _adaln_kernel(c_ref, w_ref, b_ref, o_ref):
    c = c_ref[...]
    cond = c * jax.nn.sigmoid(c)
    o_ref[...] = _dot(cond.astype(BF16), w_ref[...].astype(BF16)) + b_ref[...]


def _adaln(c8, w, b, tn):
    nl, d, n = w.shape
    return pl.pallas_call(
        _adaln_kernel,
        grid=(nl, n // tn),
        in_specs=[
            pl.BlockSpec((8, d), lambda l, j: (0, 0)),
            pl.BlockSpec((None, d, tn), lambda l, j: (l, 0, j)),
            pl.BlockSpec((None, 1, tn), lambda l, j: (l, 0, j)),
        ],
        out_specs=pl.BlockSpec((None, 8, tn), lambda l, j: (l, 0, j)),
        out_shape=jax.ShapeDtypeStruct((nl, 8, n), F32),
        compiler_params=pltpu.CompilerParams(
            dimension_semantics=("arbitrary", "arbitrary"), vmem_limit_bytes=VMEM_LIMIT),
        name="adaln",
    )(c8, w, b)


A_GATE_ROWS = 2 * A_HEADS
WT_Q0, WT_V0, WT_O0, WT_G0 = 0, A_QK_W, A_QK_W + A_V_W, A_QK_W + 2 * A_V_W
WT_ROWS = WT_G0 + A_GATE_ROWS


def _inproj_kernel(x_ref, g_ref, mod_ref, wk_ref, wt_ref, k_ref, qt_ref, vt_ref, ot_ref, gt_ref):
    h = _rms(x_ref[...]) * g_ref[...]
    h = (h * (1.0 + mod_ref[1:2, :]) + mod_ref[0:1, :]).astype(BF16)
    k_ref[...] = (_dot(h, wk_ref[...]) * (A_QK ** -0.5)).astype(BF16)
    qt_ref[...] = _dot_nt(wt_ref[WT_Q0:WT_V0, :], h).astype(BF16)
    vt_ref[...] = _dot_nt(wt_ref[WT_V0:WT_O0, :], h).astype(BF16)
    ot_ref[...] = _dot_nt(wt_ref[WT_O0:WT_G0, :], h).astype(BF16)
    gt_ref[...] = _dot_nt(wt_ref[WT_G0:WT_ROWS, :], h)


def _inproj(x, g, mod, wk, wt, seq, tm):
    t, d = x.shape
    bsz = t // seq
    per_b = seq // tm
    row = lambda i: (i, 0)
    col = lambda i: (i // per_b, 0, i % per_b)
    return pl.pallas_call(
        _inproj_kernel,
        grid=(t // tm,),
        in_specs=[
            pl.BlockSpec((tm, d), row),
            _const_spec((1, d)),
            pl.BlockSpec((None, 8, d), lambda i: (i // per_b, 0, 0)),
            _const_spec(wk.shape),
            _const_spec(wt.shape),
        ],
        out_specs=[
            pl.BlockSpec((tm, A_QK_W), row),
            pl.BlockSpec((None, A_QK_W, tm), col),
            pl.BlockSpec((None, A_V_W, tm), col),
            pl.BlockSpec((None, A_V_W, tm), col),
            pl.BlockSpec((None, A_GATE_ROWS, tm), col),
        ],
        out_shape=[
            jax.ShapeDtypeStruct((t, A_QK_W), BF16),
            jax.ShapeDtypeStruct((bsz, A_QK_W, seq), BF16),
            jax.ShapeDtypeStruct((bsz, A_V_W, seq), BF16),
            jax.ShapeDtypeStruct((bsz, A_V_W, seq), BF16),
            jax.ShapeDtypeStruct((bsz, A_GATE_ROWS, seq), F32),
        ],
        compiler_params=pltpu.CompilerParams(
            dimension_semantics=("arbitrary",), vmem_limit_bytes=VMEM_LIMIT),
        name="inproj",
    )(x, g, mod, wk, wt)


ONES_ROWS = 16
CT_ROWS = A_V + ONES_ROWS


def _scan_lanes(x, op, fill, seg):
    pos = lax.broadcasted_iota(jnp.int32, x.shape, 1) % seg
    sh = 1
    while sh < seg:
        x = op(x, jnp.where(pos >= sh, pltpu.roll(x, sh, axis=1), fill))
        sh *= 2
    return x


def _gates_kernel(gt_ref, gb_ref, bb_ref, r_ref, cm_ref, *, chunk):
    H = A_HEADS
    ig = gt_ref[0:H, :] + gb_ref[0:H, 0:1]
    fg = gt_ref[H:2 * H, :] + gb_ref[H:2 * H, 0:1]
    bb = _scan_lanes(jax.nn.log_sigmoid(fg), jnp.add, 0.0, chunk)
    r = ig - bb
    bb_ref[...] = bb
    r_ref[...] = r
    cm_ref[...] = _scan_lanes(r, jnp.maximum, -jnp.inf, chunk)


def _gates(gt, gate_b, chunk):
    bsz, _, seq = gt.shape
    spec = pl.BlockSpec((None, A_HEADS, seq), lambda b: (b, 0, 0))
    shape = jax.ShapeDtypeStruct((bsz, A_HEADS, seq), F32)
    return pl.pallas_call(
        functools.partial(_gates_kernel, chunk=chunk),
        grid=(bsz,),
        in_specs=[pl.BlockSpec((None, A_GATE_ROWS, seq), lambda b: (b, 0, 0)),
                  pl.BlockSpec(gate_b.shape, lambda b: (0, 0))],
        out_specs=[spec, spec, spec],
        out_shape=[shape, shape, shape],
        compiler_params=pltpu.CompilerParams(
            dimension_semantics=("arbitrary",), vmem_limit_bytes=VMEM_LIMIT),
        name="gates",
    )(gt, gate_b)


def _mlstm_kernel(k_ref, qt_ref, vt_ref, ot_ref, bb_ref, r_ref, cm_ref, hg_ref, out_ref,
                  ct_ref, m_ref, *, chunk, n_chunks):
    L = chunk
    H = A_HEADS

    @pl.when(pl.program_id(1) == 0)
    def _():
        ct_ref[...] = jnp.zeros_like(ct_ref)
        m_ref[...] = jnp.zeros_like(m_ref)

    src = lax.broadcasted_iota(jnp.int32, (L, L), 0)
    tgt = lax.broadcasted_iota(jnp.int32, (L, L), 1)
    causal = src <= tgt
    ones_rows = jnp.ones((ONES_ROWS, L), BF16)

    for c in range(n_chunks):
        cs = slice(c * L, (c + 1) * L)
        bb = bb_ref[:, cs]
        r = r_ref[:, cs]
        m_prev = m_ref[...]
        mm = jnp.maximum(m_prev, cm_ref[:, cs])
        e_inv = jnp.exp(-(bb + mm))
        w_inter = jnp.exp(m_prev - mm)
        mm_last = mm[:, L - 1:L]
        w_upd = jnp.exp(r - mm_last)
        decay = jnp.exp(m_prev - mm_last)
        m_ref[...] = bb[:, L - 1:L] + mm_last
        r_cols = jnp.concatenate([r, jnp.zeros((L - H, L), F32)], axis=0).T

        for h in range(H):
            kh = k_ref[cs, h * A_QK:(h + 1) * A_QK]
            qth = qt_ref[h * A_QK:(h + 1) * A_QK, cs]
            ct = ct_ref[h]
            st = _dot(kh, qth)
            inter = _dot(ct.astype(BF16), qth)
            arg = jnp.where(causal, r_cols[:, h:h + 1] - mm[h:h + 1, :], -jnp.inf)
            pt = (st * jnp.exp(arg)).astype(BF16)
            vext = jnp.concatenate([vt_ref[h * A_V:(h + 1) * A_V, cs], ones_rows], axis=0)
            tot = w_inter[h:h + 1, :] * inter + _dot(vext, pt)
            den = tot[A_V:A_V + 1, :]
            hh = tot[0:A_V, :] * (1.0 / jnp.maximum(jnp.abs(den), e_inv[h:h + 1, :]))
            ms = jnp.mean(hh * hh, axis=0, keepdims=True)
            hn = hh * lax.rsqrt(ms + EPS) * hg_ref[h * A_V:(h + 1) * A_V, :]
            og = ot_ref[h * A_V:(h + 1) * A_V, cs].astype(F32)
            out_ref[h * A_V:(h + 1) * A_V, cs] = (hn * jax.nn.sigmoid(og)).astype(BF16)
            wv = (vext.astype(F32) * w_upd[h:h + 1, :]).astype(BF16)
            ct_ref[h] = decay[h:h + 1, :] * ct + _dot(wv, kh)


def _mlstm(k, qt, vt, ot, bb, r, cm, head_g, chunk, tb):
    bsz, _, seq = qt.shape
    per_b = seq // tb
    col = lambda b, i: (b, 0, i)
    gate_spec = pl.BlockSpec((None, A_HEADS, tb), col)
    return pl.pallas_call(
        functools.partial(_mlstm_kernel, chunk=chunk, n_chunks=tb // chunk),
        grid=(bsz, per_b),
        in_specs=[
            pl.BlockSpec((tb, A_QK_W), lambda b, i: (b * per_b + i, 0)),
            pl.BlockSpec((None, A_QK_W, tb), col),
            pl.BlockSpec((None, A_V_W, tb), col),
            pl.BlockSpec((None, A_V_W, tb), col),
            gate_spec, gate_spec, gate_spec,
            pl.BlockSpec(head_g.shape, lambda b, i: (0, 0)),
        ],
        out_specs=pl.BlockSpec((None, A_V_W, tb), col),
        out_shape=jax.ShapeDtypeStruct((bsz, A_V_W, seq), BF16),
        scratch_shapes=[
            pltpu.VMEM((A_HEADS, CT_ROWS, A_QK), F32),
            pltpu.VMEM((A_HEADS, 1), F32),
        ],
        compiler_params=pltpu.CompilerParams(
            dimension_semantics=("arbitrary", "arbitrary"), vmem_limit_bytes=VMEM_LIMIT),
        name="mlstm",
    )(k, qt, vt, ot, bb, r, cm, head_g)


def _post_kernel(x_ref, mix_ref, mod_ref, ng_ref, wo_ref, w1_ref, w2_ref, out_ref, *, ff_chunk):
    y = _dot_tn(mix_ref[...], wo_ref[...])
    x1 = x_ref[...] + mod_ref[2:3, :] * (_rms(y) * ng_ref[1:2, :])
    h = _rms(x1) * ng_ref[2:3, :]
    h = (h * (1.0 + mod_ref[4:5, :]) + mod_ref[3:4, :]).astype(BF16)
    acc = None
    for j in range(D_FF // ff_chunk):
        a = _dot(h, w1_ref[:, j * ff_chunk:(j + 1) * ff_chunk])
        a = jnp.square(jnp.maximum(a, 0.0)).astype(BF16)
        part = _dot(a, w2_ref[j * ff_chunk:(j + 1) * ff_chunk, :])
        acc = part if acc is None else acc + part
    out_ref[...] = x1 + mod_ref[5:6, :] * (_rms(acc) * ng_ref[3:4, :])


def _post(x, mix_t, mod, ng, wo, w1, w2, seq, tm):
    t, d = x.shape
    per_b = seq // tm
    row = lambda i: (i, 0)
    return pl.pallas_call(
        functools.partial(_post_kernel, ff_chunk=1024),
        grid=(t // tm,),
        in_specs=[
            pl.BlockSpec((tm, d), row),
            pl.BlockSpec((None, d, tm), lambda i: (i // per_b, 0, i % per_b)),
            pl.BlockSpec((None, 8, d), lambda i: (i // per_b, 0, 0)),
            _const_spec((4, d)),
            _const_spec(wo.shape),
            _const_spec(w1.shape),
            _const_spec(w2.shape),
        ],
        out_specs=pl.BlockSpec((tm, d), row),
        out_shape=jax.ShapeDtypeStruct((t, d), F32),
        compiler_params=pltpu.CompilerParams(
            dimension_semantics=("arbitrary",), vmem_limit_bytes=VMEM_LIMIT),
        name="post",
    )(x, mix_t, mod, ng, wo, w1, w2)


ROPE_HALF = B_ROPE // 2
Q_HEAD = B_NOPE + B_ROPE
V_ROWS_T = B_V + 16
WA_COLS = B_KV_LORA + 2 * B_ROPE


def _l1proj_kernel(x_ref, pos_ref, inv_ref, ng_ref, mod_ref, wa_ref, gl_ref, wbk_ref, wbv_ref,
                   wqa_ref, gq_ref, wq_ref, wqs_ref, q_ref, k_ref, v_ref, *, q_scale):
    tm = x_ref.shape[0]
    xn = _rms(x_ref[...])
    ang = inv_ref[...] * pos_ref[...].astype(F32)
    cos_h = jnp.cos(ang)
    sin_h = jnp.sin(ang)
    cos_t = jnp.concatenate([cos_h, cos_h], axis=0)
    sin_t = jnp.concatenate([-sin_h, sin_h], axis=0)

    hk = xn * ng_ref[0:1, :]
    hk = (hk * (1.0 + mod_ref[1:2, :]) + mod_ref[0:1, :]).astype(BF16)
    kva = _dot(hk, wa_ref[...])
    ckv = (_rms(kva[:, 0:B_KV_LORA]) * gl_ref[...]).astype(BF16)
    kr_t = kva[:, B_KV_LORA:WA_COLS].T
    k_rope_t = kr_t[0:B_ROPE, :] * cos_t + kr_t[B_ROPE:2 * B_ROPE, :] * sin_t
    k_rope = jnp.concatenate([k_rope_t, jnp.zeros((LANES - B_ROPE, tm), F32)], axis=0).T
    k_rope = k_rope.astype(BF16)
    k_nope = _dot(ckv, wbk_ref[...])
    for h in range(B_HEADS):
        k_ref[:, h * HEAD_CAT:h * HEAD_CAT + B_NOPE] = k_nope[:, h * B_NOPE:(h + 1) * B_NOPE].astype(BF16)
        k_ref[:, h * HEAD_CAT + B_NOPE:(h + 1) * HEAD_CAT] = k_rope
    v_t = _dot_nt(wbv_ref[...], ckv)
    ones = jnp.ones((V_ROWS_T - B_V, tm), BF16)
    for h in range(B_HEADS):
        v_ref[h * V_ROWS_T:h * V_ROWS_T + B_V, :] = v_t[h * B_V:(h + 1) * B_V, :].astype(BF16)
        v_ref[h * V_ROWS_T + B_V:(h + 1) * V_ROWS_T, :] = ones

    hq = xn * ng_ref[1:2, :]
    hq = (hq * (1.0 + mod_ref[3:4, :]) + mod_ref[2:3, :]).astype(BF16)
    cq = (_rms(_dot(hq, wqa_ref[...])) * gq_ref[...]).astype(BF16)
    q_t = _dot_nt(wq_ref[...], cq)
    qs_t = _dot_nt(wqs_ref[...], cq)
    zeros = jnp.zeros((HEAD_CAT - Q_HEAD, tm), BF16)
    for h in range(B_HEADS):
        r0 = h * Q_HEAD
        q_ref[h * HEAD_CAT:h * HEAD_CAT + B_NOPE, :] = (q_t[r0:r0 + B_NOPE, :] * q_scale).astype(BF16)
        rope = q_t[r0 + B_NOPE:r0 + Q_HEAD, :] * cos_t + qs_t[h * B_ROPE:(h + 1) * B_ROPE, :] * sin_t
        q_ref[h * HEAD_CAT + B_NOPE:h * HEAD_CAT + Q_HEAD, :] = (rope * q_scale).astype(BF16)
        q_ref[h * HEAD_CAT + Q_HEAD:(h + 1) * HEAD_CAT, :] = zeros


def _l1proj(x, pos, inv, ng, mod, wa, gl, wbk, wbv, wqa, gq, wq, wqs, seq, tm, tq):
    t, d = x.shape
    bsz = t // seq
    per_b = seq // tm
    per_q = tq // tm
    row = lambda i: (i, 0)
    q_scale = float((B_NOPE + B_ROPE) ** -0.5 * math.log2(math.e))
    consts = [inv, ng, None, wa, gl, wbk, wbv, wqa, gq, wq, wqs]
    in_specs = [pl.BlockSpec((tm, d), row),
                pl.BlockSpec((None, 1, tm), lambda i: (i // per_b, 0, i % per_b))]
    for a in consts:
        if a is None:
            in_specs.append(pl.BlockSpec((None, 8, d), lambda i: (i // per_b, 0, 0)))
        else:
            in_specs.append(_const_spec(a.shape))
    return pl.pallas_call(
        functools.partial(_l1proj_kernel, q_scale=q_scale),
        grid=(t // tm,),
        in_specs=in_specs,
        out_specs=[
            pl.BlockSpec((None, B_HEADS * HEAD_CAT, tm), lambda i: (i // per_b, 0, i % per_b)),
            pl.BlockSpec((tm, B_HEADS * HEAD_CAT), row),
            pl.BlockSpec((None, None, B_HEADS * V_ROWS_T, tm),
                         lambda i: (i // per_b, (i % per_b) // per_q, 0, i % per_q)),
        ],
        out_shape=[
            jax.ShapeDtypeStruct((bsz, B_HEADS * HEAD_CAT, seq), BF16),
            jax.ShapeDtypeStruct((t, B_HEADS * HEAD_CAT), BF16),
            jax.ShapeDtypeStruct((bsz, seq // tq, B_HEADS * V_ROWS_T, tq), BF16),
        ],
        compiler_params=pltpu.CompilerParams(
            dimension_semantics=("arbitrary",), vmem_limit_bytes=VMEM_LIMIT),
        name="l1proj",
    )(x, pos, inv, ng, mod, wa, gl, wbk, wbv, wqa, gq, wq, wqs)


def _attn_kernel(qt_ref, k_ref, vt_ref, o_ref, sa_ref, sb_ref, *, tq):
    qi = pl.program_id(2)

    def scores(j):
        rows = pl.ds(pl.multiple_of(j * tq, tq), tq)
        return _dot(k_ref[rows, :], qt_ref[...])

    def soft_pv(j, s, m, acc, masked=False):
        if masked:
            keys = lax.broadcasted_iota(jnp.int32, (tq, tq), 0)
            qrys = lax.broadcasted_iota(jnp.int32, (tq, tq), 1)
            s = jnp.where(keys <= qrys, s, -jnp.inf)
        m_new = jnp.maximum(m, jnp.max(s, axis=0, keepdims=True))
        p = jnp.exp2(s - m_new).astype(BF16)
        acc = jnp.exp2(m - m_new) * acc + _dot(vt_ref[j], p)
        return m_new, acc

    sa_ref[...] = scores(0)

    def pair(jj, carry):
        m, acc = carry
        j0 = 2 * jj
        sb_ref[...] = scores(j0 + 1)
        m, acc = soft_pv(j0, sa_ref[...], m, acc)
        sa_ref[...] = scores(j0 + 2)
        return soft_pv(j0 + 1, sb_ref[...], m, acc)

    init = (jnp.full((1, tq), -jnp.inf, F32), jnp.zeros((V_ROWS_T, tq), F32))
    m, acc = lax.fori_loop(0, qi // 2, pair, init)

    def odd_tail(m, acc):
        sb_ref[...] = scores(qi)
        m, acc = soft_pv(qi - 1, sa_ref[...], m, acc)
        return soft_pv(qi, sb_ref[...], m, acc, masked=True)

    def even_tail(m, acc):
        return soft_pv(qi, sa_ref[...], m, acc, masked=True)

    _, acc = lax.cond(qi % 2 == 1, odd_tail, even_tail, m, acc)
    o_ref[...] = (acc[0:B_V, :] * (1.0 / acc[B_V:B_V + 1, :])).astype(BF16)


def _attn(q_t, k, v_t, tq):
    b, s, _ = k.shape
    return pl.pallas_call(
        functools.partial(_attn_kernel, tq=tq),
        grid=(b, B_HEADS, s // tq),
        in_specs=[
            pl.BlockSpec((None, HEAD_CAT, tq), lambda bi, h, i: (bi, h, i)),
            pl.BlockSpec((None, s, HEAD_CAT), lambda bi, h, i: (bi, 0, h)),
            pl.BlockSpec((None, s // tq, V_ROWS_T, tq), lambda bi, h, i: (bi, 0, h, 0)),
        ],
        out_specs=pl.BlockSpec((None, B_V, tq), lambda bi, h, i: (bi, h, i)),
        out_shape=jax.ShapeDtypeStruct((b, B_HEADS * B_V, s), BF16),
        scratch_shapes=[pltpu.VMEM((tq, tq), F32), pltpu.VMEM((tq, tq), F32)],
        compiler_params=pltpu.CompilerParams(
            dimension_semantics=("arbitrary", "arbitrary", "arbitrary"),
            vmem_limit_bytes=VMEM_LIMIT),
        name="attn",
    )(q_t, k, v_t)


def _swap_halves(w):
    half = w.shape[-1] // 2
    return jnp.concatenate([w[..., half:], w[..., :half]], axis=-1)


def kernel(x, c, positions, ada_w, ada_b, norm_g, a_w_in, a_gate_b, a_head_g, a_w_out,
           kv_ada_w, kv_ada_b, kv_norm_g, kv_w_a, kv_latent_g, kv_w_b, b_w_q_a, b_q_latent_g,
           b_w_q_b, b_w_out, mlp_w1, mlp_w2):
    bsz, seq, d = x.shape
    t = bsz * seq
    x2d = x.reshape(t, d)

    c8 = jnp.pad(c, ((0, 8 - bsz), (0, 0)))
    ada = _adaln(c8, ada_w, ada_b[:, None, :], tn=1536)[:, :bsz]
    kv_ada = _adaln(c8, kv_ada_w[None], kv_ada_b[None, None, :], tn=1024)[0, :bsz]

    def mod_rows(vecs):
        rows = [v.reshape(bsz, 1, d) for v in vecs]
        rows += [jnp.zeros((bsz, 1, d), F32)] * (8 - len(rows))
        return jnp.concatenate(rows, axis=1)

    ada0 = [ada[0][:, i * d:(i + 1) * d] for i in range(6)]
    ada1 = [ada[1][:, i * d:(i + 1) * d] for i in range(6)]
    kv_shift, kv_scale = kv_ada[:, :d], kv_ada[:, d:]

    w_in = a_w_in[0]
    c0 = 2 * A_QK_W + A_V_W
    wk = w_in[:, A_QK_W:2 * A_QK_W].astype(BF16)
    wt = jnp.concatenate([
        w_in[:, :A_QK_W],
        w_in[:, 2 * A_QK_W:c0],
        w_in[:, c0 + 2 * A_HEADS:],
        w_in[:, c0:c0 + 2 * A_HEADS],
    ], axis=1).T.astype(BF16)
    k, qt, vt, ot, gt = _inproj(x2d, norm_g[0, 0][None], mod_rows(ada0[:2]), wk, wt, seq, tm=512)
    gate_b = jnp.broadcast_to(a_gate_b[0].reshape(A_GATE_ROWS, 1), (A_GATE_ROWS, LANES))
    head_g = jnp.broadcast_to(a_head_g[0].reshape(A_V_W, 1), (A_V_W, MLSTM_CHUNK))
    bb, r, cm = _gates(gt, gate_b, MLSTM_CHUNK)
    mix0 = _mlstm(k, qt, vt, ot, bb, r, cm, head_g, MLSTM_CHUNK, tb=MLSTM_TB)
    x2d = _post(x2d, mix0, mod_rows(ada0), norm_g[0], a_w_out[0].astype(BF16),
                mlp_w1[0].astype(BF16), mlp_w2[0].astype(BF16), seq, tm=512)

    tq = ATTN_TQ
    rope = kv_w_a[:, B_KV_LORA:]
    wa_cat = jnp.concatenate([kv_w_a[:, :B_KV_LORA], rope, _swap_halves(rope)], axis=1).astype(BF16)
    wb = kv_w_b.reshape(B_KV_LORA, B_HEADS, B_NOPE + B_V)
    wbk = wb[:, :, :B_NOPE].reshape(B_KV_LORA, -1).astype(BF16)
    wbv = wb[:, :, B_NOPE:].reshape(B_KV_LORA, -1).T.astype(BF16)
    wq = b_w_q_b[0].T.astype(BF16)
    wq_rope = b_w_q_b[0].reshape(B_Q_LORA, B_HEADS, Q_HEAD)[:, :, B_NOPE:]
    wqs = _swap_halves(wq_rope).reshape(B_Q_LORA, -1).T.astype(BF16)
    inv = ROPE_THETA ** (-jnp.arange(ROPE_HALF, dtype=F32) / ROPE_HALF)
    inv_rep = jnp.broadcast_to(inv[:, None], (ROPE_HALF, L1_TM))
    ng1 = jnp.concatenate([kv_norm_g[None], norm_g[1, 0][None]], axis=0)
    mod1 = mod_rows([kv_shift, kv_scale, ada1[0], ada1[1]])
    q_t, k_cat, v_t = _l1proj(
        x2d, positions.reshape(bsz, 1, seq), inv_rep, ng1, mod1, wa_cat, kv_latent_g[None], wbk, wbv,
        b_w_q_a[0].astype(BF16), b_q_latent_g[0][None], wq, wqs, seq, tm=L1_TM, tq=tq)
    o = _attn(q_t, k_cat.reshape(bsz, seq, -1), v_t, tq=tq)
    x2d = _post(x2d, o, mod_rows(ada1), norm_g[1], b_w_out[0].astype(BF16),
                mlp_w1[1].astype(BF16), mlp_w2[1].astype(BF16), seq, tm=512)
    return x2d.reshape(bsz, seq, d)
```

```python
import functools
import math

import jax
import jax.numpy as jnp
from jax import lax
from jax.experimental import pallas as pl
from jax.experimental.pallas import tpu as pltpu

F32 = jnp.float32
BF16 = jnp.bfloat16

D_MODEL = 1024
D_FF = 4 * D_MODEL
EPS = 1e-6

A_HEADS = 8
A_QK = 64
A_V = 128
A_QK_W = A_HEADS * A_QK
A_V_W = A_HEADS * A_V
MLSTM_CHUNK = 256
MLSTM_TB = 1024

B_HEADS = 8
B_Q_LORA = 384
B_KV_LORA = 256
B_NOPE = 128
B_ROPE = 64
B_V = 128
ROPE_THETA = 10000.0
HEAD_CAT = 256
ATTN_TQ = 1024
ROW_TM = 1024

LANES = 128
VMEM_LIMIT = 56 * 1024 * 1024


def _dot(a, b):
    return jnp.dot(a, b, preferred_element_type=F32)


def _dot_nt(a, b):
    return lax.dot_general(a, b, (((1,), (1,)), ((), ())), preferred_element_type=F32)


def _dot_tn(a, b):
    return lax.dot_general(a, b, (((0,), (0,)), ((), ())), preferred_element_type=F32)


def _rms(x):
    return x * lax.rsqrt(jnp.mean(x * x, axis=-1, keepdims=True) + EPS)


def _const_spec(shape):
    nd = len(shape)
    return pl.BlockSpec(shape, lambda *_: (0,) * nd, pipeline_mode=pl.Buffered(1))


def _adaln_kernel(c_ref, w_ref, b_ref, o_ref):
    c = c_ref[...]
    cond = c * jax.nn.sigmoid(c)
    o_ref[...] = _dot(cond.astype(BF16), w_ref[...].astype(BF16)) + b_ref[...]


def _adaln(c8, w, b, tn):
    nl, d, n = w.shape
    return pl.pallas_call(
        _adaln_kernel,
        grid=(nl, n // tn),
        in_specs=[
            pl.BlockSpec((8, d), lambda l, j: (0, 0)),
            pl.BlockSpec((None, d, tn), lambda l, j: (l, 0, j)),
            pl.BlockSpec((None, 1, tn), lambda l, j: (l, 0, j)),
        ],
        out_specs=pl.BlockSpec((None, 8, tn), lambda l, j: (l, 0, j)),
        out_shape=jax.ShapeDtypeStruct((nl, 8, n), F32),
        compiler_params=pltpu.CompilerParams(
            dimension_semantics=("arbitrary", "arbitrary"), vmem_limit_bytes=VMEM_LIMIT),
        name="adaln",
    )(c8, w, b)


A_GATE_ROWS = 2 * A_HEADS
WT_Q0, WT_V0, WT_O0, WT_G0 = 0, A_QK_W, A_QK_W + A_V_W, A_QK_W + 2 * A_V_W
WT_ROWS = WT_G0 + A_GATE_ROWS


def _inproj_kernel(x_ref, g_ref, mod_ref, wk_ref, wt_ref, k_ref, qt_ref, vt_ref, ot_ref, gt_ref):
    h = _rms(x_ref[...]) * g_ref[...]
    h = (h * (1.0 + mod_ref[1:2, :]) + mod_ref[0:1, :]).astype(BF16)
    k_ref[...] = (_dot(h, wk_ref[...]) * (A_QK ** -0.5)).astype(BF16)
    feat = _dot_nt(wt_ref[...], h)
    qt_ref[...] = feat[WT_Q0:WT_V0, :].astype(BF16)
    vt_ref[...] = feat[WT_V0:WT_O0, :].astype(BF16)
    ot_ref[...] = feat[WT_O0:WT_G0, :].astype(BF16)
    gt_ref[...] = feat[WT_G0:WT_ROWS, :]


def _inproj(x, g, mod, wk, wt, seq, tm):
    t, d = x.shape
    bsz = t // seq
    per_b = seq // tm
    row = lambda i: (i, 0)
    col = lambda i: (i // per_b, 0, i % per_b)
    return pl.pallas_call(
        _inproj_kernel,
        grid=(t // tm,),
        in_specs=[
            pl.BlockSpec((tm, d), row),
            _const_spec((1, d)),
            pl.BlockSpec((None, 8, d), lambda i: (i // per_b, 0, 0)),
            _const_spec(wk.shape),
            _const_spec(wt.shape),
        ],
        out_specs=[
            pl.BlockSpec((tm, A_QK_W), row),
            pl.BlockSpec((None, A_QK_W, tm), col),
            pl.BlockSpec((None, A_V_W, tm), col),
            pl.BlockSpec((None, A_V_W, tm), col),
            pl.BlockSpec((None, A_GATE_ROWS, tm), col),
        ],
        out_shape=[
            jax.ShapeDtypeStruct((t, A_QK_W), BF16),
            jax.ShapeDtypeStruct((bsz, A_QK_W, seq), BF16),
            jax.ShapeDtypeStruct((bsz, A_V_W, seq), BF16),
            jax.ShapeDtypeStruct((bsz, A_V_W, seq), BF16),
            jax.ShapeDtypeStruct((bsz, A_GATE_ROWS, seq), F32),
        ],
        compiler_params=pltpu.CompilerParams(
            dimension_semantics=("arbitrary",), vmem_limit_bytes=VMEM_LIMIT),
        name="inproj",
    )(x, g, mod, wk, wt)


ONES_ROWS = 16
CT_ROWS = A_V + ONES_ROWS


def _scan_lanes(x, op, fill, seg):
    pos = lax.broadcasted_iota(jnp.int32, x.shape, 1) % seg
    sh = 1
    while sh < seg:
        x = op(x, jnp.where(pos >= sh, pltpu.roll(x, sh, axis=1), fill))
        sh *= 2
    return x


def _gates_kernel(gt_ref, gb_ref, bb_ref, r_ref, cm_ref, *, chunk):
    H = A_HEADS
    ig = gt_ref[0:H, :] + gb_ref[0:H, 0:1]
    fg = gt_ref[H:2 * H, :] + gb_ref[H:2 * H, 0:1]
    bb = _scan_lanes(jax.nn.log_sigmoid(fg), jnp.add, 0.0, chunk)
    r = ig - bb
    bb_ref[...] = bb
    r_ref[...] = r
    cm_ref[...] = _scan_lanes(r, jnp.maximum, -jnp.inf, chunk)


def _gates(gt, gate_b, chunk):
    bsz, _, seq = gt.shape
    spec = pl.BlockSpec((None, A_HEADS, seq), lambda b: (b, 0, 0))
    shape = jax.ShapeDtypeStruct((bsz, A_HEADS, seq), F32)
    return pl.pallas_call(
        functools.partial(_gates_kernel, chunk=chunk),
        grid=(bsz,),
        in_specs=[pl.BlockSpec((None, A_GATE_ROWS, seq), lambda b: (b, 0, 0)),
                  pl.BlockSpec(gate_b.shape, lambda b: (0, 0))],
        out_specs=[spec, spec, spec],
        out_shape=[shape, shape, shape],
        compiler_params=pltpu.CompilerParams(
            dimension_semantics=("arbitrary",), vmem_limit_bytes=VMEM_LIMIT),
        name="gates",
    )(gt, gate_b)


def _mlstm_kernel(k_ref, qt_ref, vt_ref, ot_ref, bb_ref, r_ref, cm_ref, hg_ref, out_ref,
                  ct_ref, m_ref, *, chunk, n_chunks):
    L = chunk
    H = A_HEADS

    @pl.when(pl.program_id(1) == 0)
    def _():
        ct_ref[...] = jnp.zeros_like(ct_ref)
        m_ref[...] = jnp.zeros_like(m_ref)

    src = lax.broadcasted_iota(jnp.int32, (L, L), 0)
    tgt = lax.broadcasted_iota(jnp.int32, (L, L), 1)
    causal = src <= tgt
    ones_rows = jnp.ones((ONES_ROWS, L), BF16)

    for c in range(n_chunks):
        cs = slice(c * L, (c + 1) * L)
        bb = bb_ref[:, cs]
        r = r_ref[:, cs]
        m_prev = m_ref[...]
        mm = jnp.maximum(m_prev, cm_ref[:, cs])
        e_inv = jnp.exp(-(bb + mm))
        w_inter = jnp.exp(m_prev - mm)
        mm_last = mm[:, L - 1:L]
        w_upd = jnp.exp(r - mm_last)
        decay = jnp.exp(m_prev - mm_last)
        m_ref[...] = bb[:, L - 1:L] + mm_last
        r_cols = jnp.concatenate([r, jnp.zeros((L - H, L), F32)], axis=0).T

        for hp in range(H // 2):
            h0 = 2 * hp
            k2 = k_ref[cs, hp * LANES:(hp + 1) * LANES]
            zq = jnp.zeros((A_QK, L), BF16)
            q_bd = jnp.concatenate([
                jnp.concatenate([qt_ref[h0 * A_QK:(h0 + 1) * A_QK, cs], zq], axis=1),
                jnp.concatenate([zq, qt_ref[(h0 + 1) * A_QK:(h0 + 2) * A_QK, cs]], axis=1)], axis=0)
            ct2 = ct_ref[hp]
            st2 = _dot(k2, q_bd)
            inter2 = _dot(ct2.astype(BF16), q_bd)
            wvs = []
            for j in range(2):
                h = h0 + j
                st = st2[:, j * L:(j + 1) * L]
                inter = inter2[:, j * L:(j + 1) * L]
                arg = jnp.where(causal, r_cols[:, h:h + 1] - mm[h:h + 1, :], -jnp.inf)
                pt = (st * jnp.exp(arg)).astype(BF16)
                vext = jnp.concatenate([vt_ref[h * A_V:(h + 1) * A_V, cs], ones_rows], axis=0)
                tot = w_inter[h:h + 1, :] * inter + _dot(vext, pt)
                den = tot[A_V:A_V + 1, :]
                hh = tot[0:A_V, :] * (1.0 / jnp.maximum(jnp.abs(den), e_inv[h:h + 1, :]))
                ms = jnp.mean(hh * hh, axis=0, keepdims=True)
                hn = hh * lax.rsqrt(ms + EPS) * hg_ref[h * A_V:(h + 1) * A_V, :]
                og = ot_ref[h * A_V:(h + 1) * A_V, cs].astype(F32)
                out_ref[h * A_V:(h + 1) * A_V, cs] = (hn * jax.nn.sigmoid(og)).astype(BF16)
                wvs.append((vext.astype(F32) * w_upd[h:h + 1, :]).astype(BF16))
            first = lax.broadcasted_iota(jnp.int32, (L, LANES), 1) < A_QK
            zk = jnp.zeros((L, LANES), BF16)
            k_bd = jnp.concatenate([jnp.where(first, k2, zk), jnp.where(first, zk, k2)], axis=0)
            first_row = lax.broadcasted_iota(jnp.int32, (1, LANES), 1) < A_QK
            decay2 = jnp.where(first_row, decay[h0:h0 + 1, :], decay[h0 + 1:h0 + 2, :])
            ct_ref[hp] = decay2 * ct2 + _dot(jnp.concatenate(wvs, axis=1), k_bd)


def _mlstm(k, qt, vt, ot, bb, r, cm, head_g, chunk, tb):
    bsz, _, seq = qt.shape
    per_b = seq // tb
    col = lambda b, i: (b, 0, i)
    gate_spec = pl.BlockSpec((None, A_HEADS, tb), col)
    return pl.pallas_call(
        functools.partial(_mlstm_kernel, chunk=chunk, n_chunks=tb // chunk),
        grid=(bsz, per_b),
        in_specs=[
            pl.BlockSpec((tb, A_QK_W), lambda b, i: (b * per_b + i, 0)),
            pl.BlockSpec((None, A_QK_W, tb), col),
            pl.BlockSpec((None, A_V_W, tb), col),
            pl.BlockSpec((None, A_V_W, tb), col),
            gate_spec, gate_spec, gate_spec,
            pl.BlockSpec(head_g.shape, lambda b, i: (0, 0)),
        ],
        out_specs=pl.BlockSpec((None, A_V_W, tb), col),
        out_shape=jax.ShapeDtypeStruct((bsz, A_V_W, seq), BF16),
        scratch_shapes=[
            pltpu.VMEM((A_HEADS // 2, CT_ROWS, 2 * A_QK), F32),
            pltpu.VMEM((A_HEADS, 1), F32),
        ],
        compiler_params=pltpu.CompilerParams(
            dimension_semantics=("arbitrary", "arbitrary"), vmem_limit_bytes=VMEM_LIMIT),
        name="mlstm",
    )(k, qt, vt, ot, bb, r, cm, head_g)


def _post_kernel(x_ref, mix_ref, mod_ref, ng_ref, wo_ref, w1_ref, w2_ref, out_ref, *, ff_chunk):
    y = _dot_tn(mix_ref[...], wo_ref[...])
    x1 = x_ref[...] + mod_ref[2:3, :] * (_rms(y) * ng_ref[1:2, :])
    h = _rms(x1) * ng_ref[2:3, :]
    h = (h * (1.0 + mod_ref[4:5, :]) + mod_ref[3:4, :]).astype(BF16)
    acc = None
    for j in range(D_FF // ff_chunk):
        a = _dot(h, w1_ref[:, j * ff_chunk:(j + 1) * ff_chunk])
        a = jnp.square(jnp.maximum(a, 0.0)).astype(BF16)
        part = _dot(a, w2_ref[j * ff_chunk:(j + 1) * ff_chunk, :])
        acc = part if acc is None else acc + part
    out_ref[...] = x1 + mod_ref[5:6, :] * (_rms(acc) * ng_ref[3:4, :])


def _post(x, mix_t, mod, ng, wo, w1, w2, seq, tm):
    t, d = x.shape
    per_b = seq // tm
    row = lambda i: (i, 0)
    return pl.pallas_call(
        functools.partial(_post_kernel, ff_chunk=1024),
        grid=(t // tm,),
        in_specs=[
            pl.BlockSpec((tm, d), row),
            pl.BlockSpec((None, d, tm), lambda i: (i // per_b, 0, i % per_b)),
            pl.BlockSpec((None, 8, d), lambda i: (i // per_b, 0, 0)),
            _const_spec((4, d)),
            _const_spec(wo.shape),
            _const_spec(w1.shape),
            _const_spec(w2.shape),
        ],
        out_specs=pl.BlockSpec((tm, d), row),
        out_shape=jax.ShapeDtypeStruct((t, d), F32),
        compiler_params=pltpu.CompilerParams(
            dimension_semantics=("arbitrary",), vmem_limit_bytes=VMEM_LIMIT),
        name="post",
    )(x, mix_t, mod, ng, wo, w1, w2)


ROPE_HALF = B_ROPE // 2
Q_HEAD = B_NOPE + B_ROPE
V_ROWS_T = B_V + 16
WA_COLS = B_KV_LORA + 2 * B_ROPE


def _l1proj_kernel(x_ref, pos_ref, inv_ref, ng_ref, mod_ref, wa_ref, gl_ref, wbk_ref, wbv_ref,
                   wqa_ref, gq_ref, wq_ref, wqs_ref, q_ref, k_ref, v_ref, *, q_scale):
    tm = x_ref.shape[0]
    xn = _rms(x_ref[...])
    ang = inv_ref[...] * pos_ref[...].astype(F32)
    cos_h = jnp.cos(ang)
    sin_h = jnp.sin(ang)
    cos_t = jnp.concatenate([cos_h, cos_h], axis=0)
    sin_t = jnp.concatenate([-sin_h, sin_h], axis=0)

    hk = xn * ng_ref[0:1, :]
    hk = (hk * (1.0 + mod_ref[1:2, :]) + mod_ref[0:1, :]).astype(BF16)
    kva = _dot(hk, wa_ref[...])
    ckv = (_rms(kva[:, 0:B_KV_LORA]) * gl_ref[...]).astype(BF16)
    kr_t = kva[:, B_KV_LORA:WA_COLS].T
    k_rope_t = kr_t[0:B_ROPE, :] * cos_t + kr_t[B_ROPE:2 * B_ROPE, :] * sin_t
    k_rope = jnp.concatenate([k_rope_t, jnp.zeros((LANES - B_ROPE, tm), F32)], axis=0).T
    k_rope = k_rope.astype(BF16)
    k_nope = _dot(ckv, wbk_ref[...])
    for h in range(B_HEADS):
        k_ref[:, h * HEAD_CAT:h * HEAD_CAT + B_NOPE] = k_nope[:, h * B_NOPE:(h + 1) * B_NOPE].astype(BF16)
        k_ref[:, h * HEAD_CAT + B_NOPE:(h + 1) * HEAD_CAT] = k_rope
    v_t = _dot_nt(wbv_ref[...], ckv)
    ones = jnp.ones((V_ROWS_T - B_V, tm), BF16)
    for h in range(B_HEADS):
        v_ref[h * V_ROWS_T:h * V_ROWS_T + B_V, :] = v_t[h * B_V:(h + 1) * B_V, :].astype(BF16)
        v_ref[h * V_ROWS_T + B_V:(h + 1) * V_ROWS_T, :] = ones

    hq = xn * ng_ref[1:2, :]
    hq = (hq * (1.0 + mod_ref[3:4, :]) + mod_ref[2:3, :]).astype(BF16)
    cq = (_rms(_dot(hq, wqa_ref[...])) * gq_ref[...]).astype(BF16)
    q_t = _dot_nt(wq_ref[...], cq)
    qs_t = _dot_nt(wqs_ref[...], cq)
    zeros = jnp.zeros((HEAD_CAT - Q_HEAD, tm), BF16)
    for h in range(B_HEADS):
        r0 = h * Q_HEAD
        q_ref[h * HEAD_CAT:h * HEAD_CAT + B_NOPE, :] = (q_t[r0:r0 + B_NOPE, :] * q_scale).astype(BF16)
        rope = q_t[r0 + B_NOPE:r0 + Q_HEAD, :] * cos_t + qs_t[h * B_ROPE:(h + 1) * B_ROPE, :] * sin_t
        q_ref[h * HEAD_CAT + B_NOPE:h * HEAD_CAT + Q_HEAD, :] = (rope * q_scale).astype(BF16)
        q_ref[h * HEAD_CAT + Q_HEAD:(h + 1) * HEAD_CAT, :] = zeros


def _l1proj(x, pos, inv, ng, mod, wa, gl, wbk, wbv, wqa, gq, wq, wqs, seq, tm, tq):
    t, d = x.shape
    bsz = t // seq
    per_b = seq // tm
    per_q = tq // tm
    row = lambda i: (i, 0)
    q_scale = float((B_NOPE + B_ROPE) ** -0.5 * math.log2(math.e))
    consts = [inv, ng, None, wa, gl, wbk, wbv, wqa, gq, wq, wqs]
    in_specs = [pl.BlockSpec((tm, d), row),
                pl.BlockSpec((None, 1, tm), lambda i: (i // per_b, 0, i % per_b))]
    for a in consts:
        if a is None:
            in_specs.append(pl.BlockSpec((None, 8, d), lambda i: (i // per_b, 0, 0)))
        else:
            in_specs.append(_const_spec(a.shape))
    return pl.pallas_call(
        functools.partial(_l1proj_kernel, q_scale=q_scale),
        grid=(t // tm,),
        in_specs=in_specs,
        out_specs=[
            pl.BlockSpec((None, B_HEADS * HEAD_CAT, tm), lambda i: (i // per_b, 0, i % per_b)),
            pl.BlockSpec((tm, B_HEADS * HEAD_CAT), row),
            pl.BlockSpec((None, None, B_HEADS * V_ROWS_T, tm),
                         lambda i: (i // per_b, (i % per_b) // per_q, 0, i % per_q)),
        ],
        out_shape=[
            jax.ShapeDtypeStruct((bsz, B_HEADS * HEAD_CAT, seq), BF16),
            jax.ShapeDtypeStruct((t, B_HEADS * HEAD_CAT), BF16),
            jax.ShapeDtypeStruct((bsz, seq // tq, B_HEADS * V_ROWS_T, tq), BF16),
        ],
        compiler_params=pltpu.CompilerParams(
            dimension_semantics=("arbitrary",), vmem_limit_bytes=VMEM_LIMIT),
        name="l1proj",
    )(x, pos, inv, ng, mod, wa, gl, wbk, wbv, wqa, gq, wq, wqs)


def _attn_kernel(qt_ref, k_ref, vt_ref, o_ref, sa_ref, sb_ref, *, tq):
    qi = pl.program_id(2)

    def scores(j):
        rows = pl.ds(pl.multiple_of(j * tq, tq), tq)
        return _dot(k_ref[rows, :], qt_ref[...])

    def soft_pv(j, s, m, acc, masked=False):
        if masked:
            keys = lax.broadcasted_iota(jnp.int32, (tq, tq), 0)
            qrys = lax.broadcasted_iota(jnp.int32, (tq, tq), 1)
            s = jnp.where(keys <= qrys, s, -jnp.inf)
        m_new = jnp.maximum(m, jnp.max(s, axis=0, keepdims=True))
        p = jnp.exp2(s - m_new).astype(BF16)
        acc = jnp.exp2(m - m_new) * acc + _dot(vt_ref[j], p)
        return m_new, acc

    sa_ref[...] = scores(0)

    def pair(jj, carry):
        m, acc = carry
        j0 = 2 * jj
        sb_ref[...] = scores(j0 + 1)
        m, acc = soft_pv(j0, sa_ref[...], m, acc)
        sa_ref[...] = scores(j0 + 2)
        return soft_pv(j0 + 1, sb_ref[...], m, acc)

    init = (jnp.full((1, tq), -jnp.inf, F32), jnp.zeros((V_ROWS_T, tq), F32))
    m, acc = lax.fori_loop(0, qi // 2, pair, init)

    def odd_tail(m, acc):
        sb_ref[...] = scores(qi)
        m, acc = soft_pv(qi - 1, sa_ref[...], m, acc)
        return soft_pv(qi, sb_ref[...], m, acc, masked=True)

    def even_tail(m, acc):
        return soft_pv(qi, sa_ref[...], m, acc, masked=True)

    _, acc = lax.cond(qi % 2 == 1, odd_tail, even_tail, m, acc)
    o_ref[...] = (acc[0:B_V, :] * (1.0 / acc[B_V:B_V + 1, :])).astype(BF16)


def _attn(q_t, k, v_t, tq):
    b, s, _ = k.shape
    return pl.pallas_call(
        functools.partial(_attn_kernel, tq=tq),
        grid=(b, B_HEADS, s // tq),
        in_specs=[
            pl.BlockSpec((None, HEAD_CAT, tq), lambda bi, h, i: (bi, h, i)),
            pl.BlockSpec((None, s, HEAD_CAT), lambda bi, h, i: (bi, 0, h)),
            pl.BlockSpec((None, s // tq, V_ROWS_T, tq), lambda bi, h, i: (bi, 0, h, 0)),
        ],
        out_specs=pl.BlockSpec((None, B_V, tq), lambda bi, h, i: (bi, h, i)),
        out_shape=jax.ShapeDtypeStruct((b, B_HEADS * B_V, s), BF16),
        scratch_shapes=[pltpu.VMEM((tq, tq), F32), pltpu.VMEM((tq, tq), F32)],
        compiler_params=pltpu.CompilerParams(
            dimension_semantics=("arbitrary", "arbitrary", "arbitrary"),
            vmem_limit_bytes=VMEM_LIMIT),
        name="attn",
    )(q_t, k, v_t)


def _swap_halves(w):
    half = w.shape[-1] // 2
    return jnp.concatenate([w[..., half:], w[..., :half]], axis=-1)


def kernel(x, c, positions, ada_w, ada_b, norm_g, a_w_in, a_gate_b, a_head_g, a_w_out,
           kv_ada_w, kv_ada_b, kv_norm_g, kv_w_a, kv_latent_g, kv_w_b, b_w_q_a, b_q_latent_g,
           b_w_q_b, b_w_out, mlp_w1, mlp_w2):
    bsz, seq, d = x.shape
    t = bsz * seq
    x2d = x.reshape(t, d)

    c8 = jnp.pad(c, ((0, 8 - bsz), (0, 0)))
    ada = _adaln(c8, ada_w, ada_b[:, None, :], tn=1536)[:, :bsz]
    kv_ada = _adaln(c8, kv_ada_w[None], kv_ada_b[None, None, :], tn=1024)[0, :bsz]

    def mod_rows(vecs):
        rows = [v.reshape(bsz, 1, d) for v in vecs]
        rows += [jnp.zeros((bsz, 1, d), F32)] * (8 - len(rows))
        return jnp.concatenate(rows, axis=1)

    ada0 = [ada[0][:, i * d:(i + 1) * d] for i in range(6)]
    ada1 = [ada[1][:, i * d:(i + 1) * d] for i in range(6)]
    kv_shift, kv_scale = kv_ada[:, :d], kv_ada[:, d:]

    w_in = a_w_in[0]
    c0 = 2 * A_QK_W + A_V_W
    wk = w_in[:, A_QK_W:2 * A_QK_W].astype(BF16)
    wt = jnp.concatenate([
        w_in[:, :A_QK_W],
        w_in[:, 2 * A_QK_W:c0],
        w_in[:, c0 + 2 * A_HEADS:],
        w_in[:, c0:c0 + 2 * A_HEADS],
    ], axis=1).T.astype(BF16)
    k, qt, vt, ot, gt = _inproj(x2d, norm_g[0, 0][None], mod_rows(ada0[:2]), wk, wt, seq, tm=ROW_TM)
    gate_b = jnp.broadcast_to(a_gate_b[0].reshape(A_GATE_ROWS, 1), (A_GATE_ROWS, LANES))
    head_g = jnp.broadcast_to(a_head_g[0].reshape(A_V_W, 1), (A_V_W, MLSTM_CHUNK))
    bb, r, cm = _gates(gt, gate_b, MLSTM_CHUNK)
    mix0 = _mlstm(k, qt, vt, ot, bb, r, cm, head_g, MLSTM_CHUNK, tb=MLSTM_TB)
    x2d = _post(x2d, mix0, mod_rows(ada0), norm_g[0], a_w_out[0].astype(BF16),
                mlp_w1[0].astype(BF16), mlp_w2[0].astype(BF16), seq, tm=ROW_TM)

    tq = ATTN_TQ
    rope = kv_w_a[:, B_KV_LORA:]
    wa_cat = jnp.concatenate([kv_w_a[:, :B_KV_LORA], rope, _swap_halves(rope)], axis=1).astype(BF16)
    wb = kv_w_b.reshape(B_KV_LORA, B_HEADS, B_NOPE + B_V)
    wbk = wb[:, :, :B_NOPE].reshape(B_KV_LORA, -1).astype(BF16)
    wbv = wb[:, :, B_NOPE:].reshape(B_KV_LORA, -1).T.astype(BF16)
    wq = b_w_q_b[0].T.astype(BF16)
    wq_rope = b_w_q_b[0].reshape(B_Q_LORA, B_HEADS, Q_HEAD)[:, :, B_NOPE:]
    wqs = _swap_halves(wq_rope).reshape(B_Q_LORA, -1).T.astype(BF16)
    inv = ROPE_THETA ** (-jnp.arange(ROPE_HALF, dtype=F32) / ROPE_HALF)
    inv_rep = jnp.broadcast_to(inv[:, None], (ROPE_HALF, ROW_TM))
    ng1 = jnp.concatenate([kv_norm_g[None], norm_g[1, 0][None]], axis=0)
    mod1 = mod_rows([kv_shift, kv_scale, ada1[0], ada1[1]])
    q_t, k_cat, v_t = _l1proj(
        x2d, positions.reshape(bsz, 1, seq), inv_rep, ng1, mod1, wa_cat, kv_latent_g[None], wbk, wbv,
        b_w_q_a[0].astype(BF16), b_q_latent_g[0][None], wq, wqs, seq, tm=ROW_TM, tq=tq)
    o = _attn(q_t, k_cat.reshape(bsz, seq, -1), v_t, tq=tq)
    x2d = _post(x2d, o, mod_rows(ada1), norm_g[1], b_w_out[0].astype(BF16),
                mlp_w1[1].astype(BF16), mlp_w2[1].astype(BF16), seq, tm=ROW_TM)
    return x2d.reshape(bsz, seq, d)
```

```python
import functools
import math

import jax
import jax.numpy as jnp
from jax import lax
from jax.experimental import pallas as pl
from jax.experimental.pallas import tpu as pltpu

F32 = jnp.float32
BF16 = jnp.bfloat16

D_MODEL = 1024
D_FF = 4 * D_MODEL
EPS = 1e-6

A_HEADS = 8
A_QK = 64
A_V = 128
A_QK_W = A_HEADS * A_QK
A_V_W = A_HEADS * A_V
MLSTM_CHUNK = 256
MLSTM_GROUP = 4
MLSTM_TB = 1024

B_HEADS = 8
B_Q_LORA = 384
B_KV_LORA = 256
B_NOPE = 128
B_ROPE = 64
B_V = 128
ROPE_THETA = 10000.0
HEAD_CAT = 256
ATTN_TQ = 1024
ROW_TM = 1024

LANES = 128
VMEM_LIMIT = 56 * 1024 * 1024


def _dot(a, b):
    return jnp.dot(a, b, preferred_element_type=F32)


def _dot_nt(a, b):
    return lax.dot_general(a, b, (((1,), (1,)), ((), ())), preferred_element_type=F32)


def _dot_tn(a, b):
    return lax.dot_general(a, b, (((0,), (0,)), ((), ())), preferred_element_type=F32)


def _rms(x):
    return x * lax.rsqrt(jnp.mean(x * x, axis=-1, keepdims=True) + EPS)


def _const_spec(shape):
    nd = len(shape)
    return pl.BlockSpec(shape, lambda *_: (0,) * nd, pipeline_mode=pl.Buffered(1))


def _adaln_kernel(c_ref, w_ref, b_ref, o_ref):
    c = c_ref[...]
    cond = c * jax.nn.sigmoid(c)
    o_ref[...] = _dot(cond.astype(BF16), w_ref[...].astype(BF16)) + b_ref[...]


def _adaln(c8, w, b, tn):
    nl, d, n = w.shape
    return pl.pallas_call(
        _adaln_kernel,
        grid=(nl, n // tn),
        in_specs=[
            pl.BlockSpec((8, d), lambda l, j: (0, 0)),
            pl.BlockSpec((None, d, tn), lambda l, j: (l, 0, j)),
            pl.BlockSpec((None, 1, tn), lambda l, j: (l, 0, j)),
        ],
        out_specs=pl.BlockSpec((None, 8, tn), lambda l, j: (l, 0, j)),
        out_shape=jax.ShapeDtypeStruct((nl, 8, n), F32),
        compiler_params=pltpu.CompilerParams(
            dimension_semantics=("arbitrary", "arbitrary"), vmem_limit_bytes=VMEM_LIMIT),
        name="adaln",
    )(c8, w, b)


A_GATE_ROWS = 2 * A_HEADS
WT_Q0, WT_V0, WT_O0, WT_G0 = 0, A_QK_W, A_QK_W + A_V_W, A_QK_W + 2 * A_V_W
WT_ROWS = WT_G0 + A_GATE_ROWS


def _inproj_kernel(x_ref, g_ref, mod_ref, wk_ref, wt_ref, k_ref, qt_ref, vt_ref, ot_ref, gt_ref):
    h = _rms(x_ref[...]) * g_ref[...]
    h = (h * (1.0 + mod_ref[1:2, :]) + mod_ref[0:1, :]).astype(BF16)
    k_ref[...] = (_dot(h, wk_ref[...]) * (A_QK ** -0.5)).astype(BF16)
    feat = _dot_nt(wt_ref[...], h)
    qt_ref[...] = feat[WT_Q0:WT_V0, :].astype(BF16)
    vt_ref[...] = feat[WT_V0:WT_O0, :].astype(BF16)
    ot_ref[...] = feat[WT_O0:WT_G0, :].astype(BF16)
    gt_ref[...] = feat[WT_G0:WT_ROWS, :]


def _inproj(x, g, mod, wk, wt, seq, tm):
    t, d = x.shape
    bsz = t // seq
    per_b = seq // tm
    row = lambda i: (i, 0)
    col = lambda i: (i // per_b, 0, i % per_b)
    return pl.pallas_call(
        _inproj_kernel,
        grid=(t // tm,),
        in_specs=[
            pl.BlockSpec((tm, d), row),
            _const_spec((1, d)),
            pl.BlockSpec((None, 8, d), lambda i: (i // per_b, 0, 0)),
            _const_spec(wk.shape),
            _const_spec(wt.shape),
        ],
        out_specs=[
            pl.BlockSpec((tm, A_QK_W), row),
            pl.BlockSpec((None, A_QK_W, tm), col),
            pl.BlockSpec((None, A_V_W, tm), col),
            pl.BlockSpec((None, A_V_W, tm), col),
            pl.BlockSpec((None, A_GATE_ROWS, tm), col),
        ],
        out_shape=[
            jax.ShapeDtypeStruct((t, A_QK_W), BF16),
            jax.ShapeDtypeStruct((bsz, A_QK_W, seq), BF16),
            jax.ShapeDtypeStruct((bsz, A_V_W, seq), BF16),
            jax.ShapeDtypeStruct((bsz, A_V_W, seq), BF16),
            jax.ShapeDtypeStruct((bsz, A_GATE_ROWS, seq), F32),
        ],
        compiler_params=pltpu.CompilerParams(
            dimension_semantics=("arbitrary",), vmem_limit_bytes=VMEM_LIMIT),
        name="inproj",
    )(x, g, mod, wk, wt)


ONES_ROWS = 16
CT_ROWS = A_V + ONES_ROWS


def _scan_lanes(x, op, fill, seg):
    pos = lax.broadcasted_iota(jnp.int32, x.shape, 1) % seg
    sh = 1
    while sh < seg:
        x = op(x, jnp.where(pos >= sh, pltpu.roll(x, sh, axis=1), fill))
        sh *= 2
    return x


def _gates_kernel(gt_ref, gb_ref, bb_ref, r_ref, cm_ref, *, chunk):
    H = A_HEADS
    ig = gt_ref[0:H, :] + gb_ref[0:H, 0:1]
    fg = gt_ref[H:2 * H, :] + gb_ref[H:2 * H, 0:1]
    bb = _scan_lanes(jax.nn.log_sigmoid(fg), jnp.add, 0.0, chunk)
    r = ig - bb
    bb_ref[...] = bb
    r_ref[...] = r
    cm_ref[...] = _scan_lanes(r, jnp.maximum, -jnp.inf, chunk)


def _gates(gt, gate_b, chunk):
    bsz, _, seq = gt.shape
    spec = pl.BlockSpec((None, A_HEADS, seq), lambda b: (b, 0, 0))
    shape = jax.ShapeDtypeStruct((bsz, A_HEADS, seq), F32)
    return pl.pallas_call(
        functools.partial(_gates_kernel, chunk=chunk),
        grid=(bsz,),
        in_specs=[pl.BlockSpec((None, A_GATE_ROWS, seq), lambda b: (b, 0, 0)),
                  pl.BlockSpec(gate_b.shape, lambda b: (0, 0))],
        out_specs=[spec, spec, spec],
        out_shape=[shape, shape, shape],
        compiler_params=pltpu.CompilerParams(
            dimension_semantics=("arbitrary",), vmem_limit_bytes=VMEM_LIMIT),
        name="gates",
    )(gt, gate_b)


def _mlstm_kernel(k_ref, qt_ref, vt_ref, ot_ref, bb_ref, r_ref, cm_ref, hg_ref, out_ref,
                  ct_ref, m_ref, *, chunk, n_chunks):
    L = chunk
    H = A_HEADS

    @pl.when(pl.program_id(1) == 0)
    def _():
        ct_ref[...] = jnp.zeros_like(ct_ref)
        m_ref[...] = jnp.zeros_like(m_ref)

    src = lax.broadcasted_iota(jnp.int32, (L, L), 0)
    tgt = lax.broadcasted_iota(jnp.int32, (L, L), 1)
    causal = src <= tgt
    ones_rows = jnp.ones((ONES_ROWS, L), BF16)

    for c in range(n_chunks):
        cs = slice(c * L, (c + 1) * L)
        bb = bb_ref[:, cs]
        r = r_ref[:, cs]
        m_prev = m_ref[...]
        mm = jnp.maximum(m_prev, cm_ref[:, cs])
        e_inv = jnp.exp(-(bb + mm))
        w_inter = jnp.exp(m_prev - mm)
        mm_last = mm[:, L - 1:L]
        w_upd = jnp.exp(r - mm_last)
        decay = jnp.exp(m_prev - mm_last)
        m_ref[...] = bb[:, L - 1:L] + mm_last
        r_cols = jnp.concatenate([r, jnp.zeros((L - H, L), F32)], axis=0).T

        G = MLSTM_GROUP
        GW = G * A_QK
        lane_grp = lax.broadcasted_iota(jnp.int32, (L, GW), 1) // A_QK
        row_grp = lax.broadcasted_iota(jnp.int32, (1, GW), 1) // A_QK
        zq = jnp.zeros((A_QK, L), BF16)
        zp = jnp.zeros((L, L), BF16)
        for g in range(H // G):
            hb = g * G
            kg = k_ref[cs, g * GW:(g + 1) * GW]
            q_bd = jnp.concatenate([
                jnp.concatenate([qt_ref[(hb + i) * A_QK:(hb + i + 1) * A_QK, cs] if i == j else zq
                                 for j in range(G)], axis=1) for i in range(G)], axis=0)
            ctg = ct_ref[g]
            st_g = _dot(kg, q_bd)
            inter_g = _dot(ctg.astype(BF16), q_bd)
            pts, vexts = [], []
            for j in range(G):
                h = hb + j
                arg = jnp.where(causal, r_cols[:, h:h + 1] - mm[h:h + 1, :], -jnp.inf)
                pts.append((st_g[:, j * L:(j + 1) * L] * jnp.exp(arg)).astype(BF16))
                vexts.append(jnp.concatenate([vt_ref[h * A_V:(h + 1) * A_V, cs], ones_rows], axis=0))
            wvs = []
            for j0 in range(0, G, 2):
                p_bd = jnp.concatenate([jnp.concatenate([pts[j0], zp], axis=1),
                                        jnp.concatenate([zp, pts[j0 + 1]], axis=1)], axis=0)
                intra2 = _dot(jnp.concatenate([vexts[j0], vexts[j0 + 1]], axis=1), p_bd)
                for j in (j0, j0 + 1):
                    h = hb + j
                    tot = (w_inter[h:h + 1, :] * inter_g[:, j * L:(j + 1) * L]
                           + intra2[:, (j - j0) * L:(j - j0 + 1) * L])
                    den = tot[A_V:A_V + 1, :]
                    hh = tot[0:A_V, :] * (1.0 / jnp.maximum(jnp.abs(den), e_inv[h:h + 1, :]))
                    ms = jnp.mean(hh * hh, axis=0, keepdims=True)
                    hn = hh * lax.rsqrt(ms + EPS) * hg_ref[h * A_V:(h + 1) * A_V, :]
                    og = ot_ref[h * A_V:(h + 1) * A_V, cs].astype(F32)
                    out_ref[h * A_V:(h + 1) * A_V, cs] = (hn * jax.nn.sigmoid(og)).astype(BF16)
                    wvs.append((vexts[j].astype(F32) * w_upd[h:h + 1, :]).astype(BF16))
            zk = jnp.zeros((L, GW), BF16)
            k_bd = jnp.concatenate([jnp.where(lane_grp == i, kg, zk) for i in range(G)], axis=0)
            decay_g = decay[hb:hb + 1, :]
            for i in range(1, G):
                decay_g = jnp.where(row_grp >= i, decay[hb + i:hb + i + 1, :], decay_g)
            ct_ref[g] = decay_g * ctg + _dot(jnp.concatenate(wvs, axis=1), k_bd)


def _mlstm(k, qt, vt, ot, bb, r, cm, head_g, chunk, tb):
    bsz, _, seq = qt.shape
    per_b = seq // tb
    col = lambda b, i: (b, 0, i)
    gate_spec = pl.BlockSpec((None, A_HEADS, tb), col)
    return pl.pallas_call(
        functools.partial(_mlstm_kernel, chunk=chunk, n_chunks=tb // chunk),
        grid=(bsz, per_b),
        in_specs=[
            pl.BlockSpec((tb, A_QK_W), lambda b, i: (b * per_b + i, 0)),
            pl.BlockSpec((None, A_QK_W, tb), col),
            pl.BlockSpec((None, A_V_W, tb), col),
            pl.BlockSpec((None, A_V_W, tb), col),
            gate_spec, gate_spec, gate_spec,
            pl.BlockSpec(head_g.shape, lambda b, i: (0, 0)),
        ],
        out_specs=pl.BlockSpec((None, A_V_W, tb), col),
        out_shape=jax.ShapeDtypeStruct((bsz, A_V_W, seq), BF16),
        scratch_shapes=[
            pltpu.VMEM((A_HEADS // MLSTM_GROUP, CT_ROWS, MLSTM_GROUP * A_QK), F32),
            pltpu.VMEM((A_HEADS, 1), F32),
        ],
        compiler_params=pltpu.CompilerParams(
            dimension_semantics=("arbitrary", "arbitrary"), vmem_limit_bytes=VMEM_LIMIT),
        name="mlstm",
    )(k, qt, vt, ot, bb, r, cm, head_g)


def _post_kernel(x_ref, mix_ref, mod_ref, ng_ref, wo_ref, w1_ref, w2_ref, out_ref, *, ff_chunk):
    y = _dot_tn(mix_ref[...], wo_ref[...])
    x1 = x_ref[...] + mod_ref[2:3, :] * (_rms(y) * ng_ref[1:2, :])
    h = _rms(x1) * ng_ref[2:3, :]
    h = (h * (1.0 + mod_ref[4:5, :]) + mod_ref[3:4, :]).astype(BF16)
    acc = None
    for j in range(D_FF // ff_chunk):
        a = _dot(h, w1_ref[:, j * ff_chunk:(j + 1) * ff_chunk])
        a = jnp.square(jnp.maximum(a, 0.0)).astype(BF16)
        part = _dot(a, w2_ref[j * ff_chunk:(j + 1) * ff_chunk, :])
        acc = part if acc is None else acc + part
    out_ref[...] = x1 + mod_ref[5:6, :] * (_rms(acc) * ng_ref[3:4, :])


def _post(x, mix_t, mod, ng, wo, w1, w2, layer, seq, tm):
    t, d = x.shape
    per_b = seq // tm
    row = lambda i: (i, 0)
    layer_spec = lambda w: pl.BlockSpec((None,) + w.shape[1:], lambda i: (layer, 0, 0),
                                        pipeline_mode=pl.Buffered(1))
    return pl.pallas_call(
        functools.partial(_post_kernel, ff_chunk=1024),
        grid=(t // tm,),
        in_specs=[
            pl.BlockSpec((tm, d), row),
            pl.BlockSpec((None, d, tm), lambda i: (i // per_b, 0, i % per_b)),
            pl.BlockSpec((None, 8, d), lambda i: (i // per_b, 0, 0)),
            _const_spec((4, d)),
            _const_spec(wo.shape),
            layer_spec(w1),
            layer_spec(w2),
        ],
        out_specs=pl.BlockSpec((tm, d), row),
        out_shape=jax.ShapeDtypeStruct((t, d), F32),
        compiler_params=pltpu.CompilerParams(
            dimension_semantics=("arbitrary",), vmem_limit_bytes=VMEM_LIMIT),
        name="post",
    )(x, mix_t, mod, ng, wo, w1, w2)


ROPE_HALF = B_ROPE // 2
Q_HEAD = B_NOPE + B_ROPE
V_ROWS_T = B_V + 16
WA_COLS = B_KV_LORA + 2 * B_ROPE


def _l1proj_kernel(x_ref, pos_ref, inv_ref, ng_ref, mod_ref, wa_ref, gl_ref, wbk_ref, wbv_ref,
                   wqa_ref, gq_ref, wq_ref, wqs_ref, q_ref, k_ref, v_ref, *, q_scale):
    tm = x_ref.shape[0]
    xn = _rms(x_ref[...])
    ang = inv_ref[...] * pos_ref[...].astype(F32)
    cos_h = jnp.cos(ang)
    sin_h = jnp.sin(ang)
    cos_t = jnp.concatenate([cos_h, cos_h], axis=0)
    sin_t = jnp.concatenate([-sin_h, sin_h], axis=0)

    hk = xn * ng_ref[0:1, :]
    hk = (hk * (1.0 + mod_ref[1:2, :]) + mod_ref[0:1, :]).astype(BF16)
    kva = _dot(hk, wa_ref[...])
    ckv = (_rms(kva[:, 0:B_KV_LORA]) * gl_ref[...]).astype(BF16)
    kr_t = kva[:, B_KV_LORA:WA_COLS].T
    k_rope_t = kr_t[0:B_ROPE, :] * cos_t + kr_t[B_ROPE:2 * B_ROPE, :] * sin_t
    k_rope = jnp.concatenate([k_rope_t, jnp.zeros((LANES - B_ROPE, tm), F32)], axis=0).T
    k_rope = k_rope.astype(BF16)
    k_nope = _dot(ckv, wbk_ref[...])
    for h in range(B_HEADS):
        k_ref[:, h * HEAD_CAT:h * HEAD_CAT + B_NOPE] = k_nope[:, h * B_NOPE:(h + 1) * B_NOPE].astype(BF16)
        k_ref[:, h * HEAD_CAT + B_NOPE:(h + 1) * HEAD_CAT] = k_rope
    v_t = _dot_nt(wbv_ref[...], ckv)
    ones = jnp.ones((V_ROWS_T - B_V, tm), BF16)
    for h in range(B_HEADS):
        v_ref[h * V_ROWS_T:h * V_ROWS_T + B_V, :] = v_t[h * B_V:(h + 1) * B_V, :].astype(BF16)
        v_ref[h * V_ROWS_T + B_V:(h + 1) * V_ROWS_T, :] = ones

    hq = xn * ng_ref[1:2, :]
    hq = (hq * (1.0 + mod_ref[3:4, :]) + mod_ref[2:3, :]).astype(BF16)
    cq = (_rms(_dot(hq, wqa_ref[...])) * gq_ref[...]).astype(BF16)
    q_t = _dot_nt(wq_ref[...], cq)
    qs_t = _dot_nt(wqs_ref[...], cq)
    zeros = jnp.zeros((HEAD_CAT - Q_HEAD, tm), BF16)
    for h in range(B_HEADS):
        r0 = h * Q_HEAD
        q_ref[h * HEAD_CAT:h * HEAD_CAT + B_NOPE, :] = (q_t[r0:r0 + B_NOPE, :] * q_scale).astype(BF16)
        rope = q_t[r0 + B_NOPE:r0 + Q_HEAD, :] * cos_t + qs_t[h * B_ROPE:(h + 1) * B_ROPE, :] * sin_t
        q_ref[h * HEAD_CAT + B_NOPE:h * HEAD_CAT + Q_HEAD, :] = (rope * q_scale).astype(BF16)
        q_ref[h * HEAD_CAT + Q_HEAD:(h + 1) * HEAD_CAT, :] = zeros


def _l1proj(x, pos, inv, ng, mod, wa, gl, wbk, wbv, wqa, gq, wq, wqs, seq, tm, tq):
    t, d = x.shape
    bsz = t // seq
    per_b = seq // tm
    per_q = tq // tm
    row = lambda i: (i, 0)
    q_scale = float((B_NOPE + B_ROPE) ** -0.5 * math.log2(math.e))
    consts = [inv, ng, None, wa, gl, wbk, wbv, wqa, gq, wq, wqs]
    in_specs = [pl.BlockSpec((tm, d), row),
                pl.BlockSpec((None, 1, tm), lambda i: (i // per_b, 0, i % per_b))]
    for a in consts:
        if a is None:
            in_specs.append(pl.BlockSpec((None, 8, d), lambda i: (i // per_b, 0, 0)))
        else:
            in_specs.append(_const_spec(a.shape))
    return pl.pallas_call(
        functools.partial(_l1proj_kernel, q_scale=q_scale),
        grid=(t // tm,),
        in_specs=in_specs,
        out_specs=[
            pl.BlockSpec((None, B_HEADS * HEAD_CAT, tm), lambda i: (i // per_b, 0, i % per_b)),
            pl.BlockSpec((tm, B_HEADS * HEAD_CAT), row),
            pl.BlockSpec((None, None, B_HEADS * V_ROWS_T, tm),
                         lambda i: (i // per_b, (i % per_b) // per_q, 0, i % per_q)),
        ],
        out_shape=[
            jax.ShapeDtypeStruct((bsz, B_HEADS * HEAD_CAT, seq), BF16),
            jax.ShapeDtypeStruct((t, B_HEADS * HEAD_CAT), BF16),
            jax.ShapeDtypeStruct((bsz, seq // tq, B_HEADS * V_ROWS_T, tq), BF16),
        ],
        compiler_params=pltpu.CompilerParams(
            dimension_semantics=("arbitrary",), vmem_limit_bytes=VMEM_LIMIT),
        name="l1proj",
    )(x, pos, inv, ng, mod, wa, gl, wbk, wbv, wqa, gq, wq, wqs)


def _attn_kernel(qt_ref, k_ref, vt_ref, o_ref, sa_ref, sb_ref, *, tq):
    qi = pl.program_id(2)

    def scores(j):
        rows = pl.ds(pl.multiple_of(j * tq, tq), tq)
        return _dot(k_ref[rows, :], qt_ref[...])

    def soft_pv(j, s, m, acc, masked=False):
        if masked:
            keys = lax.broadcasted_iota(jnp.int32, (tq, tq), 0)
            qrys = lax.broadcasted_iota(jnp.int32, (tq, tq), 1)
            s = jnp.where(keys <= qrys, s, -jnp.inf)
        m_new = jnp.maximum(m, jnp.max(s, axis=0, keepdims=True))
        p = jnp.exp2(s - m_new).astype(BF16)
        acc = jnp.exp2(m - m_new) * acc + _dot(vt_ref[j], p)
        return m_new, acc

    sa_ref[...] = scores(0)

    def pair(jj, carry):
        m, acc = carry
        j0 = 2 * jj
        sb_ref[...] = scores(j0 + 1)
        m, acc = soft_pv(j0, sa_ref[...], m, acc)
        sa_ref[...] = scores(j0 + 2)
        return soft_pv(j0 + 1, sb_ref[...], m, acc)

    init = (jnp.full((1, tq), -jnp.inf, F32), jnp.zeros((V_ROWS_T, tq), F32))
    m, acc = lax.fori_loop(0, qi // 2, pair, init)

    def odd_tail(m, acc):
        sb_ref[...] = scores(qi)
        m, acc = soft_pv(qi - 1, sa_ref[...], m, acc)
        return soft_pv(qi, sb_ref[...], m, acc, masked=True)

    def even_tail(m, acc):
        return soft_pv(qi, sa_ref[...], m, acc, masked=True)

    _, acc = lax.cond(qi % 2 == 1, odd_tail, even_tail, m, acc)
    o_ref[...] = (acc[0:B_V, :] * (1.0 / acc[B_V:B_V + 1, :])).astype(BF16)


def _attn(q_t, k, v_t, tq):
    b, s, _ = k.shape
    return pl.pallas_call(
        functools.partial(_attn_kernel, tq=tq),
        grid=(b, B_HEADS, s // tq),
        in_specs=[
            pl.BlockSpec((None, HEAD_CAT, tq), lambda bi, h, i: (bi, h, i)),
            pl.BlockSpec((None, s, HEAD_CAT), lambda bi, h, i: (bi, 0, h)),
            pl.BlockSpec((None, s // tq, V_ROWS_T, tq), lambda bi, h, i: (bi, 0, h, 0)),
        ],
        out_specs=pl.BlockSpec((None, B_V, tq), lambda bi, h, i: (bi, h, i)),
        out_shape=jax.ShapeDtypeStruct((b, B_HEADS * B_V, s), BF16),
        scratch_shapes=[pltpu.VMEM((tq, tq), F32), pltpu.VMEM((tq, tq), F32)],
        compiler_params=pltpu.CompilerParams(
            dimension_semantics=("arbitrary", "arbitrary", "arbitrary"),
            vmem_limit_bytes=VMEM_LIMIT),
        name="attn",
    )(q_t, k, v_t)


def _swap_halves(w):
    half = w.shape[-1] // 2
    return jnp.concatenate([w[..., half:], w[..., :half]], axis=-1)


def kernel(x, c, positions, ada_w, ada_b, norm_g, a_w_in, a_gate_b, a_head_g, a_w_out,
           kv_ada_w, kv_ada_b, kv_norm_g, kv_w_a, kv_latent_g, kv_w_b, b_w_q_a, b_q_latent_g,
           b_w_q_b, b_w_out, mlp_w1, mlp_w2):
    bsz, seq, d = x.shape
    t = bsz * seq
    x2d = x.reshape(t, d)

    c8 = jnp.pad(c, ((0, 8 - bsz), (0, 0)))
    ada = _adaln(c8, ada_w, ada_b[:, None, :], tn=1536)[:, :bsz]
    kv_ada = _adaln(c8, kv_ada_w[None], kv_ada_b[None, None, :], tn=1024)[0, :bsz]

    def mod_rows(vecs):
        rows = [v.reshape(bsz, 1, d) for v in vecs]
        rows += [jnp.zeros((bsz, 1, d), F32)] * (8 - len(rows))
        return jnp.concatenate(rows, axis=1)

    ada0 = [ada[0][:, i * d:(i + 1) * d] for i in range(6)]
    ada1 = [ada[1][:, i * d:(i + 1) * d] for i in range(6)]
    kv_shift, kv_scale = kv_ada[:, :d], kv_ada[:, d:]

    w_in = a_w_in[0]
    c0 = 2 * A_QK_W + A_V_W
    wk = w_in[:, A_QK_W:2 * A_QK_W].astype(BF16)
    wt = jnp.concatenate([
        w_in[:, :A_QK_W],
        w_in[:, 2 * A_QK_W:c0],
        w_in[:, c0 + 2 * A_HEADS:],
        w_in[:, c0:c0 + 2 * A_HEADS],
    ], axis=1).T.astype(BF16)
    k, qt, vt, ot, gt = _inproj(x2d, norm_g[0, 0][None], mod_rows(ada0[:2]), wk, wt, seq, tm=ROW_TM)
    gate_b = jnp.broadcast_to(a_gate_b[0].reshape(A_GATE_ROWS, 1), (A_GATE_ROWS, LANES))
    head_g = jnp.broadcast_to(a_head_g[0].reshape(A_V_W, 1), (A_V_W, MLSTM_CHUNK))
    bb, r, cm = _gates(gt, gate_b, MLSTM_CHUNK)
    mix0 = _mlstm(k, qt, vt, ot, bb, r, cm, head_g, MLSTM_CHUNK, tb=MLSTM_TB)
    w1_all, w2_all = mlp_w1.astype(BF16), mlp_w2.astype(BF16)
    x2d = _post(x2d, mix0, mod_rows(ada0), norm_g[0], a_w_out[0].astype(BF16), w1_all, w2_all, 0,
                seq, tm=ROW_TM)

    tq = ATTN_TQ
    rope = kv_w_a[:, B_KV_LORA:]
    wa_cat = jnp.concatenate([kv_w_a[:, :B_KV_LORA], rope, _swap_halves(rope)], axis=1).astype(BF16)
    wb = kv_w_b.reshape(B_KV_LORA, B_HEADS, B_NOPE + B_V)
    wbk = wb[:, :, :B_NOPE].reshape(B_KV_LORA, -1).astype(BF16)
    wbv = wb[:, :, B_NOPE:].reshape(B_KV_LORA, -1).T.astype(BF16)
    wq = b_w_q_b[0].T.astype(BF16)
    wq_rope = b_w_q_b[0].reshape(B_Q_LORA, B_HEADS, Q_HEAD)[:, :, B_NOPE:]
    wqs = _swap_halves(wq_rope).reshape(B_Q_LORA, -1).T.astype(BF16)
    inv = ROPE_THETA ** (-jnp.arange(ROPE_HALF, dtype=F32) / ROPE_HALF)
    inv_rep = jnp.broadcast_to(inv[:, None], (ROPE_HALF, ROW_TM))
    ng1 = jnp.concatenate([kv_norm_g[None], norm_g[1, 0][None]], axis=0)
    mod1 = mod_rows([kv_shift, kv_scale, ada1[0], ada1[1]])
    q_t, k_cat, v_t = _l1proj(
        x2d, positions.reshape(bsz, 1, seq), inv_rep, ng1, mod1, wa_cat, kv_latent_g[None], wbk, wbv,
        b_w_q_a[0].astype(BF16), b_q_latent_g[0][None], wq, wqs, seq, tm=ROW_TM, tq=tq)
    o = _attn(q_t, k_cat.reshape(bsz, seq, -1), v_t, tq=tq)
    x2d = _post(x2d, o, mod_rows(ada1), norm_g[1], b_w_out[0].astype(BF16), w1_all, w2_all, 1,
                seq, tm=ROW_TM)
    return x2d.reshape(bsz, seq, d)
```

```python
import functools
import math

import jax
import jax.numpy as jnp
from jax import lax
from jax.experimental import pallas as pl
from jax.experimental.pallas import tpu as pltpu

F32 = jnp.float32
BF16 = jnp.bfloat16

D_MODEL = 1024
D_FF = 4 * D_MODEL
EPS = 1e-6
LOG2E = math.log2(math.e)

A_HEADS = 8
A_QK = 64
A_V = 128
A_QK_W = A_HEADS * A_QK
A_V_W = A_HEADS * A_V
MLSTM_CHUNK = 256
MLSTM_GROUP = 4
MLSTM_TB = 1024

B_HEADS = 8
B_Q_LORA = 384
B_KV_LORA = 256
B_NOPE = 128
B_ROPE = 64
B_V = 128
ROPE_THETA = 10000.0
HEAD_CAT = 256
ATTN_TQ = 1024
ROW_TM = 1024
POST_TM = 512
ROW_SPLIT = 2

LANES = 128
VMEM_LIMIT = 56 * 1024 * 1024


def _dot(a, b):
    return jnp.dot(a, b, preferred_element_type=F32)


def _dot_nt(a, b):
    return lax.dot_general(a, b, (((1,), (1,)), ((), ())), preferred_element_type=F32)


def _dot_tn(a, b):
    return lax.dot_general(a, b, (((0,), (0,)), ((), ())), preferred_element_type=F32)


def _rms(x):
    return x * lax.rsqrt(jnp.mean(x * x, axis=-1, keepdims=True) + EPS)


def _const_spec(shape):
    nd = len(shape)
    return pl.BlockSpec(shape, lambda *_: (0,) * nd, pipeline_mode=pl.Buffered(1))


def _adaln_kernel(c_ref, w_ref, b_ref, o_ref):
    c = c_ref[...]
    cond = c * jax.nn.sigmoid(c)
    o_ref[...] = _dot(cond.astype(BF16), w_ref[...].astype(BF16)) + b_ref[...]


def _adaln(c8, w, b, tn):
    nl, d, n = w.shape
    return pl.pallas_call(
        _adaln_kernel,
        grid=(nl, n // tn),
        in_specs=[
            pl.BlockSpec((8, d), lambda l, j: (0, 0)),
            pl.BlockSpec((None, d, tn), lambda l, j: (l, 0, j)),
            pl.BlockSpec((None, 1, tn), lambda l, j: (l, 0, j)),
        ],
        out_specs=pl.BlockSpec((None, 8, tn), lambda l, j: (l, 0, j)),
        out_shape=jax.ShapeDtypeStruct((nl, 8, n), F32),
        compiler_params=pltpu.CompilerParams(
            dimension_semantics=("arbitrary", "arbitrary"), vmem_limit_bytes=VMEM_LIMIT),
        name="adaln",
    )(c8, w, b)


A_GATE_ROWS = 2 * A_HEADS
WT_Q0, WT_V0, WT_O0, WT_G0 = 0, A_QK_W, A_QK_W + A_V_W, A_QK_W + 2 * A_V_W
WT_ROWS = WT_G0 + A_GATE_ROWS


def _inproj_kernel(x_ref, g_ref, mod_ref, wk_ref, wt_ref, k_ref, qt_ref, vt_ref, ot_ref, gt_ref):
    ts = x_ref.shape[0] // ROW_SPLIT
    for i in range(ROW_SPLIT):
        p = slice(i * ts, (i + 1) * ts)
        h = _rms(x_ref[p, :]) * g_ref[...]
        h = (h * (1.0 + mod_ref[1:2, :]) + mod_ref[0:1, :]).astype(BF16)
        k_ref[p, :] = (_dot(h, wk_ref[...]) * (A_QK ** -0.5)).astype(BF16)
        feat = _dot_nt(wt_ref[...], h)
        qt_ref[:, p] = feat[WT_Q0:WT_V0, :].astype(BF16)
        vt_ref[:, p] = feat[WT_V0:WT_O0, :].astype(BF16)
        ot_ref[:, p] = feat[WT_O0:WT_G0, :].astype(BF16)
        gt_ref[:, p] = feat[WT_G0:WT_ROWS, :]


def _inproj(x, g, mod, wk, wt, seq, tm):
    t, d = x.shape
    bsz = t // seq
    per_b = seq // tm
    row = lambda i: (i, 0)
    col = lambda i: (i // per_b, 0, i % per_b)
    return pl.pallas_call(
        _inproj_kernel,
        grid=(t // tm,),
        in_specs=[
            pl.BlockSpec((tm, d), row),
            _const_spec((1, d)),
            pl.BlockSpec((None, 8, d), lambda i: (i // per_b, 0, 0)),
            _const_spec(wk.shape),
            _const_spec(wt.shape),
        ],
        out_specs=[
            pl.BlockSpec((tm, A_QK_W), row),
            pl.BlockSpec((None, A_QK_W, tm), col),
            pl.BlockSpec((None, A_V_W, tm), col),
            pl.BlockSpec((None, A_V_W, tm), col),
            pl.BlockSpec((None, A_GATE_ROWS, tm), col),
        ],
        out_shape=[
            jax.ShapeDtypeStruct((t, A_QK_W), BF16),
            jax.ShapeDtypeStruct((bsz, A_QK_W, seq), BF16),
            jax.ShapeDtypeStruct((bsz, A_V_W, seq), BF16),
            jax.ShapeDtypeStruct((bsz, A_V_W, seq), BF16),
            jax.ShapeDtypeStruct((bsz, A_GATE_ROWS, seq), F32),
        ],
        compiler_params=pltpu.CompilerParams(
            dimension_semantics=("arbitrary",), vmem_limit_bytes=VMEM_LIMIT),
        name="inproj",
    )(x, g, mod, wk, wt)


ONES_ROWS = 16
CT_ROWS = A_V + ONES_ROWS


def _scan_lanes(x, op, fill, seg):
    pos = lax.broadcasted_iota(jnp.int32, x.shape, 1) % seg
    sh = 1
    while sh < seg:
        x = op(x, jnp.where(pos >= sh, pltpu.roll(x, sh, axis=1), fill))
        sh *= 2
    return x


def _gates_kernel(gt_ref, gb_ref, bb_ref, r_ref, cm_ref, *, chunk):
    H = A_HEADS
    ig = gt_ref[0:H, :] + gb_ref[0:H, 0:1]
    fg = gt_ref[H:2 * H, :] + gb_ref[H:2 * H, 0:1]
    bb = _scan_lanes(jax.nn.log_sigmoid(fg) * LOG2E, jnp.add, 0.0, chunk)
    r = ig * LOG2E - bb
    bb_ref[...] = bb
    r_ref[...] = r
    cm_ref[...] = _scan_lanes(r, jnp.maximum, -jnp.inf, chunk)


def _gates(gt, gate_b, chunk):
    bsz, _, seq = gt.shape
    spec = pl.BlockSpec((None, A_HEADS, seq), lambda b: (b, 0, 0))
    shape = jax.ShapeDtypeStruct((bsz, A_HEADS, seq), F32)
    return pl.pallas_call(
        functools.partial(_gates_kernel, chunk=chunk),
        grid=(bsz,),
        in_specs=[pl.BlockSpec((None, A_GATE_ROWS, seq), lambda b: (b, 0, 0)),
                  pl.BlockSpec(gate_b.shape, lambda b: (0, 0))],
        out_specs=[spec, spec, spec],
        out_shape=[shape, shape, shape],
        compiler_params=pltpu.CompilerParams(
            dimension_semantics=("arbitrary",), vmem_limit_bytes=VMEM_LIMIT),
        name="gates",
    )(gt, gate_b)


def _mlstm_kernel(k_ref, qt_ref, vt_ref, ot_ref, bb_ref, r_ref, cm_ref, hg_ref, out_ref,
                  ct_ref, m_ref, *, chunk, n_chunks):
    L = chunk
    H = A_HEADS

    @pl.when(pl.program_id(1) == 0)
    def _():
        ct_ref[...] = jnp.zeros_like(ct_ref)
        m_ref[...] = jnp.zeros_like(m_ref)

    src = lax.broadcasted_iota(jnp.int32, (L, L), 0)
    tgt = lax.broadcasted_iota(jnp.int32, (L, L), 1)
    causal = src <= tgt
    ones_rows = jnp.ones((ONES_ROWS, L), BF16)

    for c in range(n_chunks):
        cs = slice(c * L, (c + 1) * L)
        bb = bb_ref[:, cs]
        r = r_ref[:, cs]
        m_prev = m_ref[...]
        mm = jnp.maximum(m_prev, cm_ref[:, cs])
        e_inv = jnp.exp2(-(bb + mm))
        w_inter = jnp.exp2(m_prev - mm)
        mm_last = mm[:, L - 1:L]
        w_upd = jnp.exp2(r - mm_last)
        decay = jnp.exp2(m_prev - mm_last)
        m_ref[...] = bb[:, L - 1:L] + mm_last
        r_cols = jnp.concatenate([r, jnp.zeros((L - H, L), F32)], axis=0).T

        G = MLSTM_GROUP
        GW = G * A_QK
        lane_grp = lax.broadcasted_iota(jnp.int32, (L, GW), 1) // A_QK
        row_grp = lax.broadcasted_iota(jnp.int32, (1, GW), 1) // A_QK
        zq = jnp.zeros((A_QK, L), BF16)
        zp = jnp.zeros((L, L), BF16)
        for g in range(H // G):
            hb = g * G
            kg = k_ref[cs, g * GW:(g + 1) * GW]
            q_bd = jnp.concatenate([
                jnp.concatenate([qt_ref[(hb + i) * A_QK:(hb + i + 1) * A_QK, cs] if i == j else zq
                                 for j in range(G)], axis=1) for i in range(G)], axis=0)
            ctg = ct_ref[g]
            st_g = _dot(kg, q_bd)
            inter_g = _dot(ctg.astype(BF16), q_bd)
            pts, vexts = [], []
            for j in range(G):
                h = hb + j
                arg = jnp.where(causal, r_cols[:, h:h + 1] - mm[h:h + 1, :], -jnp.inf)
                pts.append((st_g[:, j * L:(j + 1) * L] * jnp.exp2(arg)).astype(BF16))
                vexts.append(jnp.concatenate([vt_ref[h * A_V:(h + 1) * A_V, cs], ones_rows], axis=0))
            wvs = []
            for j0 in range(0, G, 2):
                p_bd = jnp.concatenate([jnp.concatenate([pts[j0], zp], axis=1),
                                        jnp.concatenate([zp, pts[j0 + 1]], axis=1)], axis=0)
                intra2 = _dot(jnp.concatenate([vexts[j0], vexts[j0 + 1]], axis=1), p_bd)
                for j in (j0, j0 + 1):
                    h = hb + j
                    tot = (w_inter[h:h + 1, :] * inter_g[:, j * L:(j + 1) * L]
                           + intra2[:, (j - j0) * L:(j - j0 + 1) * L])
                    den = tot[A_V:A_V + 1, :]
                    hh = tot[0:A_V, :] * (1.0 / jnp.maximum(jnp.abs(den), e_inv[h:h + 1, :]))
                    ms = jnp.mean(hh * hh, axis=0, keepdims=True)
                    hn = hh * lax.rsqrt(ms + EPS) * hg_ref[h * A_V:(h + 1) * A_V, :]
                    og = ot_ref[h * A_V:(h + 1) * A_V, cs].astype(F32)
                    out_ref[h * A_V:(h + 1) * A_V, cs] = (hn * jax.nn.sigmoid(og)).astype(BF16)
                    wvs.append((vexts[j].astype(F32) * w_upd[h:h + 1, :]).astype(BF16))
            zk = jnp.zeros((L, GW), BF16)
            k_bd = jnp.concatenate([jnp.where(lane_grp == i, kg, zk) for i in range(G)], axis=0)
            decay_g = decay[hb:hb + 1, :]
            for i in range(1, G):
                decay_g = jnp.where(row_grp >= i, decay[hb + i:hb + i + 1, :], decay_g)
            ct_ref[g] = decay_g * ctg + _dot(jnp.concatenate(wvs, axis=1), k_bd)


def _mlstm(k, qt, vt, ot, bb, r, cm, head_g, chunk, tb):
    bsz, _, seq = qt.shape
    per_b = seq // tb
    col = lambda b, i: (b, 0, i)
    gate_spec = pl.BlockSpec((None, A_HEADS, tb), col)
    return pl.pallas_call(
        functools.partial(_mlstm_kernel, chunk=chunk, n_chunks=tb // chunk),
        grid=(bsz, per_b),
        in_specs=[
            pl.BlockSpec((tb, A_QK_W), lambda b, i: (b * per_b + i, 0)),
            pl.BlockSpec((None, A_QK_W, tb), col),
            pl.BlockSpec((None, A_V_W, tb), col),
            pl.BlockSpec((None, A_V_W, tb), col),
            gate_spec, gate_spec, gate_spec,
            pl.BlockSpec(head_g.shape, lambda b, i: (0, 0)),
        ],
        out_specs=pl.BlockSpec((None, A_V_W, tb), col),
        out_shape=jax.ShapeDtypeStruct((bsz, A_V_W, seq), BF16),
        scratch_shapes=[
            pltpu.VMEM((A_HEADS // MLSTM_GROUP, CT_ROWS, MLSTM_GROUP * A_QK), F32),
            pltpu.VMEM((A_HEADS, 1), F32),
        ],
        compiler_params=pltpu.CompilerParams(
            dimension_semantics=("arbitrary", "arbitrary"), vmem_limit_bytes=VMEM_LIMIT),
        name="mlstm",
    )(k, qt, vt, ot, bb, r, cm, head_g)


def _post_kernel(x_ref, mix_ref, mod_ref, ng_ref, wo_ref, w1_ref, w2_ref, out_ref, *, ff_chunk):
    ts = x_ref.shape[0] // ROW_SPLIT
    parts = [slice(i * ts, (i + 1) * ts) for i in range(ROW_SPLIT)]
    ys = [_dot_tn(mix_ref[:, p], wo_ref[...]) for p in parts]
    outs = []
    for p, y in zip(parts, ys):
        x1 = x_ref[p, :] + mod_ref[2:3, :] * (_rms(y) * ng_ref[1:2, :])
        h = _rms(x1) * ng_ref[2:3, :]
        h = (h * (1.0 + mod_ref[4:5, :]) + mod_ref[3:4, :]).astype(BF16)
        acc = None
        for j in range(D_FF // ff_chunk):
            a = _dot(h, w1_ref[:, j * ff_chunk:(j + 1) * ff_chunk])
            a = jnp.square(jnp.maximum(a, 0.0)).astype(BF16)
            part = _dot(a, w2_ref[j * ff_chunk:(j + 1) * ff_chunk, :])
            acc = part if acc is None else acc + part
        outs.append((p, x1, acc))
    for p, x1, acc in outs:
        out_ref[p, :] = x1 + mod_ref[5:6, :] * (_rms(acc) * ng_ref[3:4, :])


def _post(x, mix_t, mod, ng, wo, w1, w2, layer, seq, tm):
    t, d = x.shape
    per_b = seq // tm
    row = lambda i: (i, 0)
    layer_spec = lambda w: pl.BlockSpec((None,) + w.shape[1:], lambda i: (layer, 0, 0),
                                        pipeline_mode=pl.Buffered(1))
    return pl.pallas_call(
        functools.partial(_post_kernel, ff_chunk=1024),
        grid=(t // tm,),
        in_specs=[
            pl.BlockSpec((tm, d), row),
            pl.BlockSpec((None, d, tm), lambda i: (i // per_b, 0, i % per_b)),
            pl.BlockSpec((None, 8, d), lambda i: (i // per_b, 0, 0)),
            _const_spec((4, d)),
            _const_spec(wo.shape),
            layer_spec(w1),
            layer_spec(w2),
        ],
        out_specs=pl.BlockSpec((tm, d), row),
        out_shape=jax.ShapeDtypeStruct((t, d), F32),
        compiler_params=pltpu.CompilerParams(
            dimension_semantics=("arbitrary",), vmem_limit_bytes=VMEM_LIMIT),
        name="post",
    )(x, mix_t, mod, ng, wo, w1, w2)


ROPE_HALF = B_ROPE // 2
Q_HEAD = B_NOPE + B_ROPE
V_ROWS_T = B_V + 16
WA_COLS = B_KV_LORA + 2 * B_ROPE


def _l1proj_kernel(x_ref, pos_ref, inv_ref, ng_ref, mod_ref, wa_ref, gl_ref, wbk_ref, wbv_ref,
                   wqa_ref, gq_ref, wq_ref, wqs_ref, q_ref, k_ref, v_ref, *, q_scale):
    tm = x_ref.shape[0]
    xn = _rms(x_ref[...])
    ang = inv_ref[...] * pos_ref[...].astype(F32)
    cos_h = jnp.cos(ang)
    sin_h = jnp.sin(ang)
    cos_t = jnp.concatenate([cos_h, cos_h], axis=0)
    sin_t = jnp.concatenate([-sin_h, sin_h], axis=0)

    hk = xn * ng_ref[0:1, :]
    hk = (hk * (1.0 + mod_ref[1:2, :]) + mod_ref[0:1, :]).astype(BF16)
    kva = _dot(hk, wa_ref[...])
    ckv = (_rms(kva[:, 0:B_KV_LORA]) * gl_ref[...]).astype(BF16)
    kr_t = kva[:, B_KV_LORA:WA_COLS].T
    k_rope_t = kr_t[0:B_ROPE, :] * cos_t + kr_t[B_ROPE:2 * B_ROPE, :] * sin_t
    k_rope = jnp.concatenate([k_rope_t, jnp.zeros((LANES - B_ROPE, tm), F32)], axis=0).T
    k_rope = k_rope.astype(BF16)
    k_nope = _dot(ckv, wbk_ref[...])
    for h in range(B_HEADS):
        k_ref[:, h * HEAD_CAT:h * HEAD_CAT + B_NOPE] = k_nope[:, h * B_NOPE:(h + 1) * B_NOPE].astype(BF16)
        k_ref[:, h * HEAD_CAT + B_NOPE:(h + 1) * HEAD_CAT] = k_rope
    v_t = _dot_nt(wbv_ref[...], ckv)
    ones = jnp.ones((V_ROWS_T - B_V, tm), BF16)
    for h in range(B_HEADS):
        v_ref[h * V_ROWS_T:h * V_ROWS_T + B_V, :] = v_t[h * B_V:(h + 1) * B_V, :].astype(BF16)
        v_ref[h * V_ROWS_T + B_V:(h + 1) * V_ROWS_T, :] = ones

    hq = xn * ng_ref[1:2, :]
    hq = (hq * (1.0 + mod_ref[3:4, :]) + mod_ref[2:3, :]).astype(BF16)
    cq = (_rms(_dot(hq, wqa_ref[...])) * gq_ref[...]).astype(BF16)
    q_t = _dot_nt(wq_ref[...], cq)
    qs_t = _dot_nt(wqs_ref[...], cq)
    zeros = jnp.zeros((HEAD_CAT - Q_HEAD, tm), BF16)
    for h in range(B_HEADS):
        r0 = h * Q_HEAD
        q_ref[h * HEAD_CAT:h * HEAD_CAT + B_NOPE, :] = (q_t[r0:r0 + B_NOPE, :] * q_scale).astype(BF16)
        rope = q_t[r0 + B_NOPE:r0 + Q_HEAD, :] * cos_t + qs_t[h * B_ROPE:(h + 1) * B_ROPE, :] * sin_t
        q_ref[h * HEAD_CAT + B_NOPE:h * HEAD_CAT + Q_HEAD, :] = (rope * q_scale).astype(BF16)
        q_ref[h * HEAD_CAT + Q_HEAD:(h + 1) * HEAD_CAT, :] = zeros


def _l1proj(x, pos, inv, ng, mod, wa, gl, wbk, wbv, wqa, gq, wq, wqs, seq, tm, tq):
    t, d = x.shape
    bsz = t // seq
    per_b = seq // tm
    per_q = tq // tm
    row = lambda i: (i, 0)
    q_scale = float((B_NOPE + B_ROPE) ** -0.5 * math.log2(math.e))
    consts = [inv, ng, None, wa, gl, wbk, wbv, wqa, gq, wq, wqs]
    in_specs = [pl.BlockSpec((tm, d), row),
                pl.BlockSpec((None, 1, tm), lambda i: (i // per_b, 0, i % per_b))]
    for a in consts:
        if a is None:
            in_specs.append(pl.BlockSpec((None, 8, d), lambda i: (i // per_b, 0, 0)))
        else:
            in_specs.append(_const_spec(a.shape))
    return pl.pallas_call(
        functools.partial(_l1proj_kernel, q_scale=q_scale),
        grid=(t // tm,),
        in_specs=in_specs,
        out_specs=[
            pl.BlockSpec((None, B_HEADS * HEAD_CAT, tm), lambda i: (i // per_b, 0, i % per_b)),
            pl.BlockSpec((tm, B_HEADS * HEAD_CAT), row),
            pl.BlockSpec((None, None, B_HEADS * V_ROWS_T, tm),
                         lambda i: (i // per_b, (i % per_b) // per_q, 0, i % per_q)),
        ],
        out_shape=[
            jax.ShapeDtypeStruct((bsz, B_HEADS * HEAD_CAT, seq), BF16),
            jax.ShapeDtypeStruct((t, B_HEADS * HEAD_CAT), BF16),
            jax.ShapeDtypeStruct((bsz, seq // tq, B_HEADS * V_ROWS_T, tq), BF16),
        ],
        compiler_params=pltpu.CompilerParams(
            dimension_semantics=("arbitrary",), vmem_limit_bytes=VMEM_LIMIT),
        name="l1proj",
    )(x, pos, inv, ng, mod, wa, gl, wbk, wbv, wqa, gq, wq, wqs)


def _attn_kernel(qt_ref, k_ref, vt_ref, o_ref, sa_ref, sb_ref, *, tq):
    qi = pl.program_id(2)

    def scores(j):
        rows = pl.ds(pl.multiple_of(j * tq, tq), tq)
        return _dot(k_ref[rows, :], qt_ref[...])

    def soft_pv(j, s, m, acc, masked=False):
        if masked:
            keys = lax.broadcasted_iota(jnp.int32, (tq, tq), 0)
            qrys = lax.broadcasted_iota(jnp.int32, (tq, tq), 1)
            s = jnp.where(keys <= qrys, s, -jnp.inf)
        m_new = jnp.maximum(m, jnp.max(s, axis=0, keepdims=True))
        p = jnp.exp2(s - m_new).astype(BF16)
        acc = jnp.exp2(m - m_new) * acc + _dot(vt_ref[j], p)
        return m_new, acc

    sa_ref[...] = scores(0)

    def pair(jj, carry):
        m, acc = carry
        j0 = 2 * jj
        sb_ref[...] = scores(j0 + 1)
        m, acc = soft_pv(j0, sa_ref[...], m, acc)
        sa_ref[...] = scores(j0 + 2)
        return soft_pv(j0 + 1, sb_ref[...], m, acc)

    init = (jnp.full((1, tq), -jnp.inf, F32), jnp.zeros((V_ROWS_T, tq), F32))
    m, acc = lax.fori_loop(0, qi // 2, pair, init)

    def odd_tail(m, acc):
        sb_ref[...] = scores(qi)
        m, acc = soft_pv(qi - 1, sa_ref[...], m, acc)
        return soft_pv(qi, sb_ref[...], m, acc, masked=True)

    def even_tail(m, acc):
        return soft_pv(qi, sa_ref[...], m, acc, masked=True)

    _, acc = lax.cond(qi % 2 == 1, odd_tail, even_tail, m, acc)
    o_ref[...] = (acc[0:B_V, :] * (1.0 / acc[B_V:B_V + 1, :])).astype(BF16)


def _attn(q_t, k, v_t, tq):
    b, s, _ = k.shape
    return pl.pallas_call(
        functools.partial(_attn_kernel, tq=tq),
        grid=(b, B_HEADS, s // tq),
        in_specs=[
            pl.BlockSpec((None, HEAD_CAT, tq), lambda bi, h, i: (bi, h, i)),
            pl.BlockSpec((None, s, HEAD_CAT), lambda bi, h, i: (bi, 0, h)),
            pl.BlockSpec((None, s // tq, V_ROWS_T, tq), lambda bi, h, i: (bi, 0, h, 0)),
        ],
        out_specs=pl.BlockSpec((None, B_V, tq), lambda bi, h, i: (bi, h, i)),
        out_shape=jax.ShapeDtypeStruct((b, B_HEADS * B_V, s), BF16),
        scratch_shapes=[pltpu.VMEM((tq, tq), F32), pltpu.VMEM((tq, tq), F32)],
        compiler_params=pltpu.CompilerParams(
            dimension_semantics=("arbitrary", "arbitrary", "arbitrary"),
            vmem_limit_bytes=VMEM_LIMIT),
        name="attn",
    )(q_t, k, v_t)


def _swap_halves(w):
    half = w.shape[-1] // 2
    return jnp.concatenate([w[..., half:], w[..., :half]], axis=-1)


def kernel(x, c, positions, ada_w, ada_b, norm_g, a_w_in, a_gate_b, a_head_g, a_w_out,
           kv_ada_w, kv_ada_b, kv_norm_g, kv_w_a, kv_latent_g, kv_w_b, b_w_q_a, b_q_latent_g,
           b_w_q_b, b_w_out, mlp_w1, mlp_w2):
    bsz, seq, d = x.shape
    t = bsz * seq
    x2d = x.reshape(t, d)

    c8 = jnp.pad(c, ((0, 8 - bsz), (0, 0)))
    ada = _adaln(c8, ada_w, ada_b[:, None, :], tn=1536)[:, :bsz]
    kv_ada = _adaln(c8, kv_ada_w[None], kv_ada_b[None, None, :], tn=1024)[0, :bsz]

    def mod_rows(vecs):
        rows = [v.reshape(bsz, 1, d) for v in vecs]
        rows += [jnp.zeros((bsz, 1, d), F32)] * (8 - len(rows))
        return jnp.concatenate(rows, axis=1)

    ada0 = [ada[0][:, i * d:(i + 1) * d] for i in range(6)]
    ada1 = [ada[1][:, i * d:(i + 1) * d] for i in range(6)]
    kv_shift, kv_scale = kv_ada[:, :d], kv_ada[:, d:]

    w_in = a_w_in[0]
    c0 = 2 * A_QK_W + A_V_W
    wk = w_in[:, A_QK_W:2 * A_QK_W].astype(BF16)
    wt = jnp.concatenate([
        w_in[:, :A_QK_W],
        w_in[:, 2 * A_QK_W:c0],
        w_in[:, c0 + 2 * A_HEADS:],
        w_in[:, c0:c0 + 2 * A_HEADS],
    ], axis=1).T.astype(BF16)
    k, qt, vt, ot, gt = _inproj(x2d, norm_g[0, 0][None], mod_rows(ada0[:2]), wk, wt, seq, tm=ROW_TM)
    gate_b = jnp.broadcast_to(a_gate_b[0].reshape(A_GATE_ROWS, 1), (A_GATE_ROWS, LANES))
    head_g = jnp.broadcast_to(a_head_g[0].reshape(A_V_W, 1), (A_V_W, MLSTM_CHUNK))
    bb, r, cm = _gates(gt, gate_b, MLSTM_CHUNK)
    mix0 = _mlstm(k, qt, vt, ot, bb, r, cm, head_g, MLSTM_CHUNK, tb=MLSTM_TB)
    w1_all, w2_all = mlp_w1.astype(BF16), mlp_w2.astype(BF16)
    x2d = _post(x2d, mix0, mod_rows(ada0), norm_g[0], a_w_out[0].astype(BF16), w1_all, w2_all, 0,
                seq, tm=POST_TM)

    tq = ATTN_TQ
    rope = kv_w_a[:, B_KV_LORA:]
    wa_cat = jnp.concatenate([kv_w_a[:, :B_KV_LORA], rope, _swap_halves(rope)], axis=1).astype(BF16)
    wb = kv_w_b.reshape(B_KV_LORA, B_HEADS, B_NOPE + B_V)
    wbk = wb[:, :, :B_NOPE].reshape(B_KV_LORA, -1).astype(BF16)
    wbv = wb[:, :, B_NOPE:].reshape(B_KV_LORA, -1).T.astype(BF16)
    wq = b_w_q_b[0].T.astype(BF16)
    wq_rope = b_w_q_b[0].reshape(B_Q_LORA, B_HEADS, Q_HEAD)[:, :, B_NOPE:]
    wqs = _swap_halves(wq_rope).reshape(B_Q_LORA, -1).T.astype(BF16)
    inv = ROPE_THETA ** (-jnp.arange(ROPE_HALF, dtype=F32) / ROPE_HALF)
    inv_rep = jnp.broadcast_to(inv[:, None], (ROPE_HALF, ROW_TM))
    ng1 = jnp.concatenate([kv_norm_g[None], norm_g[1, 0][None]], axis=0)
    mod1 = mod_rows([kv_shift, kv_scale, ada1[0], ada1[1]])
    q_t, k_cat, v_t = _l1proj(
        x2d, positions.reshape(bsz, 1, seq), inv_rep, ng1, mod1, wa_cat, kv_latent_g[None], wbk, wbv,
        b_w_q_a[0].astype(BF16), b_q_latent_g[0][None], wq, wqs, seq, tm=ROW_TM, tq=tq)
    o = _attn(q_t, k_cat.reshape(bsz, seq, -1), v_t, tq=tq)
    x2d = _post(x2d, o, mod_rows(ada1), norm_g[1], b_w_out[0].astype(BF16), w1_all, w2_all, 1,
                seq, tm=POST_TM)
    return x2d.reshape(bsz, seq, d)
```

```python
import functools
import math

import jax
import jax.numpy as jnp
from jax import lax
from jax.experimental import pallas as pl
from jax.experimental.pallas import tpu as pltpu

F32 = jnp.float32
BF16 = jnp.bfloat16

D_MODEL = 1024
D_FF = 4 * D_MODEL
EPS = 1e-6
LOG2E = math.log2(math.e)

A_HEADS = 8
A_QK = 64
A_V = 128
A_QK_W = A_HEADS * A_QK
A_V_W = A_HEADS * A_V
MLSTM_CHUNK = 256
MLSTM_GROUP = 4
MLSTM_TB = 1024

B_HEADS = 8
B_Q_LORA = 384
B_KV_LORA = 256
B_NOPE = 128
B_ROPE = 64
B_V = 128
ROPE_THETA = 10000.0
HEAD_CAT = 256
ATTN_TK = 1024
ATTN_TQ = 2 * ATTN_TK
ROW_TM = 1024
POST_TM = 512
ROW_SPLIT = 2

LANES = 128
VMEM_LIMIT = 56 * 1024 * 1024


def _dot(a, b):
    return jnp.dot(a, b, preferred_element_type=F32)


def _dot_nt(a, b):
    return lax.dot_general(a, b, (((1,), (1,)), ((), ())), preferred_element_type=F32)


def _dot_tn(a, b):
    return lax.dot_general(a, b, (((0,), (0,)), ((), ())), preferred_element_type=F32)


def _rms(x):
    return x * lax.rsqrt(jnp.mean(x * x, axis=-1, keepdims=True) + EPS)


def _const_spec(shape):
    nd = len(shape)
    return pl.BlockSpec(shape, lambda *_: (0,) * nd, pipeline_mode=pl.Buffered(1))


def _adaln_kernel(c_ref, w_ref, b_ref, o_ref):
    c = c_ref[...]
    cond = c * jax.nn.sigmoid(c)
    o_ref[...] = _dot(cond.astype(BF16), w_ref[...].astype(BF16)) + b_ref[...]


def _adaln(c8, w, b, tn):
    nl, d, n = w.shape
    return pl.pallas_call(
        _adaln_kernel,
        grid=(nl, n // tn),
        in_specs=[
            pl.BlockSpec((8, d), lambda l, j: (0, 0)),
            pl.BlockSpec((None, d, tn), lambda l, j: (l, 0, j)),
            pl.BlockSpec((None, 1, tn), lambda l, j: (l, 0, j)),
        ],
        out_specs=pl.BlockSpec((None, 8, tn), lambda l, j: (l, 0, j)),
        out_shape=jax.ShapeDtypeStruct((nl, 8, n), F32),
        compiler_params=pltpu.CompilerParams(
            dimension_semantics=("arbitrary", "arbitrary"), vmem_limit_bytes=VMEM_LIMIT),
        name="adaln",
    )(c8, w, b)


A_GATE_ROWS = 2 * A_HEADS
WT_Q0, WT_V0, WT_O0, WT_G0 = 0, A_QK_W, A_QK_W + A_V_W, A_QK_W + 2 * A_V_W
WT_ROWS = WT_G0 + A_GATE_ROWS


def _inproj_kernel(x_ref, g_ref, mod_ref, wk_ref, wt_ref, k_ref, qt_ref, vt_ref, ot_ref, gt_ref):
    ts = x_ref.shape[0] // ROW_SPLIT
    for i in range(ROW_SPLIT):
        p = slice(i * ts, (i + 1) * ts)
        h = _rms(x_ref[p, :]) * g_ref[...]
        h = (h * (1.0 + mod_ref[1:2, :]) + mod_ref[0:1, :]).astype(BF16)
        k_ref[p, :] = (_dot(h, wk_ref[...]) * (A_QK ** -0.5)).astype(BF16)
        feat = _dot_nt(wt_ref[...], h)
        qt_ref[:, p] = feat[WT_Q0:WT_V0, :].astype(BF16)
        vt_ref[:, p] = feat[WT_V0:WT_O0, :].astype(BF16)
        ot_ref[:, p] = feat[WT_O0:WT_G0, :].astype(BF16)
        gt_ref[:, p] = feat[WT_G0:WT_ROWS, :]


def _inproj(x, g, mod, wk, wt, seq, tm):
    t, d = x.shape
    bsz = t // seq
    per_b = seq // tm
    row = lambda i: (i, 0)
    col = lambda i: (i // per_b, 0, i % per_b)
    return pl.pallas_call(
        _inproj_kernel,
        grid=(t // tm,),
        in_specs=[
            pl.BlockSpec((tm, d), row),
            _const_spec((1, d)),
            pl.BlockSpec((None, 8, d), lambda i: (i // per_b, 0, 0)),
            _const_spec(wk.shape),
            _const_spec(wt.shape),
        ],
        out_specs=[
            pl.BlockSpec((tm, A_QK_W), row),
            pl.BlockSpec((None, A_QK_W, tm), col),
            pl.BlockSpec((None, A_V_W, tm), col),
            pl.BlockSpec((None, A_V_W, tm), col),
            pl.BlockSpec((None, A_GATE_ROWS, tm), col),
        ],
        out_shape=[
            jax.ShapeDtypeStruct((t, A_QK_W), BF16),
            jax.ShapeDtypeStruct((bsz, A_QK_W, seq), BF16),
            jax.ShapeDtypeStruct((bsz, A_V_W, seq), BF16),
            jax.ShapeDtypeStruct((bsz, A_V_W, seq), BF16),
            jax.ShapeDtypeStruct((bsz, A_GATE_ROWS, seq), F32),
        ],
        compiler_params=pltpu.CompilerParams(
            dimension_semantics=("arbitrary",), vmem_limit_bytes=VMEM_LIMIT),
        name="inproj",
    )(x, g, mod, wk, wt)


ONES_ROWS = 16
CT_ROWS = A_V + ONES_ROWS


def _scan_lanes(x, op, fill, seg):
    pos = lax.broadcasted_iota(jnp.int32, x.shape, 1) % seg
    sh = 1
    while sh < seg:
        x = op(x, jnp.where(pos >= sh, pltpu.roll(x, sh, axis=1), fill))
        sh *= 2
    return x


def _gates_kernel(gt_ref, gb_ref, bb_ref, r_ref, cm_ref, *, chunk):
    H = A_HEADS
    ig = gt_ref[0:H, :] + gb_ref[0:H, 0:1]
    fg = gt_ref[H:2 * H, :] + gb_ref[H:2 * H, 0:1]
    bb = _scan_lanes(jax.nn.log_sigmoid(fg) * LOG2E, jnp.add, 0.0, chunk)
    r = ig * LOG2E - bb
    bb_ref[...] = bb
    r_ref[...] = r
    cm_ref[...] = _scan_lanes(r, jnp.maximum, -jnp.inf, chunk)


def _gates(gt, gate_b, chunk):
    bsz, _, seq = gt.shape
    spec = pl.BlockSpec((None, A_HEADS, seq), lambda b: (b, 0, 0))
    shape = jax.ShapeDtypeStruct((bsz, A_HEADS, seq), F32)
    return pl.pallas_call(
        functools.partial(_gates_kernel, chunk=chunk),
        grid=(bsz,),
        in_specs=[pl.BlockSpec((None, A_GATE_ROWS, seq), lambda b: (b, 0, 0)),
                  pl.BlockSpec(gate_b.shape, lambda b: (0, 0))],
        out_specs=[spec, spec, spec],
        out_shape=[shape, shape, shape],
        compiler_params=pltpu.CompilerParams(
            dimension_semantics=("arbitrary",), vmem_limit_bytes=VMEM_LIMIT),
        name="gates",
    )(gt, gate_b)


def _mlstm_kernel(k_ref, qt_ref, vt_ref, ot_ref, bb_ref, r_ref, cm_ref, hg_ref, out_ref,
                  ct_ref, m_ref, *, chunk, n_chunks):
    L = chunk
    H = A_HEADS

    @pl.when(pl.program_id(1) == 0)
    def _():
        ct_ref[...] = jnp.zeros_like(ct_ref)
        m_ref[...] = jnp.zeros_like(m_ref)

    src = lax.broadcasted_iota(jnp.int32, (L, L), 0)
    tgt = lax.broadcasted_iota(jnp.int32, (L, L), 1)
    causal = src <= tgt
    ones_rows = jnp.ones((ONES_ROWS, L), BF16)

    for c in range(n_chunks):
        cs = slice(c * L, (c + 1) * L)
        bb = bb_ref[:, cs]
        r = r_ref[:, cs]
        m_prev = m_ref[...]
        mm = jnp.maximum(m_prev, cm_ref[:, cs])
        e_inv = jnp.exp2(-(bb + mm))
        w_inter = jnp.exp2(m_prev - mm)
        mm_last = mm[:, L - 1:L]
        w_upd = jnp.exp2(r - mm_last)
        decay = jnp.exp2(m_prev - mm_last)
        m_ref[...] = bb[:, L - 1:L] + mm_last
        r_cols = jnp.concatenate([r, jnp.zeros((L - H, L), F32)], axis=0).T

        G = MLSTM_GROUP
        GW = G * A_QK
        lane_grp = lax.broadcasted_iota(jnp.int32, (L, GW), 1) // A_QK
        row_grp = lax.broadcasted_iota(jnp.int32, (1, GW), 1) // A_QK
        zq = jnp.zeros((A_QK, L), BF16)
        for g in range(H // G):
            hb = g * G
            kg = k_ref[cs, g * GW:(g + 1) * GW]
            q_bd = jnp.concatenate([
                jnp.concatenate([qt_ref[(hb + i) * A_QK:(hb + i + 1) * A_QK, cs] if i == j else zq
                                 for j in range(G)], axis=1) for i in range(G)], axis=0)
            ctg = ct_ref[g]
            st_g = _dot(kg, q_bd)
            inter_g = _dot(ctg.astype(BF16), q_bd)
            pts, vexts = [], []
            for j in range(G):
                h = hb + j
                arg = jnp.where(causal, r_cols[:, h:h + 1] - mm[h:h + 1, :], -jnp.inf)
                pts.append((st_g[:, j * L:(j + 1) * L] * jnp.exp2(arg)).astype(BF16))
                vexts.append(jnp.concatenate([vt_ref[h * A_V:(h + 1) * A_V, cs], ones_rows], axis=0))
            wvs = []
            for j0 in range(0, G, 2):
                for j in (j0, j0 + 1):
                    h = hb + j
                    tot = (w_inter[h:h + 1, :] * inter_g[:, j * L:(j + 1) * L]
                           + _dot(vexts[j], pts[j]))
                    den = tot[A_V:A_V + 1, :]
                    hh = tot[0:A_V, :] * (1.0 / jnp.maximum(jnp.abs(den), e_inv[h:h + 1, :]))
                    ms = jnp.mean(hh * hh, axis=0, keepdims=True)
                    hn = hh * lax.rsqrt(ms + EPS) * hg_ref[h * A_V:(h + 1) * A_V, :]
                    og = ot_ref[h * A_V:(h + 1) * A_V, cs].astype(F32)
                    out_ref[h * A_V:(h + 1) * A_V, cs] = (hn * jax.nn.sigmoid(og)).astype(BF16)
                    wvs.append((vexts[j].astype(F32) * w_upd[h:h + 1, :]).astype(BF16))
            zk = jnp.zeros((L, GW), BF16)
            k_bd = jnp.concatenate([jnp.where(lane_grp == i, kg, zk) for i in range(G)], axis=0)
            decay_g = decay[hb:hb + 1, :]
            for i in range(1, G):
                decay_g = jnp.where(row_grp >= i, decay[hb + i:hb + i + 1, :], decay_g)
            ct_ref[g] = decay_g * ctg + _dot(jnp.concatenate(wvs, axis=1), k_bd)


def _mlstm(k, qt, vt, ot, bb, r, cm, head_g, chunk, tb):
    bsz, _, seq = qt.shape
    per_b = seq // tb
    col = lambda b, i: (b, 0, i)
    gate_spec = pl.BlockSpec((None, A_HEADS, tb), col)
    return pl.pallas_call(
        functools.partial(_mlstm_kernel, chunk=chunk, n_chunks=tb // chunk),
        grid=(bsz, per_b),
        in_specs=[
            pl.BlockSpec((tb, A_QK_W), lambda b, i: (b * per_b + i, 0)),
            pl.BlockSpec((None, A_QK_W, tb), col),
            pl.BlockSpec((None, A_V_W, tb), col),
            pl.BlockSpec((None, A_V_W, tb), col),
            gate_spec, gate_spec, gate_spec,
            pl.BlockSpec(head_g.shape, lambda b, i: (0, 0)),
        ],
        out_specs=pl.BlockSpec((None, A_V_W, tb), col),
        out_shape=jax.ShapeDtypeStruct((bsz, A_V_W, seq), BF16),
        scratch_shapes=[
            pltpu.VMEM((A_HEADS // MLSTM_GROUP, CT_ROWS, MLSTM_GROUP * A_QK), F32),
            pltpu.VMEM((A_HEADS, 1), F32),
        ],
        compiler_params=pltpu.CompilerParams(
            dimension_semantics=("arbitrary", "arbitrary"), vmem_limit_bytes=VMEM_LIMIT),
        name="mlstm",
    )(k, qt, vt, ot, bb, r, cm, head_g)


def _post_kernel(x_ref, mix_ref, mod_ref, ng_ref, wo_ref, w1_ref, w2_ref, out_ref, *, ff_chunk):
    ts = x_ref.shape[0] // ROW_SPLIT
    parts = [slice(i * ts, (i + 1) * ts) for i in range(ROW_SPLIT)]
    ys = [_dot_tn(mix_ref[:, p], wo_ref[...]) for p in parts]
    outs = []
    for p, y in zip(parts, ys):
        x1 = x_ref[p, :] + mod_ref[2:3, :] * (_rms(y) * ng_ref[1:2, :])
        h = _rms(x1) * ng_ref[2:3, :]
        h = (h * (1.0 + mod_ref[4:5, :]) + mod_ref[3:4, :]).astype(BF16)
        acc = None
        for j in range(D_FF // ff_chunk):
            a = _dot(h, w1_ref[:, j * ff_chunk:(j + 1) * ff_chunk])
            a = jnp.square(jnp.maximum(a, 0.0)).astype(BF16)
            part = _dot(a, w2_ref[j * ff_chunk:(j + 1) * ff_chunk, :])
            acc = part if acc is None else acc + part
        outs.append((p, x1, acc))
    for p, x1, acc in outs:
        out_ref[p, :] = x1 + mod_ref[5:6, :] * (_rms(acc) * ng_ref[3:4, :])


def _post(x, mix_t, mod, ng, wo, w1, w2, layer, seq, tm):
    t, d = x.shape
    per_b = seq // tm
    row = lambda i: (i, 0)
    layer_spec = lambda w: pl.BlockSpec((None,) + w.shape[1:], lambda i: (layer, 0, 0),
                                        pipeline_mode=pl.Buffered(1))
    return pl.pallas_call(
        functools.partial(_post_kernel, ff_chunk=1024),
        grid=(t // tm,),
        in_specs=[
            pl.BlockSpec((tm, d), row),
            pl.BlockSpec((None, d, tm), lambda i: (i // per_b, 0, i % per_b)),
            pl.BlockSpec((None, 8, d), lambda i: (i // per_b, 0, 0)),
            _const_spec((4, d)),
            _const_spec(wo.shape),
            layer_spec(w1),
            layer_spec(w2),
        ],
        out_specs=pl.BlockSpec((tm, d), row),
        out_shape=jax.ShapeDtypeStruct((t, d), F32),
        compiler_params=pltpu.CompilerParams(
            dimension_semantics=("arbitrary",), vmem_limit_bytes=VMEM_LIMIT),
        name="post",
    )(x, mix_t, mod, ng, wo, w1, w2)


ROPE_HALF = B_ROPE // 2
Q_HEAD = B_NOPE + B_ROPE
V_ROWS_T = B_V + 16
WA_COLS = B_KV_LORA + 2 * B_ROPE


def _l1proj_kernel(x_ref, pos_ref, inv_ref, ng_ref, mod_ref, wa_ref, gl_ref, wbk_ref, wbv_ref,
                   wqa_ref, gq_ref, wq_ref, wqs_ref, q_ref, k_ref, v_ref, *, q_scale):
    tm = x_ref.shape[0]
    xn = _rms(x_ref[...])
    ang = inv_ref[...] * pos_ref[...].astype(F32)
    cos_h = jnp.cos(ang)
    sin_h = jnp.sin(ang)
    cos_t = jnp.concatenate([cos_h, cos_h], axis=0)
    sin_t = jnp.concatenate([-sin_h, sin_h], axis=0)

    hk = xn * ng_ref[0:1, :]
    hk = (hk * (1.0 + mod_ref[1:2, :]) + mod_ref[0:1, :]).astype(BF16)
    kva = _dot(hk, wa_ref[...])
    ckv = (_rms(kva[:, 0:B_KV_LORA]) * gl_ref[...]).astype(BF16)
    kr_t = kva[:, B_KV_LORA:WA_COLS].T
    k_rope_t = kr_t[0:B_ROPE, :] * cos_t + kr_t[B_ROPE:2 * B_ROPE, :] * sin_t
    k_rope = jnp.concatenate([k_rope_t, jnp.zeros((LANES - B_ROPE, tm), F32)], axis=0).T
    k_rope = k_rope.astype(BF16)
    k_nope = _dot(ckv, wbk_ref[...])
    for h in range(B_HEADS):
        k_ref[:, h * HEAD_CAT:h * HEAD_CAT + B_NOPE] = k_nope[:, h * B_NOPE:(h + 1) * B_NOPE].astype(BF16)
        k_ref[:, h * HEAD_CAT + B_NOPE:(h + 1) * HEAD_CAT] = k_rope
    v_t = _dot_nt(wbv_ref[...], ckv)
    ones = jnp.ones((V_ROWS_T - B_V, tm), BF16)
    for h in range(B_HEADS):
        v_ref[h * V_ROWS_T:h * V_ROWS_T + B_V, :] = v_t[h * B_V:(h + 1) * B_V, :].astype(BF16)
        v_ref[h * V_ROWS_T + B_V:(h + 1) * V_ROWS_T, :] = ones

    hq = xn * ng_ref[1:2, :]
    hq = (hq * (1.0 + mod_ref[3:4, :]) + mod_ref[2:3, :]).astype(BF16)
    cq = (_rms(_dot(hq, wqa_ref[...])) * gq_ref[...]).astype(BF16)
    q_t = _dot_nt(wq_ref[...], cq)
    qs_t = _dot_nt(wqs_ref[...], cq)
    zeros = jnp.zeros((HEAD_CAT - Q_HEAD, tm), BF16)
    for h in range(B_HEADS):
        r0 = h * Q_HEAD
        q_ref[h * HEAD_CAT:h * HEAD_CAT + B_NOPE, :] = (q_t[r0:r0 + B_NOPE, :] * q_scale).astype(BF16)
        rope = q_t[r0 + B_NOPE:r0 + Q_HEAD, :] * cos_t + qs_t[h * B_ROPE:(h + 1) * B_ROPE, :] * sin_t
        q_ref[h * HEAD_CAT + B_NOPE:h * HEAD_CAT + Q_HEAD, :] = (rope * q_scale).astype(BF16)
        q_ref[h * HEAD_CAT + Q_HEAD:(h + 1) * HEAD_CAT, :] = zeros


def _l1proj(x, pos, inv, ng, mod, wa, gl, wbk, wbv, wqa, gq, wq, wqs, seq, tm, tq):
    t, d = x.shape
    bsz = t // seq
    per_b = seq // tm
    per_q = tq // tm
    row = lambda i: (i, 0)
    q_scale = float((B_NOPE + B_ROPE) ** -0.5 * math.log2(math.e))
    consts = [inv, ng, None, wa, gl, wbk, wbv, wqa, gq, wq, wqs]
    in_specs = [pl.BlockSpec((tm, d), row),
                pl.BlockSpec((None, 1, tm), lambda i: (i // per_b, 0, i % per_b))]
    for a in consts:
        if a is None:
            in_specs.append(pl.BlockSpec((None, 8, d), lambda i: (i // per_b, 0, 0)))
        else:
            in_specs.append(_const_spec(a.shape))
    return pl.pallas_call(
        functools.partial(_l1proj_kernel, q_scale=q_scale),
        grid=(t // tm,),
        in_specs=in_specs,
        out_specs=[
            pl.BlockSpec((None, B_HEADS * HEAD_CAT, tm), lambda i: (i // per_b, 0, i % per_b)),
            pl.BlockSpec((tm, B_HEADS * HEAD_CAT), row),
            pl.BlockSpec((None, None, B_HEADS * V_ROWS_T, tm),
                         lambda i: (i // per_b, (i % per_b) // per_q, 0, i % per_q)),
        ],
        out_shape=[
            jax.ShapeDtypeStruct((bsz, B_HEADS * HEAD_CAT, seq), BF16),
            jax.ShapeDtypeStruct((t, B_HEADS * HEAD_CAT), BF16),
            jax.ShapeDtypeStruct((bsz, seq // tq, B_HEADS * V_ROWS_T, tq), BF16),
        ],
        compiler_params=pltpu.CompilerParams(
            dimension_semantics=("arbitrary",), vmem_limit_bytes=VMEM_LIMIT),
        name="l1proj",
    )(x, pos, inv, ng, mod, wa, gl, wbk, wbv, wqa, gq, wq, wqs)


def _attn_kernel(qt_ref, k_ref, vt_ref, o_ref, sa_ref, sb_ref, sh_ref, *, tq, tk):
    qi = pl.program_id(2)

    def k_rows(j):
        return k_ref[pl.ds(pl.multiple_of(j * tk, tk), tk), :]

    def scores(j):
        return _dot(k_rows(j), qt_ref[...])

    def soft_pv(v_t, s, m, acc):
        m_new = jnp.maximum(m, jnp.max(s, axis=0, keepdims=True))
        p = jnp.exp2(s - m_new).astype(BF16)
        return m_new, jnp.exp2(m - m_new) * acc + _dot(v_t, p)

    sa_ref[...] = scores(0)

    def pair(jj, carry):
        m, acc = carry
        j0 = 2 * jj
        sb_ref[...] = scores(j0 + 1)
        m, acc = soft_pv(vt_ref[j0], sa_ref[...], m, acc)
        sa_ref[...] = scores(j0 + 2)
        return soft_pv(vt_ref[j0 + 1], sb_ref[...], m, acc)

    init = (jnp.full((1, tq), -jnp.inf, F32), jnp.zeros((V_ROWS_T, tq), F32))
    m, acc = lax.fori_loop(0, qi, pair, init)

    d0 = 2 * qi
    sh_ref[...] = _dot(k_rows(d0 + 1), qt_ref[:, tk:tq])
    keys = lax.broadcasted_iota(jnp.int32, (tk, tk), 0)
    qrys = lax.broadcasted_iota(jnp.int32, (tk, tk), 1)
    lower = keys <= qrys
    s_a = sa_ref[...]
    s_a = jnp.concatenate([jnp.where(lower, s_a[:, 0:tk], -jnp.inf), s_a[:, tk:tq]], axis=1)
    m, acc = soft_pv(vt_ref[d0], s_a, m, acc)
    s_b = jnp.where(lower, sh_ref[...], -jnp.inf)
    _, acc_r = soft_pv(vt_ref[d0 + 1], s_b, m[:, tk:tq], acc[:, tk:tq])
    acc = jnp.concatenate([acc[:, 0:tk], acc_r], axis=1)
    o_ref[...] = (acc[0:B_V, :] * (1.0 / acc[B_V:B_V + 1, :])).astype(BF16)


def _attn(q_t, k, v_t, tq, tk):
    b, s, _ = k.shape
    assert tq == 2 * tk
    return pl.pallas_call(
        functools.partial(_attn_kernel, tq=tq, tk=tk),
        grid=(b, B_HEADS, s // tq),
        in_specs=[
            pl.BlockSpec((None, HEAD_CAT, tq), lambda bi, h, i: (bi, h, i)),
            pl.BlockSpec((None, s, HEAD_CAT), lambda bi, h, i: (bi, 0, h)),
            pl.BlockSpec((None, s // tk, V_ROWS_T, tk), lambda bi, h, i: (bi, 0, h, 0)),
        ],
        out_specs=pl.BlockSpec((None, B_V, tq), lambda bi, h, i: (bi, h, i)),
        out_shape=jax.ShapeDtypeStruct((b, B_HEADS * B_V, s), BF16),
        scratch_shapes=[
            pltpu.VMEM((tk, tq), F32), pltpu.VMEM((tk, tq), F32),
            pltpu.VMEM((tk, tk), F32),
        ],
        compiler_params=pltpu.CompilerParams(
            dimension_semantics=("arbitrary", "arbitrary", "arbitrary"),
            vmem_limit_bytes=VMEM_LIMIT),
        name="attn",
    )(q_t, k, v_t)


def _swap_halves(w):
    half = w.shape[-1] // 2
    return jnp.concatenate([w[..., half:], w[..., :half]], axis=-1)


def kernel(x, c, positions, ada_w, ada_b, norm_g, a_w_in, a_gate_b, a_head_g, a_w_out,
           kv_ada_w, kv_ada_b, kv_norm_g, kv_w_a, kv_latent_g, kv_w_b, b_w_q_a, b_q_latent_g,
           b_w_q_b, b_w_out, mlp_w1, mlp_w2):
    bsz, seq, d = x.shape
    t = bsz * seq
    x2d = x.reshape(t, d)

    c8 = jnp.pad(c, ((0, 8 - bsz), (0, 0)))
    ada = _adaln(c8, ada_w, ada_b[:, None, :], tn=1536)[:, :bsz]
    kv_ada = _adaln(c8, kv_ada_w[None], kv_ada_b[None, None, :], tn=1024)[0, :bsz]

    def mod_rows(vecs):
        rows = [v.reshape(bsz, 1, d) for v in vecs]
        rows += [jnp.zeros((bsz, 1, d), F32)] * (8 - len(rows))
        return jnp.concatenate(rows, axis=1)

    ada0 = [ada[0][:, i * d:(i + 1) * d] for i in range(6)]
    ada1 = [ada[1][:, i * d:(i + 1) * d] for i in range(6)]
    kv_shift, kv_scale = kv_ada[:, :d], kv_ada[:, d:]

    w_in = a_w_in[0]
    c0 = 2 * A_QK_W + A_V_W
    wk = w_in[:, A_QK_W:2 * A_QK_W].astype(BF16)
    wt = jnp.concatenate([
        w_in[:, :A_QK_W],
        w_in[:, 2 * A_QK_W:c0],
        w_in[:, c0 + 2 * A_HEADS:],
        w_in[:, c0:c0 + 2 * A_HEADS],
    ], axis=1).T.astype(BF16)
    k, qt, vt, ot, gt = _inproj(x2d, norm_g[0, 0][None], mod_rows(ada0[:2]), wk, wt, seq, tm=ROW_TM)
    gate_b = jnp.broadcast_to(a_gate_b[0].reshape(A_GATE_ROWS, 1), (A_GATE_ROWS, LANES))
    head_g = jnp.broadcast_to(a_head_g[0].reshape(A_V_W, 1), (A_V_W, MLSTM_CHUNK))
    bb, r, cm = _gates(gt, gate_b, MLSTM_CHUNK)
    mix0 = _mlstm(k, qt, vt, ot, bb, r, cm, head_g, MLSTM_CHUNK, tb=MLSTM_TB)
    w1_all, w2_all = mlp_w1.astype(BF16), mlp_w2.astype(BF16)
    x2d = _post(x2d, mix0, mod_rows(ada0), norm_g[0], a_w_out[0].astype(BF16), w1_all, w2_all, 0,
                seq, tm=POST_TM)

    tq, tk = ATTN_TQ, ATTN_TK
    rope = kv_w_a[:, B_KV_LORA:]
    wa_cat = jnp.concatenate([kv_w_a[:, :B_KV_LORA], rope, _swap_halves(rope)], axis=1).astype(BF16)
    wb = kv_w_b.reshape(B_KV_LORA, B_HEADS, B_NOPE + B_V)
    wbk = wb[:, :, :B_NOPE].reshape(B_KV_LORA, -1).astype(BF16)
    wbv = wb[:, :, B_NOPE:].reshape(B_KV_LORA, -1).T.astype(BF16)
    wq = b_w_q_b[0].T.astype(BF16)
    wq_rope = b_w_q_b[0].reshape(B_Q_LORA, B_HEADS, Q_HEAD)[:, :, B_NOPE:]
    wqs = _swap_halves(wq_rope).reshape(B_Q_LORA, -1).T.astype(BF16)
    inv = ROPE_THETA ** (-jnp.arange(ROPE_HALF, dtype=F32) / ROPE_HALF)
    inv_rep = jnp.broadcast_to(inv[:, None], (ROPE_HALF, ROW_TM))
    ng1 = jnp.concatenate([kv_norm_g[None], norm_g[1, 0][None]], axis=0)
    mod1 = mod_rows([kv_shift, kv_scale, ada1[0], ada1[1]])
    q_t, k_cat, v_t = _l1proj(
        x2d, positions.reshape(bsz, 1, seq), inv_rep, ng1, mod1, wa_cat, kv_latent_g[None], wbk, wbv,
        b_w_q_a[0].astype(BF16), b_q_latent_g[0][None], wq, wqs, seq, tm=ROW_TM, tq=tk)
    o = _attn(q_t, k_cat.reshape(bsz, seq, -1), v_t, tq=tq, tk=tk)
    x2d = _post(x2d, o, mod_rows(ada1), norm_g[1], b_w_out[0].astype(BF16), w1_all, w2_all, 1,
                seq, tm=POST_TM)
    return x2d.reshape(bsz, seq, d)
```

```python
import functools
import math

import jax
import jax.numpy as jnp
from jax import lax
from jax.experimental import pallas as pl
from jax.experimental.pallas import tpu as pltpu

F32 = jnp.float32
BF16 = jnp.bfloat16

D_MODEL = 1024
D_FF = 4 * D_MODEL
EPS = 1e-6
LOG2E = math.log2(math.e)

A_HEADS = 8
A_QK = 64
A_V = 128
A_QK_W = A_HEADS * A_QK
A_V_W = A_HEADS * A_V
MLSTM_CHUNK = 256
MLSTM_GROUP = 4
MLSTM_TB = 2048

B_HEADS = 8
B_Q_LORA = 384
B_KV_LORA = 256
B_NOPE = 128
B_ROPE = 64
B_V = 128
ROPE_THETA = 10000.0
HEAD_CAT = 256
ATTN_TQ = 1024
ROW_TM = 1024
POST_TM = 512
ROW_SPLIT = 2

LANES = 128
VMEM_LIMIT = 56 * 1024 * 1024


def _dot(a, b):
    return jnp.dot(a, b, preferred_element_type=F32)


def _dot_nt(a, b):
    return lax.dot_general(a, b, (((1,), (1,)), ((), ())), preferred_element_type=F32)


def _dot_tn(a, b):
    return lax.dot_general(a, b, (((0,), (0,)), ((), ())), preferred_element_type=F32)


def _rms(x):
    return x * lax.rsqrt(jnp.mean(x * x, axis=-1, keepdims=True) + EPS)


def _const_spec(shape):
    nd = len(shape)
    return pl.BlockSpec(shape, lambda *_: (0,) * nd, pipeline_mode=pl.Buffered(1))


def _adaln_kernel(c_ref, w_ref, b_ref, o_ref):
    c = c_ref[...]
    cond = c * jax.nn.sigmoid(c)
    o_ref[...] = _dot(cond.astype(BF16), w_ref[...].astype(BF16)) + b_ref[...]


def _adaln(c8, w, b, tn):
    nl, d, n = w.shape
    return pl.pallas_call(
        _adaln_kernel,
        grid=(nl, n // tn),
        in_specs=[
            pl.BlockSpec((8, d), lambda l, j: (0, 0)),
            pl.BlockSpec((None, d, tn), lambda l, j: (l, 0, j)),
            pl.BlockSpec((None, 1, tn), lambda l, j: (l, 0, j)),
        ],
        out_specs=pl.BlockSpec((None, 8, tn), lambda l, j: (l, 0, j)),
        out_shape=jax.ShapeDtypeStruct((nl, 8, n), F32),
        compiler_params=pltpu.CompilerParams(
            dimension_semantics=("arbitrary", "arbitrary"), vmem_limit_bytes=VMEM_LIMIT),
        name="adaln",
    )(c8, w, b)


A_GATE_ROWS = 2 * A_HEADS
WT_Q0, WT_V0, WT_O0, WT_G0 = 0, A_QK_W, A_QK_W + A_V_W, A_QK_W + 2 * A_V_W
WT_ROWS = WT_G0 + A_GATE_ROWS


def _inproj_kernel(x_ref, g_ref, mod_ref, wk_ref, wt_ref, k_ref, qt_ref, vt_ref, ot_ref, gt_ref):
    ts = x_ref.shape[0] // ROW_SPLIT
    for i in range(ROW_SPLIT):
        p = slice(i * ts, (i + 1) * ts)
        h = _rms(x_ref[p, :]) * g_ref[...]
        h = (h * (1.0 + mod_ref[1:2, :]) + mod_ref[0:1, :]).astype(BF16)
        k_ref[p, :] = (_dot(h, wk_ref[...]) * (A_QK ** -0.5)).astype(BF16)
        feat = _dot_nt(wt_ref[...], h)
        qt_ref[:, p] = feat[WT_Q0:WT_V0, :].astype(BF16)
        vt_ref[:, p] = feat[WT_V0:WT_O0, :].astype(BF16)
        ot_ref[:, p] = feat[WT_O0:WT_G0, :].astype(BF16)
        gt_ref[:, p] = feat[WT_G0:WT_ROWS, :]


def _inproj(x, g, mod, wk, wt, seq, tm):
    t, d = x.shape
    bsz = t // seq
    per_b = seq // tm
    row = lambda i: (i, 0)
    col = lambda i: (i // per_b, 0, i % per_b)
    return pl.pallas_call(
        _inproj_kernel,
        grid=(t // tm,),
        in_specs=[
            pl.BlockSpec((tm, d), row),
            _const_spec((1, d)),
            pl.BlockSpec((None, 8, d), lambda i: (i // per_b, 0, 0)),
            _const_spec(wk.shape),
            _const_spec(wt.shape),
        ],
        out_specs=[
            pl.BlockSpec((tm, A_QK_W), row),
            pl.BlockSpec((None, A_QK_W, tm), col),
            pl.BlockSpec((None, A_V_W, tm), col),
            pl.BlockSpec((None, A_V_W, tm), col),
            pl.BlockSpec((None, A_GATE_ROWS, tm), col),
        ],
        out_shape=[
            jax.ShapeDtypeStruct((t, A_QK_W), BF16),
            jax.ShapeDtypeStruct((bsz, A_QK_W, seq), BF16),
            jax.ShapeDtypeStruct((bsz, A_V_W, seq), BF16),
            jax.ShapeDtypeStruct((bsz, A_V_W, seq), BF16),
            jax.ShapeDtypeStruct((bsz, A_GATE_ROWS, seq), F32),
        ],
        compiler_params=pltpu.CompilerParams(
            dimension_semantics=("arbitrary",), vmem_limit_bytes=VMEM_LIMIT),
        name="inproj",
    )(x, g, mod, wk, wt)


ONES_ROWS = 16
CT_ROWS = A_V + ONES_ROWS


def _scan_lanes(x, op, fill, seg):
    pos = lax.broadcasted_iota(jnp.int32, x.shape, 1) % seg
    sh = 1
    while sh < seg:
        x = op(x, jnp.where(pos >= sh, pltpu.roll(x, sh, axis=1), fill))
        sh *= 2
    return x


def _gates_kernel(gt_ref, gb_ref, bb_ref, r_ref, cm_ref, *, chunk):
    H = A_HEADS
    ig = gt_ref[0:H, :] + gb_ref[0:H, 0:1]
    fg = gt_ref[H:2 * H, :] + gb_ref[H:2 * H, 0:1]
    bb = _scan_lanes(jax.nn.log_sigmoid(fg) * LOG2E, jnp.add, 0.0, chunk)
    r = ig * LOG2E - bb
    bb_ref[...] = bb
    r_ref[...] = r
    cm_ref[...] = _scan_lanes(r, jnp.maximum, -jnp.inf, chunk)


def _gates(gt, gate_b, chunk):
    bsz, _, seq = gt.shape
    spec = pl.BlockSpec((None, A_HEADS, seq), lambda b: (b, 0, 0))
    shape = jax.ShapeDtypeStruct((bsz, A_HEADS, seq), F32)
    return pl.pallas_call(
        functools.partial(_gates_kernel, chunk=chunk),
        grid=(bsz,),
        in_specs=[pl.BlockSpec((None, A_GATE_ROWS, seq), lambda b: (b, 0, 0)),
                  pl.BlockSpec(gate_b.shape, lambda b: (0, 0))],
        out_specs=[spec, spec, spec],
        out_shape=[shape, shape, shape],
        compiler_params=pltpu.CompilerParams(
            dimension_semantics=("arbitrary",), vmem_limit_bytes=VMEM_LIMIT),
        name="gates",
    )(gt, gate_b)


def _mlstm_kernel(k_ref, qt_ref, vt_ref, ot_ref, bb_ref, r_ref, cm_ref, hg_ref, out_ref,
                  ct_ref, m_ref, *, chunk, n_chunks):
    L = chunk
    H = A_HEADS

    @pl.when(pl.program_id(1) == 0)
    def _():
        ct_ref[...] = jnp.zeros_like(ct_ref)
        m_ref[...] = jnp.zeros_like(m_ref)

    src = lax.broadcasted_iota(jnp.int32, (L, L), 0)
    tgt = lax.broadcasted_iota(jnp.int32, (L, L), 1)
    causal = src <= tgt
    ones_rows = jnp.ones((ONES_ROWS, L), BF16)

    for c in range(n_chunks):
        cs = slice(c * L, (c + 1) * L)
        bb = bb_ref[:, cs]
        r = r_ref[:, cs]
        m_prev = m_ref[...]
        mm = jnp.maximum(m_prev, cm_ref[:, cs])
        e_inv = jnp.exp2(-(bb + mm))
        w_inter = jnp.exp2(m_prev - mm)
        mm_last = mm[:, L - 1:L]
        w_upd = jnp.exp2(r - mm_last)
        decay = jnp.exp2(m_prev - mm_last)
        m_ref[...] = bb[:, L - 1:L] + mm_last
        r_cols = jnp.concatenate([r, jnp.zeros((L - H, L), F32)], axis=0).T

        G = MLSTM_GROUP
        GW = G * A_QK
        lane_grp = lax.broadcasted_iota(jnp.int32, (L, GW), 1) // A_QK
        row_grp = lax.broadcasted_iota(jnp.int32, (1, GW), 1) // A_QK
        zq = jnp.zeros((A_QK, L), BF16)
        for g in range(H // G):
            hb = g * G
            kg = k_ref[cs, g * GW:(g + 1) * GW]
            q_bd = jnp.concatenate([
                jnp.concatenate([qt_ref[(hb + i) * A_QK:(hb + i + 1) * A_QK, cs] if i == j else zq
                                 for j in range(G)], axis=1) for i in range(G)], axis=0)
            ctg = ct_ref[g]
            st_g = _dot(kg, q_bd)
            inter_g = _dot(ctg.astype(BF16), q_bd)
            pts, vexts = [], []
            for j in range(G):
                h = hb + j
                arg = jnp.where(causal, r_cols[:, h:h + 1] - mm[h:h + 1, :], -jnp.inf)
                pts.append((st_g[:, j * L:(j + 1) * L] * jnp.exp2(arg)).astype(BF16))
                vexts.append(jnp.concatenate([vt_ref[h * A_V:(h + 1) * A_V, cs], ones_rows], axis=0))
            wvs = []
            for j0 in range(0, G, 2):
                for j in (j0, j0 + 1):
                    h = hb + j
                    tot = (w_inter[h:h + 1, :] * inter_g[:, j * L:(j + 1) * L]
                           + _dot(vexts[j], pts[j]))
                    den = tot[A_V:A_V + 1, :]
                    hh = tot[0:A_V, :] * (1.0 / jnp.maximum(jnp.abs(den), e_inv[h:h + 1, :]))
                    ms = jnp.mean(hh * hh, axis=0, keepdims=True)
                    hn = hh * lax.rsqrt(ms + EPS) * hg_ref[h * A_V:(h + 1) * A_V, :]
                    og = ot_ref[h * A_V:(h + 1) * A_V, cs].astype(F32)
                    out_ref[h * A_V:(h + 1) * A_V, cs] = (hn * jax.nn.sigmoid(og)).astype(BF16)
                    wvs.append((vexts[j].astype(F32) * w_upd[h:h + 1, :]).astype(BF16))
            zk = jnp.zeros((L, GW), BF16)
            k_bd = jnp.concatenate([jnp.where(lane_grp == i, kg, zk) for i in range(G)], axis=0)
            decay_g = decay[hb:hb + 1, :]
            for i in range(1, G):
                decay_g = jnp.where(row_grp >= i, decay[hb + i:hb + i + 1, :], decay_g)
            ct_ref[g] = decay_g * ctg + _dot(jnp.concatenate(wvs, axis=1), k_bd)


def _mlstm(k, qt, vt, ot, bb, r, cm, head_g, chunk, tb):
    bsz, _, seq = qt.shape
    per_b = seq // tb
    col = lambda b, i: (b, 0, i)
    gate_spec = pl.BlockSpec((None, A_HEADS, tb), col)
    return pl.pallas_call(
        functools.partial(_mlstm_kernel, chunk=chunk, n_chunks=tb // chunk),
        grid=(bsz, per_b),
        in_specs=[
            pl.BlockSpec((tb, A_QK_W), lambda b, i: (b * per_b + i, 0)),
            pl.BlockSpec((None, A_QK_W, tb), col),
            pl.BlockSpec((None, A_V_W, tb), col),
            pl.BlockSpec((None, A_V_W, tb), col),
            gate_spec, gate_spec, gate_spec,
            pl.BlockSpec(head_g.shape, lambda b, i: (0, 0)),
        ],
        out_specs=pl.BlockSpec((None, A_V_W, tb), col),
        out_shape=jax.ShapeDtypeStruct((bsz, A_V_W, seq), BF16),
        scratch_shapes=[
            pltpu.VMEM((A_HEADS // MLSTM_GROUP, CT_ROWS, MLSTM_GROUP * A_QK), F32),
            pltpu.VMEM((A_HEADS, 1), F32),
        ],
        compiler_params=pltpu.CompilerParams(
            dimension_semantics=("arbitrary", "arbitrary"), vmem_limit_bytes=VMEM_LIMIT),
        name="mlstm",
    )(k, qt, vt, ot, bb, r, cm, head_g)


def _post_kernel(x_ref, mix_ref, mod_ref, ng_ref, wo_ref, w1_ref, w2_ref, out_ref, *, ff_chunk):
    ts = x_ref.shape[0] // ROW_SPLIT
    parts = [slice(i * ts, (i + 1) * ts) for i in range(ROW_SPLIT)]
    ys = [_dot_tn(mix_ref[:, p], wo_ref[...]) for p in parts]
    outs = []
    for p, y in zip(parts, ys):
        x1 = x_ref[p, :] + mod_ref[2:3, :] * (_rms(y) * ng_ref[1:2, :])
        h = _rms(x1) * ng_ref[2:3, :]
        h = (h * (1.0 + mod_ref[4:5, :]) + mod_ref[3:4, :]).astype(BF16)
        acc = None
        for j in range(D_FF // ff_chunk):
            a = _dot(h, w1_ref[:, j * ff_chunk:(j + 1) * ff_chunk])
            a = jnp.square(jnp.maximum(a, 0.0)).astype(BF16)
            part = _dot(a, w2_ref[j * ff_chunk:(j + 1) * ff_chunk, :])
            acc = part if acc is None else acc + part
        outs.append((p, x1, acc))
    for p, x1, acc in outs:
        out_ref[p, :] = x1 + mod_ref[5:6, :] * (_rms(acc) * ng_ref[3:4, :])


def _post(x, mix_t, mod, ng, wo, w1, w2, layer, seq, tm):
    t, d = x.shape
    per_b = seq // tm
    row = lambda i: (i, 0)
    layer_spec = lambda w: pl.BlockSpec((None,) + w.shape[1:], lambda i: (layer, 0, 0),
                                        pipeline_mode=pl.Buffered(1))
    return pl.pallas_call(
        functools.partial(_post_kernel, ff_chunk=1024),
        grid=(t // tm,),
        in_specs=[
            pl.BlockSpec((tm, d), row),
            pl.BlockSpec((None, d, tm), lambda i: (i // per_b, 0, i % per_b)),
            pl.BlockSpec((None, 8, d), lambda i: (i // per_b, 0, 0)),
            _const_spec((4, d)),
            _const_spec(wo.shape),
            layer_spec(w1),
            layer_spec(w2),
        ],
        out_specs=pl.BlockSpec((tm, d), row),
        out_shape=jax.ShapeDtypeStruct((t, d), F32),
        compiler_params=pltpu.CompilerParams(
            dimension_semantics=("arbitrary",), vmem_limit_bytes=VMEM_LIMIT),
        name="post",
    )(x, mix_t, mod, ng, wo, w1, w2)


ROPE_HALF = B_ROPE // 2
Q_HEAD = B_NOPE + B_ROPE
V_ROWS_T = B_V + 16
WA_COLS = B_KV_LORA + 2 * B_ROPE


def _l1proj_kernel(x_ref, pos_ref, inv_ref, ng_ref, mod_ref, wa_ref, gl_ref, wbk_ref, wbv_ref,
                   wqa_ref, gq_ref, wq_ref, wqs_ref, q_ref, k_ref, v_ref, *, q_scale):
    tm = x_ref.shape[0]
    xn = _rms(x_ref[...])
    ang = inv_ref[...] * pos_ref[...].astype(F32)
    cos_h = jnp.cos(ang)
    sin_h = jnp.sin(ang)
    cos_t = jnp.concatenate([cos_h, cos_h], axis=0)
    sin_t = jnp.concatenate([-sin_h, sin_h], axis=0)

    hk = xn * ng_ref[0:1, :]
    hk = (hk * (1.0 + mod_ref[1:2, :]) + mod_ref[0:1, :]).astype(BF16)
    kva = _dot(hk, wa_ref[...])
    ckv = (_rms(kva[:, 0:B_KV_LORA]) * gl_ref[...]).astype(BF16)
    kr_t = kva[:, B_KV_LORA:WA_COLS].T
    k_rope_t = kr_t[0:B_ROPE, :] * cos_t + kr_t[B_ROPE:2 * B_ROPE, :] * sin_t
    k_rope = jnp.concatenate([k_rope_t, jnp.zeros((LANES - B_ROPE, tm), F32)], axis=0).T
    k_rope = k_rope.astype(BF16)
    k_nope = _dot(ckv, wbk_ref[...])
    for h in range(B_HEADS):
        k_ref[:, h * HEAD_CAT:h * HEAD_CAT + B_NOPE] = k_nope[:, h * B_NOPE:(h + 1) * B_NOPE].astype(BF16)
        k_ref[:, h * HEAD_CAT + B_NOPE:(h + 1) * HEAD_CAT] = k_rope
    v_t = _dot_nt(wbv_ref[...], ckv)
    ones = jnp.ones((V_ROWS_T - B_V, tm), BF16)
    for h in range(B_HEADS):
        v_ref[h * V_ROWS_T:h * V_ROWS_T + B_V, :] = v_t[h * B_V:(h + 1) * B_V, :].astype(BF16)
        v_ref[h * V_ROWS_T + B_V:(h + 1) * V_ROWS_T, :] = ones

    hq = xn * ng_ref[1:2, :]
    hq = (hq * (1.0 + mod_ref[3:4, :]) + mod_ref[2:3, :]).astype(BF16)
    cq = (_rms(_dot(hq, wqa_ref[...])) * gq_ref[...]).astype(BF16)
    q_t = _dot_nt(wq_ref[...], cq)
    qs_t = _dot_nt(wqs_ref[...], cq)
    zeros = jnp.zeros((HEAD_CAT - Q_HEAD, tm), BF16)
    for h in range(B_HEADS):
        r0 = h * Q_HEAD
        q_ref[h * HEAD_CAT:h * HEAD_CAT + B_NOPE, :] = (q_t[r0:r0 + B_NOPE, :] * q_scale).astype(BF16)
        rope = q_t[r0 + B_NOPE:r0 + Q_HEAD, :] * cos_t + qs_t[h * B_ROPE:(h + 1) * B_ROPE, :] * sin_t
        q_ref[h * HEAD_CAT + B_NOPE:h * HEAD_CAT + Q_HEAD, :] = (rope * q_scale).astype(BF16)
        q_ref[h * HEAD_CAT + Q_HEAD:(h + 1) * HEAD_CAT, :] = zeros


def _l1proj(x, pos, inv, ng, mod, wa, gl, wbk, wbv, wqa, gq, wq, wqs, seq, tm, tq):
    t, d = x.shape
    bsz = t // seq
    per_b = seq // tm
    per_q = tq // tm
    row = lambda i: (i, 0)
    q_scale = float((B_NOPE + B_ROPE) ** -0.5 * math.log2(math.e))
    consts = [inv, ng, None, wa, gl, wbk, wbv, wqa, gq, wq, wqs]
    in_specs = [pl.BlockSpec((tm, d), row),
                pl.BlockSpec((None, 1, tm), lambda i: (i // per_b, 0, i % per_b))]
    for a in consts:
        if a is None:
            in_specs.append(pl.BlockSpec((None, 8, d), lambda i: (i // per_b, 0, 0)))
        else:
            in_specs.append(_const_spec(a.shape))
    return pl.pallas_call(
        functools.partial(_l1proj_kernel, q_scale=q_scale),
        grid=(t // tm,),
        in_specs=in_specs,
        out_specs=[
            pl.BlockSpec((None, B_HEADS * HEAD_CAT, tm), lambda i: (i // per_b, 0, i % per_b)),
            pl.BlockSpec((tm, B_HEADS * HEAD_CAT), row),
            pl.BlockSpec((None, None, B_HEADS * V_ROWS_T, tm),
                         lambda i: (i // per_b, (i % per_b) // per_q, 0, i % per_q)),
        ],
        out_shape=[
            jax.ShapeDtypeStruct((bsz, B_HEADS * HEAD_CAT, seq), BF16),
            jax.ShapeDtypeStruct((t, B_HEADS * HEAD_CAT), BF16),
            jax.ShapeDtypeStruct((bsz, seq // tq, B_HEADS * V_ROWS_T, tq), BF16),
        ],
        compiler_params=pltpu.CompilerParams(
            dimension_semantics=("arbitrary",), vmem_limit_bytes=VMEM_LIMIT),
        name="l1proj",
    )(x, pos, inv, ng, mod, wa, gl, wbk, wbv, wqa, gq, wq, wqs)


def _attn_kernel(qt_ref, k_ref, vt_ref, o_ref, sa_ref, sb_ref, *, tq):
    qi = pl.program_id(2)

    def scores(j):
        rows = pl.ds(pl.multiple_of(j * tq, tq), tq)
        return _dot(k_ref[rows, :], qt_ref[...])

    def soft_pv(j, s, m, acc, masked=False):
        if masked:
            keys = lax.broadcasted_iota(jnp.int32, (tq, tq), 0)
            qrys = lax.broadcasted_iota(jnp.int32, (tq, tq), 1)
            s = jnp.where(keys <= qrys, s, -jnp.inf)
        m_new = jnp.maximum(m, jnp.max(s, axis=0, keepdims=True))
        p = jnp.exp2(s - m_new).astype(BF16)
        acc = jnp.exp2(m - m_new) * acc + _dot(vt_ref[j], p)
        return m_new, acc

    sa_ref[...] = scores(0)

    def pair(jj, carry):
        m, acc = carry
        j0 = 2 * jj
        sb_ref[...] = scores(j0 + 1)
        m, acc = soft_pv(j0, sa_ref[...], m, acc)
        sa_ref[...] = scores(j0 + 2)
        return soft_pv(j0 + 1, sb_ref[...], m, acc)

    init = (jnp.full((1, tq), -jnp.inf, F32), jnp.zeros((V_ROWS_T, tq), F32))
    m, acc = lax.fori_loop(0, qi // 2, pair, init)

    def odd_tail(m, acc):
        sb_ref[...] = scores(qi)
        m, acc = soft_pv(qi - 1, sa_ref[...], m, acc)
        return soft_pv(qi, sb_ref[...], m, acc, masked=True)

    def even_tail(m, acc):
        return soft_pv(qi, sa_ref[...], m, acc, masked=True)

    _, acc = lax.cond(qi % 2 == 1, odd_tail, even_tail, m, acc)
    o_ref[...] = (acc[0:B_V, :] * (1.0 / acc[B_V:B_V + 1, :])).astype(BF16)


def _attn(q_t, k, v_t, tq):
    b, s, _ = k.shape
    return pl.pallas_call(
        functools.partial(_attn_kernel, tq=tq),
        grid=(b, B_HEADS, s // tq),
        in_specs=[
            pl.BlockSpec((None, HEAD_CAT, tq), lambda bi, h, i: (bi, h, i)),
            pl.BlockSpec((None, s, HEAD_CAT), lambda bi, h, i: (bi, 0, h)),
            pl.BlockSpec((None, s // tq, V_ROWS_T, tq), lambda bi, h, i: (bi, 0, h, 0)),
        ],
        out_specs=pl.BlockSpec((None, B_V, tq), lambda bi, h, i: (bi, h, i)),
        out_shape=jax.ShapeDtypeStruct((b, B_HEADS * B_V, s), BF16),
        scratch_shapes=[pltpu.VMEM((tq, tq), F32), pltpu.VMEM((tq, tq), F32)],
        compiler_params=pltpu.CompilerParams(
            dimension_semantics=("arbitrary", "arbitrary", "arbitrary"),
            vmem_limit_bytes=VMEM_LIMIT),
        name="attn",
    )(q_t, k, v_t)


def _swap_halves(w):
    half = w.shape[-1] // 2
    return jnp.concatenate([w[..., half:], w[..., :half]], axis=-1)


def kernel(x, c, positions, ada_w, ada_b, norm_g, a_w_in, a_gate_b, a_head_g, a_w_out,
           kv_ada_w, kv_ada_b, kv_norm_g, kv_w_a, kv_latent_g, kv_w_b, b_w_q_a, b_q_latent_g,
           b_w_q_b, b_w_out, mlp_w1, mlp_w2):
    bsz, seq, d = x.shape
    t = bsz * seq
    x2d = x.reshape(t, d)

    c8 = jnp.pad(c, ((0, 8 - bsz), (0, 0)))
    ada = _adaln(c8, ada_w, ada_b[:, None, :], tn=1536)[:, :bsz]
    kv_ada = _adaln(c8, kv_ada_w[None], kv_ada_b[None, None, :], tn=1024)[0, :bsz]

    def mod_rows(vecs):
        rows = [v.reshape(bsz, 1, d) for v in vecs]
        rows += [jnp.zeros((bsz, 1, d), F32)] * (8 - len(rows))
        return jnp.concatenate(rows, axis=1)

    ada0 = [ada[0][:, i * d:(i + 1) * d] for i in range(6)]
    ada1 = [ada[1][:, i * d:(i + 1) * d] for i in range(6)]
    kv_shift, kv_scale = kv_ada[:, :d], kv_ada[:, d:]

    w_in = a_w_in[0]
    c0 = 2 * A_QK_W + A_V_W
    wk = w_in[:, A_QK_W:2 * A_QK_W].astype(BF16)
    wt = jnp.concatenate([
        w_in[:, :A_QK_W],
        w_in[:, 2 * A_QK_W:c0],
        w_in[:, c0 + 2 * A_HEADS:],
        w_in[:, c0:c0 + 2 * A_HEADS],
    ], axis=1).T.astype(BF16)
    k, qt, vt, ot, gt = _inproj(x2d, norm_g[0, 0][None], mod_rows(ada0[:2]), wk, wt, seq, tm=ROW_TM)
    gate_b = jnp.broadcast_to(a_gate_b[0].reshape(A_GATE_ROWS, 1), (A_GATE_ROWS, LANES))
    head_g = jnp.broadcast_to(a_head_g[0].reshape(A_V_W, 1), (A_V_W, MLSTM_CHUNK))
    bb, r, cm = _gates(gt, gate_b, MLSTM_CHUNK)
    mix0 = _mlstm(k, qt, vt, ot, bb, r, cm, head_g, MLSTM_CHUNK, tb=MLSTM_TB)
    w1_all, w2_all = mlp_w1.astype(BF16), mlp_w2.astype(BF16)
    x2d = _post(x2d, mix0, mod_rows(ada0), norm_g[0], a_w_out[0].astype(BF16), w1_all, w2_all, 0,
                seq, tm=POST_TM)

    tq = ATTN_TQ
    rope = kv_w_a[:, B_KV_LORA:]
    wa_cat = jnp.concatenate([kv_w_a[:, :B_KV_LORA], rope, _swap_halves(rope)], axis=1).astype(BF16)
    wb = kv_w_b.reshape(B_KV_LORA, B_HEADS, B_NOPE + B_V)
    wbk = wb[:, :, :B_NOPE].reshape(B_KV_LORA, -1).astype(BF16)
    wbv = wb[:, :, B_NOPE:].reshape(B_KV_LORA, -1).T.astype(BF16)
    wq = b_w_q_b[0].T.astype(BF16)
    wq_rope = b_w_q_b[0].reshape(B_Q_LORA, B_HEADS, Q_HEAD)[:, :, B_NOPE:]
    wqs = _swap_halves(wq_rope).reshape(B_Q_LORA, -1).T.astype(BF16)
    inv = ROPE_THETA ** (-jnp.arange(ROPE_HALF, dtype=F32) / ROPE_HALF)
    inv_rep = jnp.broadcast_to(inv[:, None], (ROPE_HALF, ROW_TM))
    ng1 = jnp.concatenate([kv_norm_g[None], norm_g[1, 0][None]], axis=0)
    mod1 = mod_rows([kv_shift, kv_scale, ada1[0], ada1[1]])
    q_t, k_cat, v_t = _l1proj(
        x2d, positions.reshape(bsz, 1, seq), inv_rep, ng1, mod1, wa_cat, kv_latent_g[None], wbk, wbv,
        b_w_q_a[0].astype(BF16), b_q_latent_g[0][None], wq, wqs, seq, tm=ROW_TM, tq=tq)
    o = _attn(q_t, k_cat.reshape(bsz, seq, -1), v_t, tq=tq)
    x2d = _post(x2d, o, mod_rows(ada1), norm_g[1], b_w_out[0].astype(BF16), w1_all, w2_all, 1,
                seq, tm=POST_TM)
    return x2d.reshape(bsz, seq, d)
```

```python
import functools
import math

import jax
import jax.numpy as jnp
from jax import lax
from jax.experimental import pallas as pl
from jax.experimental.pallas import tpu as pltpu

F32 = jnp.float32
BF16 = jnp.bfloat16

D_MODEL = 1024
D_FF = 4 * D_MODEL
EPS = 1e-6
LOG2E = math.log2(math.e)

A_HEADS = 8
A_QK = 64
A_V = 128
A_QK_W = A_HEADS * A_QK
A_V_W = A_HEADS * A_V
MLSTM_CHUNK = 256
MLSTM_GROUP = 4
MLSTM_TB = 2048

B_HEADS = 8
B_Q_LORA = 384
B_KV_LORA = 256
B_NOPE = 128
B_ROPE = 64
B_V = 128
ROPE_THETA = 10000.0
HEAD_CAT = 256
ATTN_TQ = 1024
ROW_TM = 1024
ROW_SPLIT = 2
POST_SPLIT = 4

LANES = 128
VMEM_LIMIT = 56 * 1024 * 1024


def _dot(a, b):
    return jnp.dot(a, b, preferred_element_type=F32)


def _dot_nt(a, b):
    return lax.dot_general(a, b, (((1,), (1,)), ((), ())), preferred_element_type=F32)


def _dot_tn(a, b):
    return lax.dot_general(a, b, (((0,), (0,)), ((), ())), preferred_element_type=F32)


def _rms(x):
    return x * lax.rsqrt(jnp.mean(x * x, axis=-1, keepdims=True) + EPS)


def _const_spec(shape):
    nd = len(shape)
    return pl.BlockSpec(shape, lambda *_: (0,) * nd, pipeline_mode=pl.Buffered(1))


def _adaln_kernel(c_ref, w_ref, b_ref, o_ref):
    c = c_ref[...]
    cond = c * jax.nn.sigmoid(c)
    o_ref[...] = _dot(cond.astype(BF16), w_ref[...].astype(BF16)) + b_ref[...]


def _adaln(c8, w, b, tn):
    nl, d, n = w.shape
    return pl.pallas_call(
        _adaln_kernel,
        grid=(nl, n // tn),
        in_specs=[
            pl.BlockSpec((8, d), lambda l, j: (0, 0)),
            pl.BlockSpec((None, d, tn), lambda l, j: (l, 0, j)),
            pl.BlockSpec((None, 1, tn), lambda l, j: (l, 0, j)),
        ],
        out_specs=pl.BlockSpec((None, 8, tn), lambda l, j: (l, 0, j)),
        out_shape=jax.ShapeDtypeStruct((nl, 8, n), F32),
        compiler_params=pltpu.CompilerParams(
            dimension_semantics=("arbitrary", "arbitrary"), vmem_limit_bytes=VMEM_LIMIT),
        name="adaln",
    )(c8, w, b)


A_GATE_ROWS = 2 * A_HEADS
WT_Q0, WT_V0, WT_O0, WT_G0 = 0, A_QK_W, A_QK_W + A_V_W, A_QK_W + 2 * A_V_W
WT_ROWS = WT_G0 + A_GATE_ROWS


def _inproj_kernel(x_ref, g_ref, mod_ref, wk_ref, wt_ref, k_ref, qt_ref, vt_ref, ot_ref, gt_ref):
    ts = x_ref.shape[0] // ROW_SPLIT
    for i in range(ROW_SPLIT):
        p = slice(i * ts, (i + 1) * ts)
        h = _rms(x_ref[p, :]) * g_ref[...]
        h = (h * (1.0 + mod_ref[1:2, :]) + mod_ref[0:1, :]).astype(BF16)
        k_ref[p, :] = (_dot(h, wk_ref[...]) * (A_QK ** -0.5)).astype(BF16)
        feat = _dot_nt(wt_ref[...], h)
        qt_ref[:, p] = feat[WT_Q0:WT_V0, :].astype(BF16)
        vt_ref[:, p] = feat[WT_V0:WT_O0, :].astype(BF16)
        ot_ref[:, p] = feat[WT_O0:WT_G0, :].astype(BF16)
        gt_ref[:, p] = feat[WT_G0:WT_ROWS, :]


def _inproj(x, g, mod, wk, wt, seq, tm):
    t, d = x.shape
    bsz = t // seq
    per_b = seq // tm
    row = lambda i: (i, 0)
    col = lambda i: (i // per_b, 0, i % per_b)
    return pl.pallas_call(
        _inproj_kernel,
        grid=(t // tm,),
        in_specs=[
            pl.BlockSpec((tm, d), row),
            _const_spec((1, d)),
            pl.BlockSpec((None, 8, d), lambda i: (i // per_b, 0, 0)),
            _const_spec(wk.shape),
            _const_spec(wt.shape),
        ],
        out_specs=[
            pl.BlockSpec((tm, A_QK_W), row),
            pl.BlockSpec((None, A_QK_W, tm), col),
            pl.BlockSpec((None, A_V_W, tm), col),
            pl.BlockSpec((None, A_V_W, tm), col),
            pl.BlockSpec((None, A_GATE_ROWS, tm), col),
        ],
        out_shape=[
            jax.ShapeDtypeStruct((t, A_QK_W), BF16),
            jax.ShapeDtypeStruct((bsz, A_QK_W, seq), BF16),
            jax.ShapeDtypeStruct((bsz, A_V_W, seq), BF16),
            jax.ShapeDtypeStruct((bsz, A_V_W, seq), BF16),
            jax.ShapeDtypeStruct((bsz, A_GATE_ROWS, seq), F32),
        ],
        compiler_params=pltpu.CompilerParams(
            dimension_semantics=("arbitrary",), vmem_limit_bytes=VMEM_LIMIT),
        name="inproj",
    )(x, g, mod, wk, wt)


ONES_ROWS = 16
CT_ROWS = A_V + ONES_ROWS


def _scan_lanes(x, op, fill, seg):
    pos = lax.broadcasted_iota(jnp.int32, x.shape, 1) % seg
    sh = 1
    while sh < seg:
        x = op(x, jnp.where(pos >= sh, pltpu.roll(x, sh, axis=1), fill))
        sh *= 2
    return x


def _gates_kernel(gt_ref, gb_ref, bb_ref, r_ref, cm_ref, *, chunk):
    H = A_HEADS
    ig = gt_ref[0:H, :] + gb_ref[0:H, 0:1]
    fg = gt_ref[H:2 * H, :] + gb_ref[H:2 * H, 0:1]
    bb = _scan_lanes(jax.nn.log_sigmoid(fg) * LOG2E, jnp.add, 0.0, chunk)
    r = ig * LOG2E - bb
    bb_ref[...] = bb
    r_ref[...] = r
    cm_ref[...] = _scan_lanes(r, jnp.maximum, -jnp.inf, chunk)


def _gates(gt, gate_b, chunk):
    bsz, _, seq = gt.shape
    spec = pl.BlockSpec((None, A_HEADS, seq), lambda b: (b, 0, 0))
    shape = jax.ShapeDtypeStruct((bsz, A_HEADS, seq), F32)
    return pl.pallas_call(
        functools.partial(_gates_kernel, chunk=chunk),
        grid=(bsz,),
        in_specs=[pl.BlockSpec((None, A_GATE_ROWS, seq), lambda b: (b, 0, 0)),
                  pl.BlockSpec(gate_b.shape, lambda b: (0, 0))],
        out_specs=[spec, spec, spec],
        out_shape=[shape, shape, shape],
        compiler_params=pltpu.CompilerParams(
            dimension_semantics=("arbitrary",), vmem_limit_bytes=VMEM_LIMIT),
        name="gates",
    )(gt, gate_b)


def _mlstm_kernel(k_ref, qt_ref, vt_ref, ot_ref, bb_ref, r_ref, cm_ref, hg_ref, out_ref,
                  ct_ref, m_ref, *, chunk, n_chunks):
    L = chunk
    H = A_HEADS

    @pl.when(pl.program_id(1) == 0)
    def _():
        ct_ref[...] = jnp.zeros_like(ct_ref)
        m_ref[...] = jnp.zeros_like(m_ref)

    src = lax.broadcasted_iota(jnp.int32, (L, L), 0)
    tgt = lax.broadcasted_iota(jnp.int32, (L, L), 1)
    causal = src <= tgt
    ones_rows = jnp.ones((ONES_ROWS, L), BF16)

    for c in range(n_chunks):
        cs = slice(c * L, (c + 1) * L)
        bb = bb_ref[:, cs]
        r = r_ref[:, cs]
        m_prev = m_ref[...]
        mm = jnp.maximum(m_prev, cm_ref[:, cs])
        e_inv = jnp.exp2(-(bb + mm))
        w_inter = jnp.exp2(m_prev - mm)
        mm_last = mm[:, L - 1:L]
        w_upd = jnp.exp2(r - mm_last)
        decay = jnp.exp2(m_prev - mm_last)
        m_ref[...] = bb[:, L - 1:L] + mm_last
        r_cols = jnp.concatenate([r, jnp.zeros((L - H, L), F32)], axis=0).T

        G = MLSTM_GROUP
        GW = G * A_QK
        lane_grp = lax.broadcasted_iota(jnp.int32, (L, GW), 1) // A_QK
        row_grp = lax.broadcasted_iota(jnp.int32, (1, GW), 1) // A_QK
        zq = jnp.zeros((A_QK, L), BF16)
        for g in range(H // G):
            hb = g * G
            kg = k_ref[cs, g * GW:(g + 1) * GW]
            q_bd = jnp.concatenate([
                jnp.concatenate([qt_ref[(hb + i) * A_QK:(hb + i + 1) * A_QK, cs] if i == j else zq
                                 for j in range(G)], axis=1) for i in range(G)], axis=0)
            ctg = ct_ref[g]
            st_g = _dot(kg, q_bd)
            inter_g = _dot(ctg.astype(BF16), q_bd)
            pts, vexts = [], []
            for j in range(G):
                h = hb + j
                arg = jnp.where(causal, r_cols[:, h:h + 1] - mm[h:h + 1, :], -jnp.inf)
                pts.append((st_g[:, j * L:(j + 1) * L] * jnp.exp2(arg)).astype(BF16))
                vexts.append(jnp.concatenate([vt_ref[h * A_V:(h + 1) * A_V, cs], ones_rows], axis=0))
            wvs = []
            for j0 in range(0, G, 2):
                for j in (j0, j0 + 1):
                    h = hb + j
                    tot = (w_inter[h:h + 1, :] * inter_g[:, j * L:(j + 1) * L]
                           + _dot(vexts[j], pts[j]))
                    den = tot[A_V:A_V + 1, :]
                    hh = tot[0:A_V, :] * (1.0 / jnp.maximum(jnp.abs(den), e_inv[h:h + 1, :]))
                    ms = jnp.mean(hh * hh, axis=0, keepdims=True)
                    hn = hh * lax.rsqrt(ms + EPS) * hg_ref[h * A_V:(h + 1) * A_V, :]
                    og = ot_ref[h * A_V:(h + 1) * A_V, cs].astype(F32)
                    out_ref[h * A_V:(h + 1) * A_V, cs] = (hn * jax.nn.sigmoid(og)).astype(BF16)
                    wvs.append((vexts[j].astype(F32) * w_upd[h:h + 1, :]).astype(BF16))
            zk = jnp.zeros((L, GW), BF16)
            k_bd = jnp.concatenate([jnp.where(lane_grp == i, kg, zk) for i in range(G)], axis=0)
            decay_g = decay[hb:hb + 1, :]
            for i in range(1, G):
                decay_g = jnp.where(row_grp >= i, decay[hb + i:hb + i + 1, :], decay_g)
            ct_ref[g] = decay_g * ctg + _dot(jnp.concatenate(wvs, axis=1), k_bd)


def _mlstm(k, qt, vt, ot, bb, r, cm, head_g, chunk, tb):
    bsz, _, seq = qt.shape
    per_b = seq // tb
    col = lambda b, i: (b, 0, i)
    gate_spec = pl.BlockSpec((None, A_HEADS, tb), col)
    return pl.pallas_call(
        functools.partial(_mlstm_kernel, chunk=chunk, n_chunks=tb // chunk),
        grid=(bsz, per_b),
        in_specs=[
            pl.BlockSpec((tb, A_QK_W), lambda b, i: (b * per_b + i, 0)),
            pl.BlockSpec((None, A_QK_W, tb), col),
            pl.BlockSpec((None, A_V_W, tb), col),
            pl.BlockSpec((None, A_V_W, tb), col),
            gate_spec, gate_spec, gate_spec,
            pl.BlockSpec(head_g.shape, lambda b, i: (0, 0)),
        ],
        out_specs=pl.BlockSpec((None, A_V_W, tb), col),
        out_shape=jax.ShapeDtypeStruct((bsz, A_V_W, seq), BF16),
        scratch_shapes=[
            pltpu.VMEM((A_HEADS // MLSTM_GROUP, CT_ROWS, MLSTM_GROUP * A_QK), F32),
            pltpu.VMEM((A_HEADS, 1), F32),
        ],
        compiler_params=pltpu.CompilerParams(
            dimension_semantics=("arbitrary", "arbitrary"), vmem_limit_bytes=VMEM_LIMIT),
        name="mlstm",
    )(k, qt, vt, ot, bb, r, cm, head_g)


def _post_kernel(x_ref, mix_ref, mod_ref, ng_ref, wo_ref, w1_ref, w2_ref, out_ref, *, ff_chunk):
    ts = x_ref.shape[0] // POST_SPLIT
    parts = [slice(i * ts, (i + 1) * ts) for i in range(POST_SPLIT)]
    ys = [_dot_tn(mix_ref[:, p], wo_ref[...]) for p in parts]
    for p, y in zip(parts, ys):
        x1 = x_ref[p, :] + mod_ref[2:3, :] * (_rms(y) * ng_ref[1:2, :])
        h = _rms(x1) * ng_ref[2:3, :]
        h = (h * (1.0 + mod_ref[4:5, :]) + mod_ref[3:4, :]).astype(BF16)
        acc = None
        for j in range(D_FF // ff_chunk):
            a = _dot(h, w1_ref[:, j * ff_chunk:(j + 1) * ff_chunk])
            a = jnp.square(jnp.maximum(a, 0.0)).astype(BF16)
            part = _dot(a, w2_ref[j * ff_chunk:(j + 1) * ff_chunk, :])
            acc = part if acc is None else acc + part
        out_ref[p, :] = x1 + mod_ref[5:6, :] * (_rms(acc) * ng_ref[3:4, :])


def _post(x, mix_t, mod, ng, wo, w1, w2, layer, seq, tm):
    t, d = x.shape
    per_b = seq // tm
    row = lambda i: (i, 0)
    layer_spec = lambda w: pl.BlockSpec((None,) + w.shape[1:], lambda i: (layer, 0, 0),
                                        pipeline_mode=pl.Buffered(1))
    return pl.pallas_call(
        functools.partial(_post_kernel, ff_chunk=1024),
        grid=(t // tm,),
        in_specs=[
            pl.BlockSpec((tm, d), row),
            pl.BlockSpec((None, d, tm), lambda i: (i // per_b, 0, i % per_b)),
            pl.BlockSpec((None, 8, d), lambda i: (i // per_b, 0, 0)),
            _const_spec((4, d)),
            _const_spec(wo.shape),
            layer_spec(w1),
            layer_spec(w2),
        ],
        out_specs=pl.BlockSpec((tm, d), row),
        out_shape=jax.ShapeDtypeStruct((t, d), F32),
        compiler_params=pltpu.CompilerParams(
            dimension_semantics=("arbitrary",), vmem_limit_bytes=VMEM_LIMIT),
        name="post",
    )(x, mix_t, mod, ng, wo, w1, w2)


ROPE_HALF = B_ROPE // 2
Q_HEAD = B_NOPE + B_ROPE
V_ROWS_T = B_V + 16
WA_COLS = B_KV_LORA + 2 * B_ROPE


def _l1proj_kernel(x_ref, pos_ref, inv_ref, ng_ref, mod_ref, wa_ref, gl_ref, wbk_ref, wbv_ref,
                   wqa_ref, gq_ref, wq_ref, wqs_ref, q_ref, k_ref, v_ref, *, q_scale):
    tm = x_ref.shape[0]
    xn = _rms(x_ref[...])
    ang = inv_ref[...] * pos_ref[...].astype(F32)
    cos_h = jnp.cos(ang)
    sin_h = jnp.sin(ang)
    cos_t = jnp.concatenate([cos_h, cos_h], axis=0)
    sin_t = jnp.concatenate([-sin_h, sin_h], axis=0)

    hk = xn * ng_ref[0:1, :]
    hk = (hk * (1.0 + mod_ref[1:2, :]) + mod_ref[0:1, :]).astype(BF16)
    kva = _dot(hk, wa_ref[...])
    ckv = (_rms(kva[:, 0:B_KV_LORA]) * gl_ref[...]).astype(BF16)
    kr_t = kva[:, B_KV_LORA:WA_COLS].T
    k_rope_t = kr_t[0:B_ROPE, :] * cos_t + kr_t[B_ROPE:2 * B_ROPE, :] * sin_t
    k_rope = jnp.concatenate([k_rope_t, jnp.zeros((LANES - B_ROPE, tm), F32)], axis=0).T
    k_rope = k_rope.astype(BF16)
    k_nope = _dot(ckv, wbk_ref[...])
    for h in range(B_HEADS):
        k_ref[:, h * HEAD_CAT:h * HEAD_CAT + B_NOPE] = k_nope[:, h * B_NOPE:(h + 1) * B_NOPE].astype(BF16)
        k_ref[:, h * HEAD_CAT + B_NOPE:(h + 1) * HEAD_CAT] = k_rope
    v_t = _dot_nt(wbv_ref[...], ckv)
    ones = jnp.ones((V_ROWS_T - B_V, tm), BF16)
    for h in range(B_HEADS):
        v_ref[h * V_ROWS_T:h * V_ROWS_T + B_V, :] = v_t[h * B_V:(h + 1) * B_V, :].astype(BF16)
        v_ref[h * V_ROWS_T + B_V:(h + 1) * V_ROWS_T, :] = ones

    hq = xn * ng_ref[1:2, :]
    hq = (hq * (1.0 + mod_ref[3:4, :]) + mod_ref[2:3, :]).astype(BF16)
    cq = (_rms(_dot(hq, wqa_ref[...])) * gq_ref[...]).astype(BF16)
    q_t = _dot_nt(wq_ref[...], cq)
    qs_t = _dot_nt(wqs_ref[...], cq)
    zeros = jnp.zeros((HEAD_CAT - Q_HEAD, tm), BF16)
    for h in range(B_HEADS):
        r0 = h * Q_HEAD
        q_ref[h * HEAD_CAT:h * HEAD_CAT + B_NOPE, :] = (q_t[r0:r0 + B_NOPE, :] * q_scale).astype(BF16)
        rope = q_t[r0 + B_NOPE:r0 + Q_HEAD, :] * cos_t + qs_t[h * B_ROPE:(h + 1) * B_ROPE, :] * sin_t
        q_ref[h * HEAD_CAT + B_NOPE:h * HEAD_CAT + Q_HEAD, :] = (rope * q_scale).astype(BF16)
        q_ref[h * HEAD_CAT + Q_HEAD:(h + 1) * HEAD_CAT, :] = zeros


def _l1proj(x, pos, inv, ng, mod, wa, gl, wbk, wbv, wqa, gq, wq, wqs, seq, tm, tq):
    t, d = x.shape
    bsz = t // seq
    per_b = seq // tm
    per_q = tq // tm
    row = lambda i: (i, 0)
    q_scale = float((B_NOPE + B_ROPE) ** -0.5 * math.log2(math.e))
    consts = [inv, ng, None, wa, gl, wbk, wbv, wqa, gq, wq, wqs]
    in_specs = [pl.BlockSpec((tm, d), row),
                pl.BlockSpec((None, 1, tm), lambda i: (i // per_b, 0, i % per_b))]
    for a in consts:
        if a is None:
            in_specs.append(pl.BlockSpec((None, 8, d), lambda i: (i // per_b, 0, 0)))
        else:
            in_specs.append(_const_spec(a.shape))
    return pl.pallas_call(
        functools.partial(_l1proj_kernel, q_scale=q_scale),
        grid=(t // tm,),
        in_specs=in_specs,
        out_specs=[
            pl.BlockSpec((None, B_HEADS * HEAD_CAT, tm), lambda i: (i // per_b, 0, i % per_b)),
            pl.BlockSpec((tm, B_HEADS * HEAD_CAT), row),
            pl.BlockSpec((None, None, B_HEADS * V_ROWS_T, tm),
                         lambda i: (i // per_b, (i % per_b) // per_q, 0, i % per_q)),
        ],
        out_shape=[
            jax.ShapeDtypeStruct((bsz, B_HEADS * HEAD_CAT, seq), BF16),
            jax.ShapeDtypeStruct((t, B_HEADS * HEAD_CAT), BF16),
            jax.ShapeDtypeStruct((bsz, seq // tq, B_HEADS * V_ROWS_T, tq), BF16),
        ],
        compiler_params=pltpu.CompilerParams(
            dimension_semantics=("arbitrary",), vmem_limit_bytes=VMEM_LIMIT),
        name="l1proj",
    )(x, pos, inv, ng, mod, wa, gl, wbk, wbv, wqa, gq, wq, wqs)


def _attn_kernel(qt_ref, k_ref, vt_ref, o_ref, sa_ref, sb_ref, *, tq):
    qi = pl.program_id(2)

    def scores(j):
        rows = pl.ds(pl.multiple_of(j * tq, tq), tq)
        return _dot(k_ref[rows, :], qt_ref[...])

    def soft_pv(j, s, m, acc, masked=False):
        if masked:
            keys = lax.broadcasted_iota(jnp.int32, (tq, tq), 0)
            qrys = lax.broadcasted_iota(jnp.int32, (tq, tq), 1)
            s = jnp.where(keys <= qrys, s, -jnp.inf)
        m_new = jnp.maximum(m, jnp.max(s, axis=0, keepdims=True))
        p = jnp.exp2(s - m_new).astype(BF16)
        acc = jnp.exp2(m - m_new) * acc + _dot(vt_ref[j], p)
        return m_new, acc

    sa_ref[...] = scores(0)

    def pair(jj, carry):
        m, acc = carry
        j0 = 2 * jj
        sb_ref[...] = scores(j0 + 1)
        m, acc = soft_pv(j0, sa_ref[...], m, acc)
        sa_ref[...] = scores(j0 + 2)
        return soft_pv(j0 + 1, sb_ref[...], m, acc)

    init = (jnp.full((1, tq), -jnp.inf, F32), jnp.zeros((V_ROWS_T, tq), F32))
    m, acc = lax.fori_loop(0, qi // 2, pair, init)

    def odd_tail(m, acc):
        sb_ref[...] = scores(qi)
        m, acc = soft_pv(qi - 1, sa_ref[...], m, acc)
        return soft_pv(qi, sb_ref[...], m, acc, masked=True)

    def even_tail(m, acc):
        return soft_pv(qi, sa_ref[...], m, acc, masked=True)

    _, acc = lax.cond(qi % 2 == 1, odd_tail, even_tail, m, acc)
    o_ref[...] = (acc[0:B_V, :] * (1.0 / acc[B_V:B_V + 1, :])).astype(BF16)


def _attn(q_t, k, v_t, tq):
    b, s, _ = k.shape
    return pl.pallas_call(
        functools.partial(_attn_kernel, tq=tq),
        grid=(b, B_HEADS, s // tq),
        in_specs=[
            pl.BlockSpec((None, HEAD_CAT, tq), lambda bi, h, i: (bi, h, i)),
            pl.BlockSpec((None, s, HEAD_CAT), lambda bi, h, i: (bi, 0, h)),
            pl.BlockSpec((None, s // tq, V_ROWS_T, tq), lambda bi, h, i: (bi, 0, h, 0)),
        ],
        out_specs=pl.BlockSpec((None, B_V, tq), lambda bi, h, i: (bi, h, i)),
        out_shape=jax.ShapeDtypeStruct((b, B_HEADS * B_V, s), BF16),
        scratch_shapes=[pltpu.VMEM((tq, tq), F32), pltpu.VMEM((tq, tq), F32)],
        compiler_params=pltpu.CompilerParams(
            dimension_semantics=("arbitrary", "arbitrary", "arbitrary"),
            vmem_limit_bytes=VMEM_LIMIT),
        name="attn",
    )(q_t, k, v_t)


def _swap_halves(w):
    half = w.shape[-1] // 2
    return jnp.concatenate([w[..., half:], w[..., :half]], axis=-1)


def kernel(x, c, positions, ada_w, ada_b, norm_g, a_w_in, a_gate_b, a_head_g, a_w_out,
           kv_ada_w, kv_ada_b, kv_norm_g, kv_w_a, kv_latent_g, kv_w_b, b_w_q_a, b_q_latent_g,
           b_w_q_b, b_w_out, mlp_w1, mlp_w2):
    bsz, seq, d = x.shape
    t = bsz * seq
    x2d = x.reshape(t, d)

    c8 = jnp.pad(c, ((0, 8 - bsz), (0, 0)))
    ada = _adaln(c8, ada_w, ada_b[:, None, :], tn=1536)[:, :bsz]
    kv_ada = _adaln(c8, kv_ada_w[None], kv_ada_b[None, None, :], tn=1024)[0, :bsz]

    def mod_rows(vecs):
        rows = [v.reshape(bsz, 1, d) for v in vecs]
        rows += [jnp.zeros((bsz, 1, d), F32)] * (8 - len(rows))
        return jnp.concatenate(rows, axis=1)

    ada0 = [ada[0][:, i * d:(i + 1) * d] for i in range(6)]
    ada1 = [ada[1][:, i * d:(i + 1) * d] for i in range(6)]
    kv_shift, kv_scale = kv_ada[:, :d], kv_ada[:, d:]

    w_in = a_w_in[0]
    c0 = 2 * A_QK_W + A_V_W
    wk = w_in[:, A_QK_W:2 * A_QK_W].astype(BF16)
    wt = jnp.concatenate([
        w_in[:, :A_QK_W],
        w_in[:, 2 * A_QK_W:c0],
        w_in[:, c0 + 2 * A_HEADS:],
        w_in[:, c0:c0 + 2 * A_HEADS],
    ], axis=1).T.astype(BF16)
    k, qt, vt, ot, gt = _inproj(x2d, norm_g[0, 0][None], mod_rows(ada0[:2]), wk, wt, seq, tm=ROW_TM)
    gate_b = jnp.broadcast_to(a_gate_b[0].reshape(A_GATE_ROWS, 1), (A_GATE_ROWS, LANES))
    head_g = jnp.broadcast_to(a_head_g[0].reshape(A_V_W, 1), (A_V_W, MLSTM_CHUNK))
    bb, r, cm = _gates(gt, gate_b, MLSTM_CHUNK)
    mix0 = _mlstm(k, qt, vt, ot, bb, r, cm, head_g, MLSTM_CHUNK, tb=MLSTM_TB)
    w1_all, w2_all = mlp_w1.astype(BF16), mlp_w2.astype(BF16)
    x2d = _post(x2d, mix0, mod_rows(ada0), norm_g[0], a_w_out[0].astype(BF16), w1_all, w2_all, 0,
                seq, tm=ROW_TM)

    tq = ATTN_TQ
    rope = kv_w_a[:, B_KV_LORA:]
    wa_cat = jnp.concatenate([kv_w_a[:, :B_KV_LORA], rope, _swap_halves(rope)], axis=1).astype(BF16)
    wb = kv_w_b.reshape(B_KV_LORA, B_HEADS, B_NOPE + B_V)
    wbk = wb[:, :, :B_NOPE].reshape(B_KV_LORA, -1).astype(BF16)
    wbv = wb[:, :, B_NOPE:].reshape(B_KV_LORA, -1).T.astype(BF16)
    wq = b_w_q_b[0].T.astype(BF16)
    wq_rope = b_w_q_b[0].reshape(B_Q_LORA, B_HEADS, Q_HEAD)[:, :, B_NOPE:]
    wqs = _swap_halves(wq_rope).reshape(B_Q_LORA, -1).T.astype(BF16)
    inv = ROPE_THETA ** (-jnp.arange(ROPE_HALF, dtype=F32) / ROPE_HALF)
    inv_rep = jnp.broadcast_to(inv[:, None], (ROPE_HALF, ROW_TM))
    ng1 = jnp.concatenate([kv_norm_g[None], norm_g[1, 0][None]], axis=0)
    mod1 = mod_rows([kv_shift, kv_scale, ada1[0], ada1[1]])
    q_t, k_cat, v_t = _l1proj(
        x2d, positions.reshape(bsz, 1, seq), inv_rep, ng1, mod1, wa_cat, kv_latent_g[None], wbk, wbv,
        b_w_q_a[0].astype(BF16), b_q_latent_g[0][None], wq, wqs, seq, tm=ROW_TM, tq=tq)
    o = _attn(q_t, k_cat.reshape(bsz, seq, -1), v_t, tq=tq)
    x2d = _post(x2d, o, mod_rows(ada1), norm_g[1], b_w_out[0].astype(BF16), w1_all, w2_all, 1,
                seq, tm=ROW_TM)
    return x2d.reshape(bsz, seq, d)
```

```python
import functools
import math

import jax
import jax.numpy as jnp
from jax import lax
from jax.experimental import pallas as pl
from jax.experimental.pallas import tpu as pltpu

F32 = jnp.float32
BF16 = jnp.bfloat16

D_MODEL = 1024
D_FF = 4 * D_MODEL
EPS = 1e-6
LOG2E = math.log2(math.e)

A_HEADS = 8
A_QK = 64
A_V = 128
A_QK_W = A_HEADS * A_QK
A_V_W = A_HEADS * A_V
MLSTM_CHUNK = 256
MLSTM_GROUP = 4
MLSTM_TB = 2048

B_HEADS = 8
B_Q_LORA = 384
B_KV_LORA = 256
B_NOPE = 128
B_ROPE = 64
B_V = 128
ROPE_THETA = 10000.0
HEAD_CAT = 256
ATTN_TQ = 1024
ATTN_QSPLIT = 2
ROW_TM = 1024
ROW_SPLIT = 2
POST_SPLIT = 4

LANES = 128
VMEM_LIMIT = 56 * 1024 * 1024


def _dot(a, b):
    return jnp.dot(a, b, preferred_element_type=F32)


def _dot_nt(a, b):
    return lax.dot_general(a, b, (((1,), (1,)), ((), ())), preferred_element_type=F32)


def _dot_tn(a, b):
    return lax.dot_general(a, b, (((0,), (0,)), ((), ())), preferred_element_type=F32)


def _rms(x):
    return x * lax.rsqrt(jnp.mean(x * x, axis=-1, keepdims=True) + EPS)


def _const_spec(shape):
    nd = len(shape)
    return pl.BlockSpec(shape, lambda *_: (0,) * nd, pipeline_mode=pl.Buffered(1))


def _adaln_kernel(c_ref, w_ref, b_ref, o_ref):
    c = c_ref[...]
    cond = c * jax.nn.sigmoid(c)
    o_ref[...] = _dot(cond.astype(BF16), w_ref[...].astype(BF16)) + b_ref[...]


def _adaln(c8, w, b, tn):
    nl, d, n = w.shape
    return pl.pallas_call(
        _adaln_kernel,
        grid=(nl, n // tn),
        in_specs=[
            pl.BlockSpec((8, d), lambda l, j: (0, 0)),
            pl.BlockSpec((None, d, tn), lambda l, j: (l, 0, j)),
            pl.BlockSpec((None, 1, tn), lambda l, j: (l, 0, j)),
        ],
        out_specs=pl.BlockSpec((None, 8, tn), lambda l, j: (l, 0, j)),
        out_shape=jax.ShapeDtypeStruct((nl, 8, n), F32),
        compiler_params=pltpu.CompilerParams(
            dimension_semantics=("arbitrary", "arbitrary"), vmem_limit_bytes=VMEM_LIMIT),
        name="adaln",
    )(c8, w, b)


A_GATE_ROWS = 2 * A_HEADS
WT_Q0, WT_V0, WT_O0, WT_G0 = 0, A_QK_W, A_QK_W + A_V_W, A_QK_W + 2 * A_V_W
WT_ROWS = WT_G0 + A_GATE_ROWS


def _inproj_kernel(x_ref, g_ref, mod_ref, wk_ref, wt_ref, k_ref, qt_ref, vt_ref, ot_ref, gt_ref):
    ts = x_ref.shape[0] // ROW_SPLIT
    for i in range(ROW_SPLIT):
        p = slice(i * ts, (i + 1) * ts)
        h = _rms(x_ref[p, :]) * g_ref[...]
        h = (h * (1.0 + mod_ref[1:2, :]) + mod_ref[0:1, :]).astype(BF16)
        k_ref[p, :] = (_dot(h, wk_ref[...]) * (A_QK ** -0.5)).astype(BF16)
        feat = _dot_nt(wt_ref[...], h)
        qt_ref[:, p] = feat[WT_Q0:WT_V0, :].astype(BF16)
        vt_ref[:, p] = feat[WT_V0:WT_O0, :].astype(BF16)
        ot_ref[:, p] = feat[WT_O0:WT_G0, :].astype(BF16)
        gt_ref[:, p] = feat[WT_G0:WT_ROWS, :]


def _inproj(x, g, mod, wk, wt, seq, tm):
    t, d = x.shape
    bsz = t // seq
    per_b = seq // tm
    row = lambda i: (i, 0)
    col = lambda i: (i // per_b, 0, i % per_b)
    return pl.pallas_call(
        _inproj_kernel,
        grid=(t // tm,),
        in_specs=[
            pl.BlockSpec((tm, d), row),
            _const_spec((1, d)),
            pl.BlockSpec((None, 8, d), lambda i: (i // per_b, 0, 0)),
            _const_spec(wk.shape),
            _const_spec(wt.shape),
        ],
        out_specs=[
            pl.BlockSpec((tm, A_QK_W), row),
            pl.BlockSpec((None, A_QK_W, tm), col),
            pl.BlockSpec((None, A_V_W, tm), col),
            pl.BlockSpec((None, A_V_W, tm), col),
            pl.BlockSpec((None, A_GATE_ROWS, tm), col),
        ],
        out_shape=[
            jax.ShapeDtypeStruct((t, A_QK_W), BF16),
            jax.ShapeDtypeStruct((bsz, A_QK_W, seq), BF16),
            jax.ShapeDtypeStruct((bsz, A_V_W, seq), BF16),
            jax.ShapeDtypeStruct((bsz, A_V_W, seq), BF16),
            jax.ShapeDtypeStruct((bsz, A_GATE_ROWS, seq), F32),
        ],
        compiler_params=pltpu.CompilerParams(
            dimension_semantics=("arbitrary",), vmem_limit_bytes=VMEM_LIMIT),
        name="inproj",
    )(x, g, mod, wk, wt)


ONES_ROWS = 16
CT_ROWS = A_V + ONES_ROWS


def _scan_lanes(x, op, fill, seg):
    pos = lax.broadcasted_iota(jnp.int32, x.shape, 1) % seg
    sh = 1
    while sh < seg:
        x = op(x, jnp.where(pos >= sh, pltpu.roll(x, sh, axis=1), fill))
        sh *= 2
    return x


def _gates_kernel(gt_ref, gb_ref, bb_ref, r_ref, cm_ref, *, chunk):
    H = A_HEADS
    ig = gt_ref[0:H, :] + gb_ref[0:H, 0:1]
    fg = gt_ref[H:2 * H, :] + gb_ref[H:2 * H, 0:1]
    bb = _scan_lanes(jax.nn.log_sigmoid(fg) * LOG2E, jnp.add, 0.0, chunk)
    r = ig * LOG2E - bb
    bb_ref[...] = bb
    r_ref[...] = r
    cm_ref[...] = _scan_lanes(r, jnp.maximum, -jnp.inf, chunk)


def _gates(gt, gate_b, chunk):
    bsz, _, seq = gt.shape
    spec = pl.BlockSpec((None, A_HEADS, seq), lambda b: (b, 0, 0))
    shape = jax.ShapeDtypeStruct((bsz, A_HEADS, seq), F32)
    return pl.pallas_call(
        functools.partial(_gates_kernel, chunk=chunk),
        grid=(bsz,),
        in_specs=[pl.BlockSpec((None, A_GATE_ROWS, seq), lambda b: (b, 0, 0)),
                  pl.BlockSpec(gate_b.shape, lambda b: (0, 0))],
        out_specs=[spec, spec, spec],
        out_shape=[shape, shape, shape],
        compiler_params=pltpu.CompilerParams(
            dimension_semantics=("arbitrary",), vmem_limit_bytes=VMEM_LIMIT),
        name="gates",
    )(gt, gate_b)


def _mlstm_kernel(k_ref, qt_ref, vt_ref, ot_ref, bb_ref, r_ref, cm_ref, hg_ref, out_ref,
                  ct_ref, m_ref, *, chunk, n_chunks):
    L = chunk
    H = A_HEADS

    @pl.when(pl.program_id(1) == 0)
    def _():
        ct_ref[...] = jnp.zeros_like(ct_ref)
        m_ref[...] = jnp.zeros_like(m_ref)

    src = lax.broadcasted_iota(jnp.int32, (L, L), 0)
    tgt = lax.broadcasted_iota(jnp.int32, (L, L), 1)
    causal = src <= tgt
    ones_rows = jnp.ones((ONES_ROWS, L), BF16)

    for c in range(n_chunks):
        cs = slice(c * L, (c + 1) * L)
        bb = bb_ref[:, cs]
        r = r_ref[:, cs]
        m_prev = m_ref[...]
        mm = jnp.maximum(m_prev, cm_ref[:, cs])
        e_inv = jnp.exp2(-(bb + mm))
        w_inter = jnp.exp2(m_prev - mm)
        mm_last = mm[:, L - 1:L]
        w_upd = jnp.exp2(r - mm_last)
        decay = jnp.exp2(m_prev - mm_last)
        m_ref[...] = bb[:, L - 1:L] + mm_last
        r_cols = jnp.concatenate([r, jnp.zeros((L - H, L), F32)], axis=0).T

        G = MLSTM_GROUP
        GW = G * A_QK
        lane_grp = lax.broadcasted_iota(jnp.int32, (L, GW), 1) // A_QK
        row_grp = lax.broadcasted_iota(jnp.int32, (1, GW), 1) // A_QK
        zq = jnp.zeros((A_QK, L), BF16)
        for g in range(H // G):
            hb = g * G
            kg = k_ref[cs, g * GW:(g + 1) * GW]
            q_bd = jnp.concatenate([
                jnp.concatenate([qt_ref[(hb + i) * A_QK:(hb + i + 1) * A_QK, cs] if i == j else zq
                                 for j in range(G)], axis=1) for i in range(G)], axis=0)
            ctg = ct_ref[g]
            st_g = _dot(kg, q_bd)
            inter_g = _dot(ctg.astype(BF16), q_bd)
            pts, vexts = [], []
            for j in range(G):
                h = hb + j
                arg = jnp.where(causal, r_cols[:, h:h + 1] - mm[h:h + 1, :], -jnp.inf)
                pts.append((st_g[:, j * L:(j + 1) * L] * jnp.exp2(arg)).astype(BF16))
                vexts.append(jnp.concatenate([vt_ref[h * A_V:(h + 1) * A_V, cs], ones_rows], axis=0))
            wvs = []
            for j0 in range(0, G, 2):
                for j in (j0, j0 + 1):
                    h = hb + j
                    tot = (w_inter[h:h + 1, :] * inter_g[:, j * L:(j + 1) * L]
                           + _dot(vexts[j], pts[j]))
                    den = tot[A_V:A_V + 1, :]
                    hh = tot[0:A_V, :] * (1.0 / jnp.maximum(jnp.abs(den), e_inv[h:h + 1, :]))
                    ms = jnp.mean(hh * hh, axis=0, keepdims=True)
                    hn = hh * lax.rsqrt(ms + EPS) * hg_ref[h * A_V:(h + 1) * A_V, :]
                    og = ot_ref[h * A_V:(h + 1) * A_V, cs].astype(F32)
                    out_ref[h * A_V:(h + 1) * A_V, cs] = (hn * jax.nn.sigmoid(og)).astype(BF16)
                    wvs.append((vexts[j].astype(F32) * w_upd[h:h + 1, :]).astype(BF16))
            zk = jnp.zeros((L, GW), BF16)
            k_bd = jnp.concatenate([jnp.where(lane_grp == i, kg, zk) for i in range(G)], axis=0)
            decay_g = decay[hb:hb + 1, :]
            for i in range(1, G):
                decay_g = jnp.where(row_grp >= i, decay[hb + i:hb + i + 1, :], decay_g)
            ct_ref[g] = decay_g * ctg + _dot(jnp.concatenate(wvs, axis=1), k_bd)


def _mlstm(k, qt, vt, ot, bb, r, cm, head_g, chunk, tb):
    bsz, _, seq = qt.shape
    per_b = seq // tb
    col = lambda b, i: (b, 0, i)
    gate_spec = pl.BlockSpec((None, A_HEADS, tb), col)
    return pl.pallas_call(
        functools.partial(_mlstm_kernel, chunk=chunk, n_chunks=tb // chunk),
        grid=(bsz, per_b),
        in_specs=[
            pl.BlockSpec((tb, A_QK_W), lambda b, i: (b * per_b + i, 0)),
            pl.BlockSpec((None, A_QK_W, tb), col),
            pl.BlockSpec((None, A_V_W, tb), col),
            pl.BlockSpec((None, A_V_W, tb), col),
            gate_spec, gate_spec, gate_spec,
            pl.BlockSpec(head_g.shape, lambda b, i: (0, 0)),
        ],
        out_specs=pl.BlockSpec((None, A_V_W, tb), col),
        out_shape=jax.ShapeDtypeStruct((bsz, A_V_W, seq), BF16),
        scratch_shapes=[
            pltpu.VMEM((A_HEADS // MLSTM_GROUP, CT_ROWS, MLSTM_GROUP * A_QK), F32),
            pltpu.VMEM((A_HEADS, 1), F32),
        ],
        compiler_params=pltpu.CompilerParams(
            dimension_semantics=("arbitrary", "arbitrary"), vmem_limit_bytes=VMEM_LIMIT),
        name="mlstm",
    )(k, qt, vt, ot, bb, r, cm, head_g)


def _post_kernel(x_ref, mix_ref, mod_ref, ng_ref, wo_ref, w1_ref, w2_ref, out_ref, *, ff_chunk):
    ts = x_ref.shape[0] // POST_SPLIT
    parts = [slice(i * ts, (i + 1) * ts) for i in range(POST_SPLIT)]
    ys = [_dot_tn(mix_ref[:, p], wo_ref[...]) for p in parts]
    for p, y in zip(parts, ys):
        x1 = x_ref[p, :] + mod_ref[2:3, :] * (_rms(y) * ng_ref[1:2, :])
        h = _rms(x1) * ng_ref[2:3, :]
        h = (h * (1.0 + mod_ref[4:5, :]) + mod_ref[3:4, :]).astype(BF16)
        acc = None
        for j in range(D_FF // ff_chunk):
            a = _dot(h, w1_ref[:, j * ff_chunk:(j + 1) * ff_chunk])
            a = jnp.square(jnp.maximum(a, 0.0)).astype(BF16)
            part = _dot(a, w2_ref[j * ff_chunk:(j + 1) * ff_chunk, :])
            acc = part if acc is None else acc + part
        out_ref[p, :] = x1 + mod_ref[5:6, :] * (_rms(acc) * ng_ref[3:4, :])


def _post(x, mix_t, mod, ng, wo, w1, w2, layer, seq, tm):
    t, d = x.shape
    per_b = seq // tm
    row = lambda i: (i, 0)
    layer_spec = lambda w: pl.BlockSpec((None,) + w.shape[1:], lambda i: (layer, 0, 0),
                                        pipeline_mode=pl.Buffered(1))
    return pl.pallas_call(
        functools.partial(_post_kernel, ff_chunk=1024),
        grid=(t // tm,),
        in_specs=[
            pl.BlockSpec((tm, d), row),
            pl.BlockSpec((None, d, tm), lambda i: (i // per_b, 0, i % per_b)),
            pl.BlockSpec((None, 8, d), lambda i: (i // per_b, 0, 0)),
            _const_spec((4, d)),
            _const_spec(wo.shape),
            layer_spec(w1),
            layer_spec(w2),
        ],
        out_specs=pl.BlockSpec((tm, d), row),
        out_shape=jax.ShapeDtypeStruct((t, d), F32),
        compiler_params=pltpu.CompilerParams(
            dimension_semantics=("arbitrary",), vmem_limit_bytes=VMEM_LIMIT),
        name="post",
    )(x, mix_t, mod, ng, wo, w1, w2)


ROPE_HALF = B_ROPE // 2
Q_HEAD = B_NOPE + B_ROPE
V_ROWS_T = B_V + 16
WA_COLS = B_KV_LORA + 2 * B_ROPE


def _l1proj_kernel(x_ref, pos_ref, inv_ref, ng_ref, mod_ref, wa_ref, gl_ref, wbk_ref, wbv_ref,
                   wqa_ref, gq_ref, wq_ref, wqs_ref, q_ref, k_ref, v_ref, *, q_scale):
    tm = x_ref.shape[0]
    xn = _rms(x_ref[...])
    ang = inv_ref[...] * pos_ref[...].astype(F32)
    cos_h = jnp.cos(ang)
    sin_h = jnp.sin(ang)
    cos_t = jnp.concatenate([cos_h, cos_h], axis=0)
    sin_t = jnp.concatenate([-sin_h, sin_h], axis=0)

    hk = xn * ng_ref[0:1, :]
    hk = (hk * (1.0 + mod_ref[1:2, :]) + mod_ref[0:1, :]).astype(BF16)
    kva = _dot(hk, wa_ref[...])
    ckv = (_rms(kva[:, 0:B_KV_LORA]) * gl_ref[...]).astype(BF16)
    kr_t = kva[:, B_KV_LORA:WA_COLS].T
    k_rope_t = kr_t[0:B_ROPE, :] * cos_t + kr_t[B_ROPE:2 * B_ROPE, :] * sin_t
    k_rope = jnp.concatenate([k_rope_t, jnp.zeros((LANES - B_ROPE, tm), F32)], axis=0).T
    k_rope = k_rope.astype(BF16)
    k_nope = _dot(ckv, wbk_ref[...])
    for h in range(B_HEADS):
        k_ref[:, h * HEAD_CAT:h * HEAD_CAT + B_NOPE] = k_nope[:, h * B_NOPE:(h + 1) * B_NOPE].astype(BF16)
        k_ref[:, h * HEAD_CAT + B_NOPE:(h + 1) * HEAD_CAT] = k_rope
    v_t = _dot_nt(wbv_ref[...], ckv)
    ones = jnp.ones((V_ROWS_T - B_V, tm), BF16)
    for h in range(B_HEADS):
        v_ref[h * V_ROWS_T:h * V_ROWS_T + B_V, :] = v_t[h * B_V:(h + 1) * B_V, :].astype(BF16)
        v_ref[h * V_ROWS_T + B_V:(h + 1) * V_ROWS_T, :] = ones

    hq = xn * ng_ref[1:2, :]
    hq = (hq * (1.0 + mod_ref[3:4, :]) + mod_ref[2:3, :]).astype(BF16)
    cq = (_rms(_dot(hq, wqa_ref[...])) * gq_ref[...]).astype(BF16)
    q_t = _dot_nt(wq_ref[...], cq)
    qs_t = _dot_nt(wqs_ref[...], cq)
    zeros = jnp.zeros((HEAD_CAT - Q_HEAD, tm), BF16)
    for h in range(B_HEADS):
        r0 = h * Q_HEAD
        q_ref[h * HEAD_CAT:h * HEAD_CAT + B_NOPE, :] = (q_t[r0:r0 + B_NOPE, :] * q_scale).astype(BF16)
        rope = q_t[r0 + B_NOPE:r0 + Q_HEAD, :] * cos_t + qs_t[h * B_ROPE:(h + 1) * B_ROPE, :] * sin_t
        q_ref[h * HEAD_CAT + B_NOPE:h * HEAD_CAT + Q_HEAD, :] = (rope * q_scale).astype(BF16)
        q_ref[h * HEAD_CAT + Q_HEAD:(h + 1) * HEAD_CAT, :] = zeros


def _l1proj(x, pos, inv, ng, mod, wa, gl, wbk, wbv, wqa, gq, wq, wqs, seq, tm, tq):
    t, d = x.shape
    bsz = t // seq
    per_b = seq // tm
    per_q = tq // tm
    row = lambda i: (i, 0)
    q_scale = float((B_NOPE + B_ROPE) ** -0.5 * math.log2(math.e))
    consts = [inv, ng, None, wa, gl, wbk, wbv, wqa, gq, wq, wqs]
    in_specs = [pl.BlockSpec((tm, d), row),
                pl.BlockSpec((None, 1, tm), lambda i: (i // per_b, 0, i % per_b))]
    for a in consts:
        if a is None:
            in_specs.append(pl.BlockSpec((None, 8, d), lambda i: (i // per_b, 0, 0)))
        else:
            in_specs.append(_const_spec(a.shape))
    return pl.pallas_call(
        functools.partial(_l1proj_kernel, q_scale=q_scale),
        grid=(t // tm,),
        in_specs=in_specs,
        out_specs=[
            pl.BlockSpec((None, B_HEADS * HEAD_CAT, tm), lambda i: (i // per_b, 0, i % per_b)),
            pl.BlockSpec((tm, B_HEADS * HEAD_CAT), row),
            pl.BlockSpec((None, None, B_HEADS * V_ROWS_T, tm),
                         lambda i: (i // per_b, (i % per_b) // per_q, 0, i % per_q)),
        ],
        out_shape=[
            jax.ShapeDtypeStruct((bsz, B_HEADS * HEAD_CAT, seq), BF16),
            jax.ShapeDtypeStruct((t, B_HEADS * HEAD_CAT), BF16),
            jax.ShapeDtypeStruct((bsz, seq // tq, B_HEADS * V_ROWS_T, tq), BF16),
        ],
        compiler_params=pltpu.CompilerParams(
            dimension_semantics=("arbitrary",), vmem_limit_bytes=VMEM_LIMIT),
        name="l1proj",
    )(x, pos, inv, ng, mod, wa, gl, wbk, wbv, wqa, gq, wq, wqs)


def _attn_kernel(qt_ref, k_ref, vt_ref, o_ref, sa_ref, sb_ref, *, tq):
    qi = pl.program_id(2)
    G = ATTN_QSPLIT
    tg = tq // G

    def scores(g, j):
        rows = pl.ds(pl.multiple_of(j * tq, tq), tq)
        return _dot(k_ref[rows, :], qt_ref[:, g * tg:(g + 1) * tg])

    def soft_pv(g, j, s, m, acc, masked=False):
        if masked:
            keys = lax.broadcasted_iota(jnp.int32, (tq, tg), 0)
            qrys = lax.broadcasted_iota(jnp.int32, (tq, tg), 1) + g * tg
            s = jnp.where(keys <= qrys, s, -jnp.inf)
        m_new = jnp.maximum(m, jnp.max(s, axis=0, keepdims=True))
        p = jnp.exp2(s - m_new).astype(BF16)
        acc = jnp.exp2(m - m_new) * acc + _dot(vt_ref[j], p)
        return m_new, acc

    for g in range(G):
        sa_ref[g] = scores(g, 0)

    def pair(jj, carry):
        j0 = 2 * jj
        out = []
        for g in range(G):
            m, acc = carry[g]
            sb_ref[g] = scores(g, j0 + 1)
            m, acc = soft_pv(g, j0, sa_ref[g], m, acc)
            sa_ref[g] = scores(g, j0 + 2)
            out.append(soft_pv(g, j0 + 1, sb_ref[g], m, acc))
        return tuple(out)

    init = tuple((jnp.full((1, tg), -jnp.inf, F32), jnp.zeros((V_ROWS_T, tg), F32)) for _ in range(G))
    carry = lax.fori_loop(0, qi // 2, pair, init)

    def odd_tail(carry):
        out = []
        for g in range(G):
            m, acc = carry[g]
            sb_ref[g] = scores(g, qi)
            m, acc = soft_pv(g, qi - 1, sa_ref[g], m, acc)
            out.append(soft_pv(g, qi, sb_ref[g], m, acc, masked=True))
        return tuple(out)

    def even_tail(carry):
        return tuple(soft_pv(g, qi, sa_ref[g], carry[g][0], carry[g][1], masked=True) for g in range(G))

    carry = lax.cond(qi % 2 == 1, odd_tail, even_tail, carry)
    for g in range(G):
        acc = carry[g][1]
        o_ref[:, g * tg:(g + 1) * tg] = (acc[0:B_V, :] * (1.0 / acc[B_V:B_V + 1, :])).astype(BF16)


def _attn(q_t, k, v_t, tq):
    b, s, _ = k.shape
    g = ATTN_QSPLIT
    return pl.pallas_call(
        functools.partial(_attn_kernel, tq=tq),
        grid=(b, B_HEADS, s // tq),
        in_specs=[
            pl.BlockSpec((None, HEAD_CAT, tq), lambda bi, h, i: (bi, h, i)),
            pl.BlockSpec((None, s, HEAD_CAT), lambda bi, h, i: (bi, 0, h)),
            pl.BlockSpec((None, s // tq, V_ROWS_T, tq), lambda bi, h, i: (bi, 0, h, 0)),
        ],
        out_specs=pl.BlockSpec((None, B_V, tq), lambda bi, h, i: (bi, h, i)),
        out_shape=jax.ShapeDtypeStruct((b, B_HEADS * B_V, s), BF16),
        scratch_shapes=[pltpu.VMEM((g, tq, tq // g), F32), pltpu.VMEM((g, tq, tq // g), F32)],
        compiler_params=pltpu.CompilerParams(
            dimension_semantics=("arbitrary", "arbitrary", "arbitrary"),
            vmem_limit_bytes=VMEM_LIMIT),
        name="attn",
    )(q_t, k, v_t)


def _swap_halves(w):
    half = w.shape[-1] // 2
    return jnp.concatenate([w[..., half:], w[..., :half]], axis=-1)


def kernel(x, c, positions, ada_w, ada_b, norm_g, a_w_in, a_gate_b, a_head_g, a_w_out,
           kv_ada_w, kv_ada_b, kv_norm_g, kv_w_a, kv_latent_g, kv_w_b, b_w_q_a, b_q_latent_g,
           b_w_q_b, b_w_out, mlp_w1, mlp_w2):
    bsz, seq, d = x.shape
    t = bsz * seq
    x2d = x.reshape(t, d)

    c8 = jnp.pad(c, ((0, 8 - bsz), (0, 0)))
    ada = _adaln(c8, ada_w, ada_b[:, None, :], tn=1536)[:, :bsz]
    kv_ada = _adaln(c8, kv_ada_w[None], kv_ada_b[None, None, :], tn=1024)[0, :bsz]

    def mod_rows(vecs):
        rows = [v.reshape(bsz, 1, d) for v in vecs]
        rows += [jnp.zeros((bsz, 1, d), F32)] * (8 - len(rows))
        return jnp.concatenate(rows, axis=1)

    ada0 = [ada[0][:, i * d:(i + 1) * d] for i in range(6)]
    ada1 = [ada[1][:, i * d:(i + 1) * d] for i in range(6)]
    kv_shift, kv_scale = kv_ada[:, :d], kv_ada[:, d:]

    w_in = a_w_in[0]
    c0 = 2 * A_QK_W + A_V_W
    wk = w_in[:, A_QK_W:2 * A_QK_W].astype(BF16)
    wt = jnp.concatenate([
        w_in[:, :A_QK_W],
        w_in[:, 2 * A_QK_W:c0],
        w_in[:, c0 + 2 * A_HEADS:],
        w_in[:, c0:c0 + 2 * A_HEADS],
    ], axis=1).T.astype(BF16)
    k, qt, vt, ot, gt = _inproj(x2d, norm_g[0, 0][None], mod_rows(ada0[:2]), wk, wt, seq, tm=ROW_TM)
    gate_b = jnp.broadcast_to(a_gate_b[0].reshape(A_GATE_ROWS, 1), (A_GATE_ROWS, LANES))
    head_g = jnp.broadcast_to(a_head_g[0].reshape(A_V_W, 1), (A_V_W, MLSTM_CHUNK))
    bb, r, cm = _gates(gt, gate_b, MLSTM_CHUNK)
    mix0 = _mlstm(k, qt, vt, ot, bb, r, cm, head_g, MLSTM_CHUNK, tb=MLSTM_TB)
    w1_all, w2_all = mlp_w1.astype(BF16), mlp_w2.astype(BF16)
    x2d = _post(x2d, mix0, mod_rows(ada0), norm_g[0], a_w_out[0].astype(BF16), w1_all, w2_all, 0,
                seq, tm=ROW_TM)

    tq = ATTN_TQ
    rope = kv_w_a[:, B_KV_LORA:]
    wa_cat = jnp.concatenate([kv_w_a[:, :B_KV_LORA], rope, _swap_halves(rope)], axis=1).astype(BF16)
    wb = kv_w_b.reshape(B_KV_LORA, B_HEADS, B_NOPE + B_V)
    wbk = wb[:, :, :B_NOPE].reshape(B_KV_LORA, -1).astype(BF16)
    wbv = wb[:, :, B_NOPE:].reshape(B_KV_LORA, -1).T.astype(BF16)
    wq = b_w_q_b[0].T.astype(BF16)
    wq_rope = b_w_q_b[0].reshape(B_Q_LORA, B_HEADS, Q_HEAD)[:, :, B_NOPE:]
    wqs = _swap_halves(wq_rope).reshape(B_Q_LORA, -1).T.astype(BF16)
    inv = ROPE_THETA ** (-jnp.arange(ROPE_HALF, dtype=F32) / ROPE_HALF)
    inv_rep = jnp.broadcast_to(inv[:, None], (ROPE_HALF, ROW_TM))
    ng1 = jnp.concatenate([kv_norm_g[None], norm_g[1, 0][None]], axis=0)
    mod1 = mod_rows([kv_shift, kv_scale, ada1[0], ada1[1]])
    q_t, k_cat, v_t = _l1proj(
        x2d, positions.reshape(bsz, 1, seq), inv_rep, ng1, mod1, wa_cat, kv_latent_g[None], wbk, wbv,
        b_w_q_a[0].astype(BF16), b_q_latent_g[0][None], wq, wqs, seq, tm=ROW_TM, tq=tq)
    o = _attn(q_t, k_cat.reshape(bsz, seq, -1), v_t, tq=tq)
    x2d = _post(x2d, o, mod_rows(ada1), norm_g[1], b_w_out[0].astype(BF16), w1_all, w2_all, 1,
                seq, tm=ROW_TM)
    return x2d.reshape(bsz, seq, d)
```

```python
import functools
import math

import jax
import jax.numpy as jnp
from jax import lax
from jax.experimental import pallas as pl
from jax.experimental.pallas import tpu as pltpu

F32 = jnp.float32
BF16 = jnp.bfloat16

D_MODEL = 1024
D_FF = 4 * D_MODEL
EPS = 1e-6
LOG2E = math.log2(math.e)

A_HEADS = 8
A_QK = 64
A_V = 128
A_QK_W = A_HEADS * A_QK
A_V_W = A_HEADS * A_V
MLSTM_CHUNK = 128
MLSTM_GROUP = 4
MLSTM_TB = 1024

B_HEADS = 8
B_Q_LORA = 384
B_KV_LORA = 256
B_NOPE = 128
B_ROPE = 64
B_V = 128
ROPE_THETA = 10000.0
HEAD_CAT = 256
ATTN_TQ = 1024
ROW_TM = 1024
ROW_SPLIT = 2
POST_SPLIT = 4

LANES = 128
VMEM_LIMIT = 56 * 1024 * 1024


def _dot(a, b):
    return jnp.dot(a, b, preferred_element_type=F32)


def _dot_nt(a, b):
    return lax.dot_general(a, b, (((1,), (1,)), ((), ())), preferred_element_type=F32)


def _dot_tn(a, b):
    return lax.dot_general(a, b, (((0,), (0,)), ((), ())), preferred_element_type=F32)


def _rms(x):
    return x * lax.rsqrt(jnp.mean(x * x, axis=-1, keepdims=True) + EPS)


def _const_spec(shape):
    nd = len(shape)
    return pl.BlockSpec(shape, lambda *_: (0,) * nd, pipeline_mode=pl.Buffered(1))


def _adaln_kernel(c_ref, w_ref, b_ref, o_ref):
    c = c_ref[...]
    cond = c * jax.nn.sigmoid(c)
    o_ref[...] = _dot(cond.astype(BF16), w_ref[...].astype(BF16)) + b_ref[...]


def _adaln(c8, w, b, tn):
    nl, d, n = w.shape
    return pl.pallas_call(
        _adaln_kernel,
        grid=(nl, n // tn),
        in_specs=[
            pl.BlockSpec((8, d), lambda l, j: (0, 0)),
            pl.BlockSpec((None, d, tn), lambda l, j: (l, 0, j)),
            pl.BlockSpec((None, 1, tn), lambda l, j: (l, 0, j)),
        ],
        out_specs=pl.BlockSpec((None, 8, tn), lambda l, j: (l, 0, j)),
        out_shape=jax.ShapeDtypeStruct((nl, 8, n), F32),
        compiler_params=pltpu.CompilerParams(
            dimension_semantics=("arbitrary", "arbitrary"), vmem_limit_bytes=VMEM_LIMIT),
        name="adaln",
    )(c8, w, b)


A_GATE_ROWS = 2 * A_HEADS
WT_Q0, WT_V0, WT_O0, WT_G0 = 0, A_QK_W, A_QK_W + A_V_W, A_QK_W + 2 * A_V_W
WT_ROWS = WT_G0 + A_GATE_ROWS


def _inproj_kernel(x_ref, g_ref, mod_ref, wk_ref, wt_ref, k_ref, qt_ref, vt_ref, ot_ref, gt_ref):
    ts = x_ref.shape[0] // ROW_SPLIT
    for i in range(ROW_SPLIT):
        p = slice(i * ts, (i + 1) * ts)
        h = _rms(x_ref[p, :]) * g_ref[...]
        h = (h * (1.0 + mod_ref[1:2, :]) + mod_ref[0:1, :]).astype(BF16)
        k_ref[p, :] = (_dot(h, wk_ref[...]) * (A_QK ** -0.5)).astype(BF16)
        feat = _dot_nt(wt_ref[...], h)
        qt_ref[:, p] = feat[WT_Q0:WT_V0, :].astype(BF16)
        vt_ref[:, p] = feat[WT_V0:WT_O0, :].astype(BF16)
        ot_ref[:, p] = feat[WT_O0:WT_G0, :].astype(BF16)
        gt_ref[:, p] = feat[WT_G0:WT_ROWS, :]


def _inproj(x, g, mod, wk, wt, seq, tm):
    t, d = x.shape
    bsz = t // seq
    per_b = seq // tm
    row = lambda i: (i, 0)
    col = lambda i: (i // per_b, 0, i % per_b)
    return pl.pallas_call(
        _inproj_kernel,
        grid=(t // tm,),
        in_specs=[
            pl.BlockSpec((tm, d), row),
            _const_spec((1, d)),
            pl.BlockSpec((None, 8, d), lambda i: (i // per_b, 0, 0)),
            _const_spec(wk.shape),
            _const_spec(wt.shape),
        ],
        out_specs=[
            pl.BlockSpec((tm, A_QK_W), row),
            pl.BlockSpec((None, A_QK_W, tm), col),
            pl.BlockSpec((None, A_V_W, tm), col),
            pl.BlockSpec((None, A_V_W, tm), col),
            pl.BlockSpec((None, A_GATE_ROWS, tm), col),
        ],
        out_shape=[
            jax.ShapeDtypeStruct((t, A_QK_W), BF16),
            jax.ShapeDtypeStruct((bsz, A_QK_W, seq), BF16),
            jax.ShapeDtypeStruct((bsz, A_V_W, seq), BF16),
            jax.ShapeDtypeStruct((bsz, A_V_W, seq), BF16),
            jax.ShapeDtypeStruct((bsz, A_GATE_ROWS, seq), F32),
        ],
        compiler_params=pltpu.CompilerParams(
            dimension_semantics=("arbitrary",), vmem_limit_bytes=VMEM_LIMIT),
        name="inproj",
    )(x, g, mod, wk, wt)


ONES_ROWS = 16
CT_ROWS = A_V + ONES_ROWS


def _scan_lanes(x, op, fill, seg):
    pos = lax.broadcasted_iota(jnp.int32, x.shape, 1) % seg
    sh = 1
    while sh < seg:
        x = op(x, jnp.where(pos >= sh, pltpu.roll(x, sh, axis=1), fill))
        sh *= 2
    return x


def _gates_kernel(gt_ref, gb_ref, bb_ref, r_ref, cm_ref, *, chunk):
    H = A_HEADS
    ig = gt_ref[0:H, :] + gb_ref[0:H, 0:1]
    fg = gt_ref[H:2 * H, :] + gb_ref[H:2 * H, 0:1]
    bb = _scan_lanes(jax.nn.log_sigmoid(fg) * LOG2E, jnp.add, 0.0, chunk)
    r = ig * LOG2E - bb
    bb_ref[...] = bb
    r_ref[...] = r
    cm_ref[...] = _scan_lanes(r, jnp.maximum, -jnp.inf, chunk)


def _gates(gt, gate_b, chunk):
    bsz, _, seq = gt.shape
    spec = pl.BlockSpec((None, A_HEADS, seq), lambda b: (b, 0, 0))
    shape = jax.ShapeDtypeStruct((bsz, A_HEADS, seq), F32)
    return pl.pallas_call(
        functools.partial(_gates_kernel, chunk=chunk),
        grid=(bsz,),
        in_specs=[pl.BlockSpec((None, A_GATE_ROWS, seq), lambda b: (b, 0, 0)),
                  pl.BlockSpec(gate_b.shape, lambda b: (0, 0))],
        out_specs=[spec, spec, spec],
        out_shape=[shape, shape, shape],
        compiler_params=pltpu.CompilerParams(
            dimension_semantics=("arbitrary",), vmem_limit_bytes=VMEM_LIMIT),
        name="gates",
    )(gt, gate_b)


def _mlstm_kernel(k_ref, qt_ref, vt_ref, ot_ref, bb_ref, r_ref, cm_ref, hg_ref, out_ref,
                  ct_ref, m_ref, *, chunk, n_chunks):
    L = chunk
    H = A_HEADS

    @pl.when(pl.program_id(1) == 0)
    def _():
        ct_ref[...] = jnp.zeros_like(ct_ref)
        m_ref[...] = jnp.zeros_like(m_ref)

    src = lax.broadcasted_iota(jnp.int32, (L, L), 0)
    tgt = lax.broadcasted_iota(jnp.int32, (L, L), 1)
    causal = src <= tgt
    ones_rows = jnp.ones((ONES_ROWS, L), BF16)

    for c in range(n_chunks):
        cs = slice(c * L, (c + 1) * L)
        bb = bb_ref[:, cs]
        r = r_ref[:, cs]
        m_prev = m_ref[...]
        mm = jnp.maximum(m_prev, cm_ref[:, cs])
        e_inv = jnp.exp2(-(bb + mm))
        w_inter = jnp.exp2(m_prev - mm)
        mm_last = mm[:, L - 1:L]
        w_upd = jnp.exp2(r - mm_last)
        decay = jnp.exp2(m_prev - mm_last)
        m_ref[...] = bb[:, L - 1:L] + mm_last
        r_cols = jnp.concatenate([r, jnp.zeros((L - H, L), F32)], axis=0).T

        G = MLSTM_GROUP
        GW = G * A_QK
        lane_grp = lax.broadcasted_iota(jnp.int32, (L, GW), 1) // A_QK
        row_grp = lax.broadcasted_iota(jnp.int32, (1, GW), 1) // A_QK
        zq = jnp.zeros((A_QK, L), BF16)
        for g in range(H // G):
            hb = g * G
            kg = k_ref[cs, g * GW:(g + 1) * GW]
            q_bd = jnp.concatenate([
                jnp.concatenate([qt_ref[(hb + i) * A_QK:(hb + i + 1) * A_QK, cs] if i == j else zq
                                 for j in range(G)], axis=1) for i in range(G)], axis=0)
            ctg = ct_ref[g]
            st_g = _dot(kg, q_bd)
            inter_g = _dot(ctg.astype(BF16), q_bd)
            pts, vexts = [], []
            for j in range(G):
                h = hb + j
                arg = jnp.where(causal, r_cols[:, h:h + 1] - mm[h:h + 1, :], -jnp.inf)
                pts.append((st_g[:, j * L:(j + 1) * L] * jnp.exp2(arg)).astype(BF16))
                vexts.append(jnp.concatenate([vt_ref[h * A_V:(h + 1) * A_V, cs], ones_rows], axis=0))
            wvs = []
            for j0 in range(0, G, 2):
                for j in (j0, j0 + 1):
                    h = hb + j
                    tot = (w_inter[h:h + 1, :] * inter_g[:, j * L:(j + 1) * L]
                           + _dot(vexts[j], pts[j]))
                    den = tot[A_V:A_V + 1, :]
                    hh = tot[0:A_V, :] * (1.0 / jnp.maximum(jnp.abs(den), e_inv[h:h + 1, :]))
                    ms = jnp.mean(hh * hh, axis=0, keepdims=True)
                    hn = hh * lax.rsqrt(ms + EPS) * hg_ref[h * A_V:(h + 1) * A_V, :]
                    og = ot_ref[h * A_V:(h + 1) * A_V, cs].astype(F32)
                    out_ref[h * A_V:(h + 1) * A_V, cs] = (hn * jax.nn.sigmoid(og)).astype(BF16)
                    wvs.append((vexts[j].astype(F32) * w_upd[h:h + 1, :]).astype(BF16))
            zk = jnp.zeros((L, GW), BF16)
            k_bd = jnp.concatenate([jnp.where(lane_grp == i, kg, zk) for i in range(G)], axis=0)
            decay_g = decay[hb:hb + 1, :]
            for i in range(1, G):
                decay_g = jnp.where(row_grp >= i, decay[hb + i:hb + i + 1, :], decay_g)
            ct_ref[g] = decay_g * ctg + _dot(jnp.concatenate(wvs, axis=1), k_bd)


def _mlstm(k, qt, vt, ot, bb, r, cm, head_g, chunk, tb):
    bsz, _, seq = qt.shape
    per_b = seq // tb
    col = lambda b, i: (b, 0, i)
    gate_spec = pl.BlockSpec((None, A_HEADS, tb), col)
    return pl.pallas_call(
        functools.partial(_mlstm_kernel, chunk=chunk, n_chunks=tb // chunk),
        grid=(bsz, per_b),
        in_specs=[
            pl.BlockSpec((tb, A_QK_W), lambda b, i: (b * per_b + i, 0)),
            pl.BlockSpec((None, A_QK_W, tb), col),
            pl.BlockSpec((None, A_V_W, tb), col),
            pl.BlockSpec((None, A_V_W, tb), col),
            gate_spec, gate_spec, gate_spec,
            pl.BlockSpec(head_g.shape, lambda b, i: (0, 0)),
        ],
        out_specs=pl.BlockSpec((None, A_V_W, tb), col),
        out_shape=jax.ShapeDtypeStruct((bsz, A_V_W, seq), BF16),
        scratch_shapes=[
            pltpu.VMEM((A_HEADS // MLSTM_GROUP, CT_ROWS, MLSTM_GROUP * A_QK), F32),
            pltpu.VMEM((A_HEADS, 1), F32),
        ],
        compiler_params=pltpu.CompilerParams(
            dimension_semantics=("arbitrary", "arbitrary"), vmem_limit_bytes=VMEM_LIMIT),
        name="mlstm",
    )(k, qt, vt, ot, bb, r, cm, head_g)


def _post_kernel(x_ref, mix_ref, mod_ref, ng_ref, wo_ref, w1_ref, w2_ref, out_ref, *, ff_chunk):
    ts = x_ref.shape[0] // POST_SPLIT
    parts = [slice(i * ts, (i + 1) * ts) for i in range(POST_SPLIT)]
    ys = [_dot_tn(mix_ref[:, p], wo_ref[...]) for p in parts]
    for p, y in zip(parts, ys):
        x1 = x_ref[p, :] + mod_ref[2:3, :] * (_rms(y) * ng_ref[1:2, :])
        h = _rms(x1) * ng_ref[2:3, :]
        h = (h * (1.0 + mod_ref[4:5, :]) + mod_ref[3:4, :]).astype(BF16)
        acc = None
        for j in range(D_FF // ff_chunk):
            a = _dot(h, w1_ref[:, j * ff_chunk:(j + 1) * ff_chunk])
            a = jnp.square(jnp.maximum(a, 0.0)).astype(BF16)
            part = _dot(a, w2_ref[j * ff_chunk:(j + 1) * ff_chunk, :])
            acc = part if acc is None else acc + part
        out_ref[p, :] = x1 + mod_ref[5:6, :] * (_rms(acc) * ng_ref[3:4, :])


def _post(x, mix_t, mod, ng, wo, w1, w2, layer, seq, tm):
    t, d = x.shape
    per_b = seq // tm
    row = lambda i: (i, 0)
    layer_spec = lambda w: pl.BlockSpec((None,) + w.shape[1:], lambda i: (layer, 0, 0),
                                        pipeline_mode=pl.Buffered(1))
    return pl.pallas_call(
        functools.partial(_post_kernel, ff_chunk=1024),
        grid=(t // tm,),
        in_specs=[
            pl.BlockSpec((tm, d), row),
            pl.BlockSpec((None, d, tm), lambda i: (i // per_b, 0, i % per_b)),
            pl.BlockSpec((None, 8, d), lambda i: (i // per_b, 0, 0)),
            _const_spec((4, d)),
            _const_spec(wo.shape),
            layer_spec(w1),
            layer_spec(w2),
        ],
        out_specs=pl.BlockSpec((tm, d), row),
        out_shape=jax.ShapeDtypeStruct((t, d), F32),
        compiler_params=pltpu.CompilerParams(
            dimension_semantics=("arbitrary",), vmem_limit_bytes=VMEM_LIMIT),
        name="post",
    )(x, mix_t, mod, ng, wo, w1, w2)


ROPE_HALF = B_ROPE // 2
Q_HEAD = B_NOPE + B_ROPE
V_ROWS_T = B_V + 16
WA_COLS = B_KV_LORA + 2 * B_ROPE


def _l1proj_kernel(x_ref, pos_ref, inv_ref, ng_ref, mod_ref, wa_ref, gl_ref, wbk_ref, wbv_ref,
                   wqa_ref, gq_ref, wq_ref, wqs_ref, q_ref, k_ref, v_ref, *, q_scale):
    tm = x_ref.shape[0]
    xn = _rms(x_ref[...])
    ang = inv_ref[...] * pos_ref[...].astype(F32)
    cos_h = jnp.cos(ang)
    sin_h = jnp.sin(ang)
    cos_t = jnp.concatenate([cos_h, cos_h], axis=0)
    sin_t = jnp.concatenate([-sin_h, sin_h], axis=0)

    hk = xn * ng_ref[0:1, :]
    hk = (hk * (1.0 + mod_ref[1:2, :]) + mod_ref[0:1, :]).astype(BF16)
    kva = _dot(hk, wa_ref[...])
    ckv = (_rms(kva[:, 0:B_KV_LORA]) * gl_ref[...]).astype(BF16)
    kr_t = kva[:, B_KV_LORA:WA_COLS].T
    k_rope_t = kr_t[0:B_ROPE, :] * cos_t + kr_t[B_ROPE:2 * B_ROPE, :] * sin_t
    k_rope = jnp.concatenate([k_rope_t, jnp.zeros((LANES - B_ROPE, tm), F32)], axis=0).T
    k_rope = k_rope.astype(BF16)
    k_nope = _dot(ckv, wbk_ref[...])
    for h in range(B_HEADS):
        k_ref[:, h * HEAD_CAT:h * HEAD_CAT + B_NOPE] = k_nope[:, h * B_NOPE:(h + 1) * B_NOPE].astype(BF16)
        k_ref[:, h * HEAD_CAT + B_NOPE:(h + 1) * HEAD_CAT] = k_rope
    v_t = _dot_nt(wbv_ref[...], ckv)
    ones = jnp.ones((V_ROWS_T - B_V, tm), BF16)
    for h in range(B_HEADS):
        v_ref[h * V_ROWS_T:h * V_ROWS_T + B_V, :] = v_t[h * B_V:(h + 1) * B_V, :].astype(BF16)
        v_ref[h * V_ROWS_T + B_V:(h + 1) * V_ROWS_T, :] = ones

    hq = xn * ng_ref[1:2, :]
    hq = (hq * (1.0 + mod_ref[3:4, :]) + mod_ref[2:3, :]).astype(BF16)
    cq = (_rms(_dot(hq, wqa_ref[...])) * gq_ref[...]).astype(BF16)
    q_t = _dot_nt(wq_ref[...], cq)
    qs_t = _dot_nt(wqs_ref[...], cq)
    zeros = jnp.zeros((HEAD_CAT - Q_HEAD, tm), BF16)
    for h in range(B_HEADS):
        r0 = h * Q_HEAD
        q_ref[h * HEAD_CAT:h * HEAD_CAT + B_NOPE, :] = (q_t[r0:r0 + B_NOPE, :] * q_scale).astype(BF16)
        rope = q_t[r0 + B_NOPE:r0 + Q_HEAD, :] * cos_t + qs_t[h * B_ROPE:(h + 1) * B_ROPE, :] * sin_t
        q_ref[h * HEAD_CAT + B_NOPE:h * HEAD_CAT + Q_HEAD, :] = (rope * q_scale).astype(BF16)
        q_ref[h * HEAD_CAT + Q_HEAD:(h + 1) * HEAD_CAT, :] = zeros


def _l1proj(x, pos, inv, ng, mod, wa, gl, wbk, wbv, wqa, gq, wq, wqs, seq, tm, tq):
    t, d = x.shape
    bsz = t // seq
    per_b = seq // tm
    per_q = tq // tm
    row = lambda i: (i, 0)
    q_scale = float((B_NOPE + B_ROPE) ** -0.5 * math.log2(math.e))
    consts = [inv, ng, None, wa, gl, wbk, wbv, wqa, gq, wq, wqs]
    in_specs = [pl.BlockSpec((tm, d), row),
                pl.BlockSpec((None, 1, tm), lambda i: (i // per_b, 0, i % per_b))]
    for a in consts:
        if a is None:
            in_specs.append(pl.BlockSpec((None, 8, d), lambda i: (i // per_b, 0, 0)))
        else:
            in_specs.append(_const_spec(a.shape))
    return pl.pallas_call(
        functools.partial(_l1proj_kernel, q_scale=q_scale),
        grid=(t // tm,),
        in_specs=in_specs,
        out_specs=[
            pl.BlockSpec((None, B_HEADS * HEAD_CAT, tm), lambda i: (i // per_b, 0, i % per_b)),
            pl.BlockSpec((tm, B_HEADS * HEAD_CAT), row),
            pl.BlockSpec((None, None, B_HEADS * V_ROWS_T, tm),
                         lambda i: (i // per_b, (i % per_b) // per_q, 0, i % per_q)),
        ],
        out_shape=[
            jax.ShapeDtypeStruct((bsz, B_HEADS * HEAD_CAT, seq), BF16),
            jax.ShapeDtypeStruct((t, B_HEADS * HEAD_CAT), BF16),
            jax.ShapeDtypeStruct((bsz, seq // tq, B_HEADS * V_ROWS_T, tq), BF16),
        ],
        compiler_params=pltpu.CompilerParams(
            dimension_semantics=("arbitrary",), vmem_limit_bytes=VMEM_LIMIT),
        name="l1proj",
    )(x, pos, inv, ng, mod, wa, gl, wbk, wbv, wqa, gq, wq, wqs)


def _attn_kernel(qt_ref, k_ref, vt_ref, o_ref, sa_ref, sb_ref, *, tq):
    qi = pl.program_id(2)

    def scores(j):
        rows = pl.ds(pl.multiple_of(j * tq, tq), tq)
        return _dot(k_ref[rows, :], qt_ref[...])

    def soft_pv(j, s, m, acc, masked=False):
        if masked:
            keys = lax.broadcasted_iota(jnp.int32, (tq, tq), 0)
            qrys = lax.broadcasted_iota(jnp.int32, (tq, tq), 1)
            s = jnp.where(keys <= qrys, s, -jnp.inf)
        m_new = jnp.maximum(m, jnp.max(s, axis=0, keepdims=True))
        p = jnp.exp2(s - m_new).astype(BF16)
        acc = jnp.exp2(m - m_new) * acc + _dot(vt_ref[j], p)
        return m_new, acc

    sa_ref[...] = scores(0)

    def pair(jj, carry):
        m, acc = carry
        j0 = 2 * jj
        sb_ref[...] = scores(j0 + 1)
        m, acc = soft_pv(j0, sa_ref[...], m, acc)
        sa_ref[...] = scores(j0 + 2)
        return soft_pv(j0 + 1, sb_ref[...], m, acc)

    init = (jnp.full((1, tq), -jnp.inf, F32), jnp.zeros((V_ROWS_T, tq), F32))
    m, acc = lax.fori_loop(0, qi // 2, pair, init)

    def odd_tail(m, acc):
        sb_ref[...] = scores(qi)
        m, acc = soft_pv(qi - 1, sa_ref[...], m, acc)
        return soft_pv(qi, sb_ref[...], m, acc, masked=True)

    def even_tail(m, acc):
        return soft_pv(qi, sa_ref[...], m, acc, masked=True)

    _, acc = lax.cond(qi % 2 == 1, odd_tail, even_tail, m, acc)
    o_ref[...] = (acc[0:B_V, :] * (1.0 / acc[B_V:B_V + 1, :])).astype(BF16)


def _attn(q_t, k, v_t, tq):
    b, s, _ = k.shape
    return pl.pallas_call(
        functools.partial(_attn_kernel, tq=tq),
        grid=(b, B_HEADS, s // tq),
        in_specs=[
            pl.BlockSpec((None, HEAD_CAT, tq), lambda bi, h, i: (bi, h, i)),
            pl.BlockSpec((None, s, HEAD_CAT), lambda bi, h, i: (bi, 0, h)),
            pl.BlockSpec((None, s // tq, V_ROWS_T, tq), lambda bi, h, i: (bi, 0, h, 0)),
        ],
        out_specs=pl.BlockSpec((None, B_V, tq), lambda bi, h, i: (bi, h, i)),
        out_shape=jax.ShapeDtypeStruct((b, B_HEADS * B_V, s), BF16),
        scratch_shapes=[pltpu.VMEM((tq, tq), F32), pltpu.VMEM((tq, tq), F32)],
        compiler_params=pltpu.CompilerParams(
            dimension_semantics=("arbitrary", "arbitrary", "arbitrary"),
            vmem_limit_bytes=VMEM_LIMIT),
        name="attn",
    )(q_t, k, v_t)


def _swap_halves(w):
    half = w.shape[-1] // 2
    return jnp.concatenate([w[..., half:], w[..., :half]], axis=-1)


def kernel(x, c, positions, ada_w, ada_b, norm_g, a_w_in, a_gate_b, a_head_g, a_w_out,
           kv_ada_w, kv_ada_b, kv_norm_g, kv_w_a, kv_latent_g, kv_w_b, b_w_q_a, b_q_latent_g,
           b_w_q_b, b_w_out, mlp_w1, mlp_w2):
    bsz, seq, d = x.shape
    t = bsz * seq
    x2d = x.reshape(t, d)

    c8 = jnp.pad(c, ((0, 8 - bsz), (0, 0)))
    ada = _adaln(c8, ada_w, ada_b[:, None, :], tn=1536)[:, :bsz]
    kv_ada = _adaln(c8, kv_ada_w[None], kv_ada_b[None, None, :], tn=1024)[0, :bsz]

    def mod_rows(vecs):
        rows = [v.reshape(bsz, 1, d) for v in vecs]
        rows += [jnp.zeros((bsz, 1, d), F32)] * (8 - len(rows))
        return jnp.concatenate(rows, axis=1)

    ada0 = [ada[0][:, i * d:(i + 1) * d] for i in range(6)]
    ada1 = [ada[1][:, i * d:(i + 1) * d] for i in range(6)]
    kv_shift, kv_scale = kv_ada[:, :d], kv_ada[:, d:]

    w_in = a_w_in[0]
    c0 = 2 * A_QK_W + A_V_W
    wk = w_in[:, A_QK_W:2 * A_QK_W].astype(BF16)
    wt = jnp.concatenate([
        w_in[:, :A_QK_W],
        w_in[:, 2 * A_QK_W:c0],
        w_in[:, c0 + 2 * A_HEADS:],
        w_in[:, c0:c0 + 2 * A_HEADS],
    ], axis=1).T.astype(BF16)
    k, qt, vt, ot, gt = _inproj(x2d, norm_g[0, 0][None], mod_rows(ada0[:2]), wk, wt, seq, tm=ROW_TM)
    gate_b = jnp.broadcast_to(a_gate_b[0].reshape(A_GATE_ROWS, 1), (A_GATE_ROWS, LANES))
    head_g = jnp.broadcast_to(a_head_g[0].reshape(A_V_W, 1), (A_V_W, MLSTM_CHUNK))
    bb, r, cm = _gates(gt, gate_b, MLSTM_CHUNK)
    mix0 = _mlstm(k, qt, vt, ot, bb, r, cm, head_g, MLSTM_CHUNK, tb=MLSTM_TB)
    w1_all, w2_all = mlp_w1.astype(BF16), mlp_w2.astype(BF16)
    x2d = _post(x2d, mix0, mod_rows(ada0), norm_g[0], a_w_out[0].astype(BF16), w1_all, w2_all, 0,
                seq, tm=ROW_TM)

    tq = ATTN_TQ
    rope = kv_w_a[:, B_KV_LORA:]
    wa_cat = jnp.concatenate([kv_w_a[:, :B_KV_LORA], rope, _swap_halves(rope)], axis=1).astype(BF16)
    wb = kv_w_b.reshape(B_KV_LORA, B_HEADS, B_NOPE + B_V)
    wbk = wb[:, :, :B_NOPE].reshape(B_KV_LORA, -1).astype(BF16)
    wbv = wb[:, :, B_NOPE:].reshape(B_KV_LORA, -1).T.astype(BF16)
    wq = b_w_q_b[0].T.astype(BF16)
    wq_rope = b_w_q_b[0].reshape(B_Q_LORA, B_HEADS, Q_HEAD)[:, :, B_NOPE:]
    wqs = _swap_halves(wq_rope).reshape(B_Q_LORA, -1).T.astype(BF16)
    inv = ROPE_THETA ** (-jnp.arange(ROPE_HALF, dtype=F32) / ROPE_HALF)
    inv_rep = jnp.broadcast_to(inv[:, None], (ROPE_HALF, ROW_TM))
    ng1 = jnp.concatenate([kv_norm_g[None], norm_g[1, 0][None]], axis=0)
    mod1 = mod_rows([kv_shift, kv_scale, ada1[0], ada1[1]])
    q_t, k_cat, v_t = _l1proj(
        x2d, positions.reshape(bsz, 1, seq), inv_rep, ng1, mod1, wa_cat, kv_latent_g[None], wbk, wbv,
        b_w_q_a[0].astype(BF16), b_q_latent_g[0][None], wq, wqs, seq, tm=ROW_TM, tq=tq)
    o = _attn(q_t, k_cat.reshape(bsz, seq, -1), v_t, tq=tq)
    x2d = _post(x2d, o, mod_rows(ada1), norm_g[1], b_w_out[0].astype(BF16), w1_all, w2_all, 1,
                seq, tm=ROW_TM)
    return x2d.reshape(bsz, seq, d)
```

```python
import functools
import math

import jax
import jax.numpy as jnp
from jax import lax
from jax.experimental import pallas as pl
from jax.experimental.pallas import tpu as pltpu

F32 = jnp.float32
BF16 = jnp.bfloat16

D_MODEL = 1024
D_FF = 4 * D_MODEL
EPS = 1e-6
LOG2E = math.log2(math.e)

A_HEADS = 8
A_QK = 64
A_V = 128
A_QK_W = A_HEADS * A_QK
A_V_W = A_HEADS * A_V
MLSTM_CHUNK = 256
MLSTM_GROUP = 4
MLSTM_TB = 2048

B_HEADS = 8
B_Q_LORA = 384
B_KV_LORA = 256
B_NOPE = 128
B_ROPE = 64
B_V = 128
ROPE_THETA = 10000.0
HEAD_CAT = 256
ATTN_TQ = 1024
ROW_TM = 1024
ROW_SPLIT = 2
POST_SPLIT = 4

LANES = 128
VMEM_LIMIT = 56 * 1024 * 1024


def _dot(a, b):
    return jnp.dot(a, b, preferred_element_type=F32)


def _dot_nt(a, b):
    return lax.dot_general(a, b, (((1,), (1,)), ((), ())), preferred_element_type=F32)


def _dot_tn(a, b):
    return lax.dot_general(a, b, (((0,), (0,)), ((), ())), preferred_element_type=F32)


def _rms(x):
    return x * lax.rsqrt(jnp.mean(x * x, axis=-1, keepdims=True) + EPS)


def _const_spec(shape):
    nd = len(shape)
    return pl.BlockSpec(shape, lambda *_: (0,) * nd, pipeline_mode=pl.Buffered(1))


def _adaln_kernel(c_ref, w_ref, b_ref, o_ref):
    c = c_ref[...]
    cond = c * jax.nn.sigmoid(c)
    o_ref[...] = _dot(cond.astype(BF16), w_ref[...].astype(BF16)) + b_ref[...]


def _adaln(c8, w, b, tn):
    nl, d, n = w.shape
    return pl.pallas_call(
        _adaln_kernel,
        grid=(nl, n // tn),
        in_specs=[
            pl.BlockSpec((8, d), lambda l, j: (0, 0)),
            pl.BlockSpec((None, d, tn), lambda l, j: (l, 0, j)),
            pl.BlockSpec((None, 1, tn), lambda l, j: (l, 0, j)),
        ],
        out_specs=pl.BlockSpec((None, 8, tn), lambda l, j: (l, 0, j)),
        out_shape=jax.ShapeDtypeStruct((nl, 8, n), F32),
        compiler_params=pltpu.CompilerParams(
            dimension_semantics=("arbitrary", "arbitrary"), vmem_limit_bytes=VMEM_LIMIT),
        name="adaln",
    )(c8, w, b)


A_GATE_ROWS = 2 * A_HEADS
WT_Q0, WT_V0, WT_O0, WT_G0 = 0, A_QK_W, A_QK_W + A_V_W, A_QK_W + 2 * A_V_W
WT_ROWS = WT_G0 + A_GATE_ROWS


def _inproj_kernel(x_ref, g_ref, mod_ref, wk_ref, wt_ref, k_ref, qt_ref, vt_ref, ot_ref, gt_ref):
    ts = x_ref.shape[0] // ROW_SPLIT
    for i in range(ROW_SPLIT):
        p = slice(i * ts, (i + 1) * ts)
        h = _rms(x_ref[p, :]) * g_ref[...]
        h = (h * (1.0 + mod_ref[1:2, :]) + mod_ref[0:1, :]).astype(BF16)
        k_ref[p, :] = (_dot(h, wk_ref[...]) * (A_QK ** -0.5)).astype(BF16)
        feat = _dot_nt(wt_ref[...], h)
        qt_ref[:, p] = feat[WT_Q0:WT_V0, :].astype(BF16)
        vt_ref[:, p] = feat[WT_V0:WT_O0, :].astype(BF16)
        ot_ref[:, p] = feat[WT_O0:WT_G0, :].astype(BF16)
        gt_ref[:, p] = feat[WT_G0:WT_ROWS, :]


def _inproj(x, g, mod, wk, wt, seq, tm):
    t, d = x.shape
    bsz = t // seq
    per_b = seq // tm
    row = lambda i: (i, 0)
    col = lambda i: (i // per_b, 0, i % per_b)
    return pl.pallas_call(
        _inproj_kernel,
        grid=(t // tm,),
        in_specs=[
            pl.BlockSpec((tm, d), row),
            _const_spec((1, d)),
            pl.BlockSpec((None, 8, d), lambda i: (i // per_b, 0, 0)),
            _const_spec(wk.shape),
            _const_spec(wt.shape),
        ],
        out_specs=[
            pl.BlockSpec((tm, A_QK_W), row),
            pl.BlockSpec((None, A_QK_W, tm), col),
            pl.BlockSpec((None, A_V_W, tm), col),
            pl.BlockSpec((None, A_V_W, tm), col),
            pl.BlockSpec((None, A_GATE_ROWS, tm), col),
        ],
        out_shape=[
            jax.ShapeDtypeStruct((t, A_QK_W), BF16),
            jax.ShapeDtypeStruct((bsz, A_QK_W, seq), BF16),
            jax.ShapeDtypeStruct((bsz, A_V_W, seq), BF16),
            jax.ShapeDtypeStruct((bsz, A_V_W, seq), BF16),
            jax.ShapeDtypeStruct((bsz, A_GATE_ROWS, seq), F32),
        ],
        compiler_params=pltpu.CompilerParams(
            dimension_semantics=("parallel",), vmem_limit_bytes=VMEM_LIMIT),
        name="inproj",
    )(x, g, mod, wk, wt)


ONES_ROWS = 16
CT_ROWS = A_V + ONES_ROWS


def _scan_lanes(x, op, fill, seg):
    pos = lax.broadcasted_iota(jnp.int32, x.shape, 1) % seg
    sh = 1
    while sh < seg:
        x = op(x, jnp.where(pos >= sh, pltpu.roll(x, sh, axis=1), fill))
        sh *= 2
    return x


def _gates_kernel(gt_ref, gb_ref, bb_ref, r_ref, cm_ref, *, chunk):
    H = A_HEADS
    ig = gt_ref[0:H, :] + gb_ref[0:H, 0:1]
    fg = gt_ref[H:2 * H, :] + gb_ref[H:2 * H, 0:1]
    bb = _scan_lanes(jax.nn.log_sigmoid(fg) * LOG2E, jnp.add, 0.0, chunk)
    r = ig * LOG2E - bb
    bb_ref[...] = bb
    r_ref[...] = r
    cm_ref[...] = _scan_lanes(r, jnp.maximum, -jnp.inf, chunk)


def _gates(gt, gate_b, chunk):
    bsz, _, seq = gt.shape
    spec = pl.BlockSpec((None, A_HEADS, seq), lambda b: (b, 0, 0))
    shape = jax.ShapeDtypeStruct((bsz, A_HEADS, seq), F32)
    return pl.pallas_call(
        functools.partial(_gates_kernel, chunk=chunk),
        grid=(bsz,),
        in_specs=[pl.BlockSpec((None, A_GATE_ROWS, seq), lambda b: (b, 0, 0)),
                  pl.BlockSpec(gate_b.shape, lambda b: (0, 0))],
        out_specs=[spec, spec, spec],
        out_shape=[shape, shape, shape],
        compiler_params=pltpu.CompilerParams(
            dimension_semantics=("parallel",), vmem_limit_bytes=VMEM_LIMIT),
        name="gates",
    )(gt, gate_b)


def _mlstm_kernel(k_ref, qt_ref, vt_ref, ot_ref, bb_ref, r_ref, cm_ref, hg_ref, out_ref,
                  ct_ref, m_ref, *, chunk, n_chunks):
    L = chunk
    H = A_HEADS

    @pl.when(pl.program_id(1) == 0)
    def _():
        ct_ref[...] = jnp.zeros_like(ct_ref)
        m_ref[...] = jnp.zeros_like(m_ref)

    src = lax.broadcasted_iota(jnp.int32, (L, L), 0)
    tgt = lax.broadcasted_iota(jnp.int32, (L, L), 1)
    causal = src <= tgt
    ones_rows = jnp.ones((ONES_ROWS, L), BF16)

    for c in range(n_chunks):
        cs = slice(c * L, (c + 1) * L)
        bb = bb_ref[:, cs]
        r = r_ref[:, cs]
        m_prev = m_ref[...]
        mm = jnp.maximum(m_prev, cm_ref[:, cs])
        e_inv = jnp.exp2(-(bb + mm))
        w_inter = jnp.exp2(m_prev - mm)
        mm_last = mm[:, L - 1:L]
        w_upd = jnp.exp2(r - mm_last)
        decay = jnp.exp2(m_prev - mm_last)
        m_ref[...] = bb[:, L - 1:L] + mm_last
        r_cols = jnp.concatenate([r, jnp.zeros((L - H, L), F32)], axis=0).T

        G = MLSTM_GROUP
        GW = G * A_QK
        lane_grp = lax.broadcasted_iota(jnp.int32, (L, GW), 1) // A_QK
        row_grp = lax.broadcasted_iota(jnp.int32, (1, GW), 1) // A_QK
        zq = jnp.zeros((A_QK, L), BF16)
        for g in range(H // G):
            hb = g * G
            kg = k_ref[cs, g * GW:(g + 1) * GW]
            q_bd = jnp.concatenate([
                jnp.concatenate([qt_ref[(hb + i) * A_QK:(hb + i + 1) * A_QK, cs] if i == j else zq
                                 for j in range(G)], axis=1) for i in range(G)], axis=0)
            ctg = ct_ref[g]
            st_g = _dot(kg, q_bd)
            inter_g = _dot(ctg.astype(BF16), q_bd)
            pts, vexts = [], []
            for j in range(G):
                h = hb + j
                arg = jnp.where(causal, r_cols[:, h:h + 1] - mm[h:h + 1, :], -jnp.inf)
                pts.append((st_g[:, j * L:(j + 1) * L] * jnp.exp2(arg)).astype(BF16))
                vexts.append(jnp.concatenate([vt_ref[h * A_V:(h + 1) * A_V, cs], ones_rows], axis=0))
            wvs = []
            for j0 in range(0, G, 2):
                for j in (j0, j0 + 1):
                    h = hb + j
                    tot = (w_inter[h:h + 1, :] * inter_g[:, j * L:(j + 1) * L]
                           + _dot(vexts[j], pts[j]))
                    den = tot[A_V:A_V + 1, :]
                    hh = tot[0:A_V, :] * (1.0 / jnp.maximum(jnp.abs(den), e_inv[h:h + 1, :]))
                    ms = jnp.mean(hh * hh, axis=0, keepdims=True)
                    hn = hh * lax.rsqrt(ms + EPS) * hg_ref[h * A_V:(h + 1) * A_V, :]
                    og = ot_ref[h * A_V:(h + 1) * A_V, cs].astype(F32)
                    out_ref[h * A_V:(h + 1) * A_V, cs] = (hn * jax.nn.sigmoid(og)).astype(BF16)
                    wvs.append((vexts[j].astype(F32) * w_upd[h:h + 1, :]).astype(BF16))
            zk = jnp.zeros((L, GW), BF16)
            k_bd = jnp.concatenate([jnp.where(lane_grp == i, kg, zk) for i in range(G)], axis=0)
            decay_g = decay[hb:hb + 1, :]
            for i in range(1, G):
                decay_g = jnp.where(row_grp >= i, decay[hb + i:hb + i + 1, :], decay_g)
            ct_ref[g] = decay_g * ctg + _dot(jnp.concatenate(wvs, axis=1), k_bd)


def _mlstm(k, qt, vt, ot, bb, r, cm, head_g, chunk, tb):
    bsz, _, seq = qt.shape
    per_b = seq // tb
    col = lambda b, i: (b, 0, i)
    gate_spec = pl.BlockSpec((None, A_HEADS, tb), col)
    return pl.pallas_call(
        functools.partial(_mlstm_kernel, chunk=chunk, n_chunks=tb // chunk),
        grid=(bsz, per_b),
        in_specs=[
            pl.BlockSpec((tb, A_QK_W), lambda b, i: (b * per_b + i, 0)),
            pl.BlockSpec((None, A_QK_W, tb), col),
            pl.BlockSpec((None, A_V_W, tb), col),
            pl.BlockSpec((None, A_V_W, tb), col),
            gate_spec, gate_spec, gate_spec,
            pl.BlockSpec(head_g.shape, lambda b, i: (0, 0)),
        ],
        out_specs=pl.BlockSpec((None, A_V_W, tb), col),
        out_shape=jax.ShapeDtypeStruct((bsz, A_V_W, seq), BF16),
        scratch_shapes=[
            pltpu.VMEM((A_HEADS // MLSTM_GROUP, CT_ROWS, MLSTM_GROUP * A_QK), F32),
            pltpu.VMEM((A_HEADS, 1), F32),
        ],
        compiler_params=pltpu.CompilerParams(
            dimension_semantics=("arbitrary", "arbitrary"), vmem_limit_bytes=VMEM_LIMIT),
        name="mlstm",
    )(k, qt, vt, ot, bb, r, cm, head_g)


def _post_kernel(x_ref, mix_ref, mod_ref, ng_ref, wo_ref, w1_ref, w2_ref, out_ref, *, ff_chunk):
    ts = x_ref.shape[0] // POST_SPLIT
    parts = [slice(i * ts, (i + 1) * ts) for i in range(POST_SPLIT)]
    ys = [_dot_tn(mix_ref[:, p], wo_ref[...]) for p in parts]
    for p, y in zip(parts, ys):
        x1 = x_ref[p, :] + mod_ref[2:3, :] * (_rms(y) * ng_ref[1:2, :])
        h = _rms(x1) * ng_ref[2:3, :]
        h = (h * (1.0 + mod_ref[4:5, :]) + mod_ref[3:4, :]).astype(BF16)
        acc = None
        for j in range(D_FF // ff_chunk):
            a = _dot(h, w1_ref[:, j * ff_chunk:(j + 1) * ff_chunk])
            a = jnp.square(jnp.maximum(a, 0.0)).astype(BF16)
            part = _dot(a, w2_ref[j * ff_chunk:(j + 1) * ff_chunk, :])
            acc = part if acc is None else acc + part
        out_ref[p, :] = x1 + mod_ref[5:6, :] * (_rms(acc) * ng_ref[3:4, :])


def _post(x, mix_t, mod, ng, wo, w1, w2, layer, seq, tm):
    t, d = x.shape
    per_b = seq // tm
    row = lambda i: (i, 0)
    layer_spec = lambda w: pl.BlockSpec((None,) + w.shape[1:], lambda i: (layer, 0, 0),
                                        pipeline_mode=pl.Buffered(1))
    return pl.pallas_call(
        functools.partial(_post_kernel, ff_chunk=1024),
        grid=(t // tm,),
        in_specs=[
            pl.BlockSpec((tm, d), row),
            pl.BlockSpec((None, d, tm), lambda i: (i // per_b, 0, i % per_b)),
            pl.BlockSpec((None, 8, d), lambda i: (i // per_b, 0, 0)),
            _const_spec((4, d)),
            _const_spec(wo.shape),
            layer_spec(w1),
            layer_spec(w2),
        ],
        out_specs=pl.BlockSpec((tm, d), row),
        out_shape=jax.ShapeDtypeStruct((t, d), F32),
        compiler_params=pltpu.CompilerParams(
            dimension_semantics=("parallel",), vmem_limit_bytes=VMEM_LIMIT),
        name="post",
    )(x, mix_t, mod, ng, wo, w1, w2)


ROPE_HALF = B_ROPE // 2
Q_HEAD = B_NOPE + B_ROPE
V_ROWS_T = B_V + 16
WA_COLS = B_KV_LORA + 2 * B_ROPE


def _l1proj_kernel(x_ref, pos_ref, inv_ref, ng_ref, mod_ref, wa_ref, gl_ref, wbk_ref, wbv_ref,
                   wqa_ref, gq_ref, wq_ref, wqs_ref, q_ref, k_ref, v_ref, *, q_scale):
    tm = x_ref.shape[0]
    xn = _rms(x_ref[...])
    ang = inv_ref[...] * pos_ref[...].astype(F32)
    cos_h = jnp.cos(ang)
    sin_h = jnp.sin(ang)
    cos_t = jnp.concatenate([cos_h, cos_h], axis=0)
    sin_t = jnp.concatenate([-sin_h, sin_h], axis=0)

    hk = xn * ng_ref[0:1, :]
    hk = (hk * (1.0 + mod_ref[1:2, :]) + mod_ref[0:1, :]).astype(BF16)
    kva = _dot(hk, wa_ref[...])
    ckv = (_rms(kva[:, 0:B_KV_LORA]) * gl_ref[...]).astype(BF16)
    kr_t = kva[:, B_KV_LORA:WA_COLS].T
    k_rope_t = kr_t[0:B_ROPE, :] * cos_t + kr_t[B_ROPE:2 * B_ROPE, :] * sin_t
    k_rope = jnp.concatenate([k_rope_t, jnp.zeros((LANES - B_ROPE, tm), F32)], axis=0).T
    k_rope = k_rope.astype(BF16)
    k_nope = _dot(ckv, wbk_ref[...])
    for h in range(B_HEADS):
        k_ref[:, h * HEAD_CAT:h * HEAD_CAT + B_NOPE] = k_nope[:, h * B_NOPE:(h + 1) * B_NOPE].astype(BF16)
        k_ref[:, h * HEAD_CAT + B_NOPE:(h + 1) * HEAD_CAT] = k_rope
    v_t = _dot_nt(wbv_ref[...], ckv)
    ones = jnp.ones((V_ROWS_T - B_V, tm), BF16)
    for h in range(B_HEADS):
        v_ref[h * V_ROWS_T:h * V_ROWS_T + B_V, :] = v_t[h * B_V:(h + 1) * B_V, :].astype(BF16)
        v_ref[h * V_ROWS_T + B_V:(h + 1) * V_ROWS_T, :] = ones

    hq = xn * ng_ref[1:2, :]
    hq = (hq * (1.0 + mod_ref[3:4, :]) + mod_ref[2:3, :]).astype(BF16)
    cq = (_rms(_dot(hq, wqa_ref[...])) * gq_ref[...]).astype(BF16)
    q_t = _dot_nt(wq_ref[...], cq)
    qs_t = _dot_nt(wqs_ref[...], cq)
    zeros = jnp.zeros((HEAD_CAT - Q_HEAD, tm), BF16)
    for h in range(B_HEADS):
        r0 = h * Q_HEAD
        q_ref[h * HEAD_CAT:h * HEAD_CAT + B_NOPE, :] = (q_t[r0:r0 + B_NOPE, :] * q_scale).astype(BF16)
        rope = q_t[r0 + B_NOPE:r0 + Q_HEAD, :] * cos_t + qs_t[h * B_ROPE:(h + 1) * B_ROPE, :] * sin_t
        q_ref[h * HEAD_CAT + B_NOPE:h * HEAD_CAT + Q_HEAD, :] = (rope * q_scale).astype(BF16)
        q_ref[h * HEAD_CAT + Q_HEAD:(h + 1) * HEAD_CAT, :] = zeros


def _l1proj(x, pos, inv, ng, mod, wa, gl, wbk, wbv, wqa, gq, wq, wqs, seq, tm, tq):
    t, d = x.shape
    bsz = t // seq
    per_b = seq // tm
    per_q = tq // tm
    row = lambda i: (i, 0)
    q_scale = float((B_NOPE + B_ROPE) ** -0.5 * math.log2(math.e))
    consts = [inv, ng, None, wa, gl, wbk, wbv, wqa, gq, wq, wqs]
    in_specs = [pl.BlockSpec((tm, d), row),
                pl.BlockSpec((None, 1, tm), lambda i: (i // per_b, 0, i % per_b))]
    for a in consts:
        if a is None:
            in_specs.append(pl.BlockSpec((None, 8, d), lambda i: (i // per_b, 0, 0)))
        else:
            in_specs.append(_const_spec(a.shape))
    return pl.pallas_call(
        functools.partial(_l1proj_kernel, q_scale=q_scale),
        grid=(t // tm,),
        in_specs=in_specs,
        out_specs=[
            pl.BlockSpec((None, B_HEADS * HEAD_CAT, tm), lambda i: (i // per_b, 0, i % per_b)),
            pl.BlockSpec((tm, B_HEADS * HEAD_CAT), row),
            pl.BlockSpec((None, None, B_HEADS * V_ROWS_T, tm),
                         lambda i: (i // per_b, (i % per_b) // per_q, 0, i % per_q)),
        ],
        out_shape=[
            jax.ShapeDtypeStruct((bsz, B_HEADS * HEAD_CAT, seq), BF16),
            jax.ShapeDtypeStruct((t, B_HEADS * HEAD_CAT), BF16),
            jax.ShapeDtypeStruct((bsz, seq // tq, B_HEADS * V_ROWS_T, tq), BF16),
        ],
        compiler_params=pltpu.CompilerParams(
            dimension_semantics=("parallel",), vmem_limit_bytes=VMEM_LIMIT),
        name="l1proj",
    )(x, pos, inv, ng, mod, wa, gl, wbk, wbv, wqa, gq, wq, wqs)


def _attn_kernel(qt_ref, k_ref, vt_ref, o_ref, sa_ref, sb_ref, *, tq):
    qi = pl.program_id(2)

    def scores(j):
        rows = pl.ds(pl.multiple_of(j * tq, tq), tq)
        return _dot(k_ref[rows, :], qt_ref[...])

    def soft_pv(j, s, m, acc, masked=False):
        if masked:
            keys = lax.broadcasted_iota(jnp.int32, (tq, tq), 0)
            qrys = lax.broadcasted_iota(jnp.int32, (tq, tq), 1)
            s = jnp.where(keys <= qrys, s, -jnp.inf)
        m_new = jnp.maximum(m, jnp.max(s, axis=0, keepdims=True))
        p = jnp.exp2(s - m_new).astype(BF16)
        acc = jnp.exp2(m - m_new) * acc + _dot(vt_ref[j], p)
        return m_new, acc

    sa_ref[...] = scores(0)

    def pair(jj, carry):
        m, acc = carry
        j0 = 2 * jj
        sb_ref[...] = scores(j0 + 1)
        m, acc = soft_pv(j0, sa_ref[...], m, acc)
        sa_ref[...] = scores(j0 + 2)
        return soft_pv(j0 + 1, sb_ref[...], m, acc)

    init = (jnp.full((1, tq), -jnp.inf, F32), jnp.zeros((V_ROWS_T, tq), F32))
    m, acc = lax.fori_loop(0, qi // 2, pair, init)

    def odd_tail(m, acc):
        sb_ref[...] = scores(qi)
        m, acc = soft_pv(qi - 1, sa_ref[...], m, acc)
        return soft_pv(qi, sb_ref[...], m, acc, masked=True)

    def even_tail(m, acc):
        return soft_pv(qi, sa_ref[...], m, acc, masked=True)

    _, acc = lax.cond(qi % 2 == 1, odd_tail, even_tail, m, acc)
    o_ref[...] = (acc[0:B_V, :] * (1.0 / acc[B_V:B_V + 1, :])).astype(BF16)


def _attn(q_t, k, v_t, tq):
    b, s, _ = k.shape
    return pl.pallas_call(
        functools.partial(_attn_kernel, tq=tq),
        grid=(b, B_HEADS, s // tq),
        in_specs=[
            pl.BlockSpec((None, HEAD_CAT, tq), lambda bi, h, i: (bi, h, i)),
            pl.BlockSpec((None, s, HEAD_CAT), lambda bi, h, i: (bi, 0, h)),
            pl.BlockSpec((None, s // tq, V_ROWS_T, tq), lambda bi, h, i: (bi, 0, h, 0)),
        ],
        out_specs=pl.BlockSpec((None, B_V, tq), lambda bi, h, i: (bi, h, i)),
        out_shape=jax.ShapeDtypeStruct((b, B_HEADS * B_V, s), BF16),
        scratch_shapes=[pltpu.VMEM((tq, tq), F32), pltpu.VMEM((tq, tq), F32)],
        compiler_params=pltpu.CompilerParams(
            dimension_semantics=("parallel", "parallel", "parallel"),
            vmem_limit_bytes=VMEM_LIMIT),
        name="attn",
    )(q_t, k, v_t)


def _swap_halves(w):
    half = w.shape[-1] // 2
    return jnp.concatenate([w[..., half:], w[..., :half]], axis=-1)


def kernel(x, c, positions, ada_w, ada_b, norm_g, a_w_in, a_gate_b, a_head_g, a_w_out,
           kv_ada_w, kv_ada_b, kv_norm_g, kv_w_a, kv_latent_g, kv_w_b, b_w_q_a, b_q_latent_g,
           b_w_q_b, b_w_out, mlp_w1, mlp_w2):
    bsz, seq, d = x.shape
    t = bsz * seq
    x2d = x.reshape(t, d)

    c8 = jnp.pad(c, ((0, 8 - bsz), (0, 0)))
    ada = _adaln(c8, ada_w, ada_b[:, None, :], tn=1536)[:, :bsz]
    kv_ada = _adaln(c8, kv_ada_w[None], kv_ada_b[None, None, :], tn=1024)[0, :bsz]

    def mod_rows(vecs):
        rows = [v.reshape(bsz, 1, d) for v in vecs]
        rows += [jnp.zeros((bsz, 1, d), F32)] * (8 - len(rows))
        return jnp.concatenate(rows, axis=1)

    ada0 = [ada[0][:, i * d:(i + 1) * d] for i in range(6)]
    ada1 = [ada[1][:, i * d:(i + 1) * d] for i in range(6)]
    kv_shift, kv_scale = kv_ada[:, :d], kv_ada[:, d:]

    w_in = a_w_in[0]
    c0 = 2 * A_QK_W + A_V_W
    wk = w_in[:, A_QK_W:2 * A_QK_W].astype(BF16)
    wt = jnp.concatenate([
        w_in[:, :A_QK_W],
        w_in[:, 2 * A_QK_W:c0],
        w_in[:, c0 + 2 * A_HEADS:],
        w_in[:, c0:c0 + 2 * A_HEADS],
    ], axis=1).T.astype(BF16)
    k, qt, vt, ot, gt = _inproj(x2d, norm_g[0, 0][None], mod_rows(ada0[:2]), wk, wt, seq, tm=ROW_TM)
    gate_b = jnp.broadcast_to(a_gate_b[0].reshape(A_GATE_ROWS, 1), (A_GATE_ROWS, LANES))
    head_g = jnp.broadcast_to(a_head_g[0].reshape(A_V_W, 1), (A_V_W, MLSTM_CHUNK))
    bb, r, cm = _gates(gt, gate_b, MLSTM_CHUNK)
    mix0 = _mlstm(k, qt, vt, ot, bb, r, cm, head_g, MLSTM_CHUNK, tb=MLSTM_TB)
    w1_all, w2_all = mlp_w1.astype(BF16), mlp_w2.astype(BF16)
    x2d = _post(x2d, mix0, mod_rows(ada0), norm_g[0], a_w_out[0].astype(BF16), w1_all, w2_all, 0,
                seq, tm=ROW_TM)

    tq = ATTN_TQ
    rope = kv_w_a[:, B_KV_LORA:]
    wa_cat = jnp.concatenate([kv_w_a[:, :B_KV_LORA], rope, _swap_halves(rope)], axis=1).astype(BF16)
    wb = kv_w_b.reshape(B_KV_LORA, B_HEADS, B_NOPE + B_V)
    wbk = wb[:, :, :B_NOPE].reshape(B_KV_LORA, -1).astype(BF16)
    wbv = wb[:, :, B_NOPE:].reshape(B_KV_LORA, -1).T.astype(BF16)
    wq = b_w_q_b[0].T.astype(BF16)
    wq_rope = b_w_q_b[0].reshape(B_Q_LORA, B_HEADS, Q_HEAD)[:, :, B_NOPE:]
    wqs = _swap_halves(wq_rope).reshape(B_Q_LORA, -1).T.astype(BF16)
    inv = ROPE_THETA ** (-jnp.arange(ROPE_HALF, dtype=F32) / ROPE_HALF)
    inv_rep = jnp.broadcast_to(inv[:, None], (ROPE_HALF, ROW_TM))
    ng1 = jnp.concatenate([kv_norm_g[None], norm_g[1, 0][None]], axis=0)
    mod1 = mod_rows([kv_shift, kv_scale, ada1[0], ada1[1]])
    q_t, k_cat, v_t = _l1proj(
        x2d, positions.reshape(bsz, 1, seq), inv_rep, ng1, mod1, wa_cat, kv_latent_g[None], wbk, wbv,
        b_w_q_a[0].astype(BF16), b_q_latent_g[0][None], wq, wqs, seq, tm=ROW_TM, tq=tq)
    o = _attn(q_t, k_cat.reshape(bsz, seq, -1), v_t, tq=tq)
    x2d = _post(x2d, o, mod_rows(ada1), norm_g[1], b_w_out[0].astype(BF16), w1_all, w2_all, 1,
                seq, tm=ROW_TM)
    return x2d.reshape(bsz, seq, d)
```
